```python
import jax, jax.numpy as jnp
from jax import lax
import numpy as np

D_MODEL = 1024
BATCH = 8
SEQ = 8192
DEPTH = 2
DEC_BATCH = 32
DEC_SEQ = 64
PAST_LEN = 1024

CHUNK = 64
MIX_WIDTH = D_MODEL
A_WIDTH = MIX_WIDTH // 2
R_WIDTH = MIX_WIDTH - A_WIDTH
A_HEADS = 4
A_DK = A_WIDTH // A_HEADS
A_DV = A_DK
R_HEADS = 4
R_DK = R_WIDTH // R_HEADS
R_DV = R_DK
IN_SIZES = (A_WIDTH, A_WIDTH, A_WIDTH, A_WIDTH, R_WIDTH, R_WIDTH, R_WIDTH, R_WIDTH)
IN_COLS = 4 * A_WIDTH + 4 * R_WIDTH
N_EXPERTS = 16
N_GROUPS = 4
EXP_PER_GROUP = N_EXPERTS // N_GROUPS
TOP_K = 2
D_EXPERT = D_MODEL
MOE_BLOCK = 128
ROPE_BASE = 10000.0
EPS = 1e-6

kernel_name = 'hymba_hgrn2_retnet_grouped_moe_stream_step'


def rmsnorm(x, g):
    xf = x.astype(jnp.float32)
    y = xf * lax.rsqrt(jnp.mean(xf * xf, axis=-1, keepdims=True) + EPS)
    return (y * g.astype(jnp.float32)).astype(x.dtype)


def modulate(x, g, shift, scale):
    return rmsnorm(x, g) * (1.0 + scale[:, None, :]) + shift[:, None, :]


def _split_heads(t, n_heads):
    b, s, w = t.shape
    return t.reshape(b, s, n_heads, w // n_heads)


def _to_chunks(t, L):
    b, s, h, d = t.shape
    return t.reshape(b, s // L, L, h, d).transpose(1, 0, 3, 2, 4)


def _from_chunks(t):
    n, b, h, L, d = t.shape
    return t.transpose(1, 0, 3, 2, 4).reshape(b, n * L, h, d)


def rotary(x, offset):
    seq, d = x.shape[1], x.shape[-1]
    half = d // 2
    freq = ROPE_BASE ** (-jnp.arange(half, dtype=jnp.float32) / half)
    ang = (jnp.arange(seq, dtype=jnp.float32) + offset)[:, None] * freq[None, :]
    cos = jnp.cos(ang)[None, :, None, :]
    sin = jnp.sin(ang)[None, :, None, :]
    xf = x.astype(jnp.float32)
    x1, x2 = xf[..., :half], xf[..., half:]
    return jnp.concatenate([x1 * cos - x2 * sin, x2 * cos + x1 * sin], axis=-1)


def hgrn2_mix(q, f_logit, v, lb, state0):
    seq = q.shape[1]
    L = min(CHUNK, seq)
    lbh = lb.astype(jnp.float32).reshape(A_HEADS, A_DK)
    log_f = jnp.logaddexp(jnp.log(lbh), jnp.log1p(-lbh) + jax.nn.log_sigmoid(f_logit.astype(jnp.float32)))
    k = 1.0 - jnp.exp(log_f)
    causal = jnp.tril(jnp.ones((L, L), dtype=bool))[:, :, None]

    def step(S, blk):
        qb, kb, vb, lfb = blk
        b = jnp.cumsum(lfb, axis=2)
        diff = b[:, :, :, None, :] - b[:, :, None, :, :]
        decay = jnp.exp(jnp.where(causal, diff, -jnp.inf))
        scores = jnp.einsum('bhtd,bhsd,bhtsd->bhts', qb, kb, decay)
        o = (jnp.einsum('bhts,bhsv->bhtv', scores, vb)
             + jnp.einsum('bhtd,bhdv->bhtv', qb * jnp.exp(b), S))
        b_last = b[:, :, -1:, :]
        S = (jnp.exp(b_last[:, :, 0, :])[..., None] * S
             + jnp.einsum('bhsd,bhsv->bhdv', kb * jnp.exp(b_last - b), vb))
        return S, o

    blocks = tuple(_to_chunks(t, L) for t in (q.astype(jnp.float32), k, v.astype(jnp.float32), log_f))
    S, o = lax.scan(step, state0, blocks)
    return _from_chunks(o), S


def retention_mix(q, k, v, state0):
    seq = q.shape[1]
    L = min(CHUNK, seq)
    log_g = jnp.log(1.0 - 2.0 ** (-5.0 - jnp.arange(R_HEADS, dtype=jnp.float32)))
    idx = jnp.arange(L, dtype=jnp.float32)
    rel = idx[:, None] - idx[None, :]
    intra = jnp.where(rel >= 0, jnp.exp(jnp.maximum(rel, 0.0)[None] * log_g[:, None, None]), 0.0)
    inter = jnp.exp((idx[None, :] + 1.0) * log_g[:, None])[..., None]
    kdec = jnp.exp((L - 1.0 - idx[None, :]) * log_g[:, None])[..., None]
    sdec = jnp.exp(L * log_g)[:, None, None]

    def step(S, blk):
        qb, kb, vb = blk
        scores = jnp.einsum('bhtd,bhsd->bhts', qb, kb) * intra
        o = jnp.einsum('bhts,bhsv->bhtv', scores, vb) + jnp.einsum('bhtd,bhdv->bhtv', qb, S) * inter
        S = sdec * S + jnp.einsum('bhsd,bhsv->bhdv', kb * kdec, vb)
        return S, o

    blocks = tuple(_to_chunks(t.astype(jnp.float32), L) for t in (q, k, v))
    S, o = lax.scan(step, state0, blocks)
    return _from_chunks(o), S


def mixer_block(h, s_a, s_r, offset, w_in, w_out, lb, g_a, g_r):
    b, s, _ = h.shape
    proj = h @ w_in
    aq, af, ai, ag, rq, rk, rv, rg = jnp.split(proj, np.cumsum(IN_SIZES)[:-1].tolist(), axis=-1)
    oa, s_a = hgrn2_mix(_split_heads(aq, A_HEADS), _split_heads(af, A_HEADS), _split_heads(ai, A_HEADS), lb, s_a)
    oa = oa * lax.rsqrt(jnp.mean(oa * oa, axis=-1, keepdims=True) + EPS)
    oa = oa * g_a.astype(jnp.float32).reshape(A_HEADS, A_DV) * jax.nn.sigmoid(_split_heads(ag, A_HEADS).astype(jnp.float32))
    qr = rotary(_split_heads(rq, R_HEADS), offset)
    kr = rotary(_split_heads(rk, R_HEADS), offset) * (R_DK ** -0.5)
    orr, s_r = retention_mix(qr, kr, _split_heads(rv, R_HEADS), s_r)
    mu = jnp.mean(orr, axis=-1, keepdims=True)
    var = jnp.mean(jnp.square(orr - mu), axis=-1, keepdims=True)
    orr = (orr - mu) * lax.rsqrt(var + EPS) * g_r.astype(jnp.float32).reshape(R_HEADS, R_DV)
    orr = orr * jax.nn.silu(_split_heads(rg, R_HEADS).astype(jnp.float32))
    o = jnp.concatenate([oa.reshape(b, s, A_WIDTH), orr.reshape(b, s, R_WIDTH)], axis=-1).astype(h.dtype)
    return o @ w_out, s_a, s_r


def moe_ffn(h, w_router, b_router, w1, w3, w2):
    b, s, d = h.shape
    n_tok = b * s
    ht = h.reshape(n_tok, d)
    scores = jax.nn.sigmoid((ht @ w_router).astype(jnp.float32))
    biased = (scores + b_router.astype(jnp.float32)).reshape(n_tok, N_GROUPS, EXP_PER_GROUP)
    grp_score = jnp.sum(lax.top_k(biased, TOP_K)[0], axis=-1)
    g_sel = jnp.argmax(grp_score, axis=-1)
    in_grp = jnp.take_along_axis(biased, g_sel[:, None, None], axis=1)[:, 0]
    _, local = lax.top_k(in_grp, TOP_K)
    idx = (g_sel[:, None] * EXP_PER_GROUP + local).astype(jnp.int32)
    w = jnp.take_along_axis(scores, idx, axis=-1)
    w = w / jnp.sum(w, axis=-1, keepdims=True)
    n_asg = n_tok * TOP_K
    flat_e = idx.reshape(-1)
    order = jnp.argsort(flat_e)
    sorted_e = flat_e[order]
    counts = jnp.bincount(flat_e, length=N_EXPERTS)
    padded = (counts + MOE_BLOCK - 1) // MOE_BLOCK * MOE_BLOCK
    pad_end = jnp.cumsum(padded)
    pad_start = pad_end - padded
    start = jnp.cumsum(counts) - counts
    dest = pad_start[sorted_e] + jnp.arange(n_asg, dtype=jnp.int32) - start[sorted_e]
    n_blocks = -(-n_asg // MOE_BLOCK) + N_EXPERTS
    n_rows = n_blocks * MOE_BLOCK
    row_tok = jnp.zeros((n_rows,), jnp.int32).at[dest].set((order // TOP_K).astype(jnp.int32))
    row_w = jnp.zeros((n_rows,), jnp.float32).at[dest].set(w.reshape(-1)[order])
    block_e = jnp.minimum(jnp.searchsorted(pad_end, jnp.arange(n_blocks, dtype=jnp.int32) * MOE_BLOCK, side='right'),
                          N_EXPERTS - 1)
    xb = ht[row_tok].reshape(n_blocks, MOE_BLOCK, d)

    def expert(args):
        xe, e = args
        hid = jax.nn.silu(xe @ w1[e]) * (xe @ w3[e])
        return hid @ w2[e]

    yb = lax.map(expert, (xb, block_e)).reshape(n_rows, d)
    y = jax.ops.segment_sum(yb.astype(jnp.float32) * row_w[:, None], row_tok, num_segments=n_tok)
    return y.astype(h.dtype).reshape(b, s, d)


def run_trunk(x, c, states_a, states_r, offset, w_in, w_out, lb_all, hgrn_norm, ret_norm,
              norm_attn, norm_moe, w_ada, b_ada, w_router, b_router, w1, w3, w2, norm_f):
    new_a, new_r = [], []
    cs = jax.nn.silu(c)
    for l in range(DEPTH):
        mod = cs @ w_ada[l] + b_ada[l]
        sh1, sc1, g1, sh2, sc2, g2 = jnp.split(mod, 6, axis=-1)
        out, s_a, s_r = mixer_block(modulate(x, norm_attn[l], sh1, sc1), states_a[l], states_r[l], offset,
                                    w_in[l], w_out[l], lb_all[l], hgrn_norm[l], ret_norm[l])
        x = x + g1[:, None, :] * out
        x = x + g2[:, None, :] * moe_ffn(modulate(x, norm_moe[l], sh2, sc2), w_router, b_router, w1[l], w3[l], w2[l])
        new_a.append(s_a.astype(x.dtype))
        new_r.append(s_r.astype(x.dtype))
    return rmsnorm(x, norm_f), jnp.stack(new_a), jnp.stack(new_r)


def setup_inputs(seed: int = 0) -> dict:
    key = jax.random.key(seed)
    ks = jax.random.split(key, 24)
    nrm = lambda k, shape, s: jax.random.normal(k, shape, jnp.float32) * s
    D = D_MODEL
    return {
        'x_prompt': nrm(ks[0], (BATCH, SEQ, D), 1.0),
        'x_sample': nrm(ks[1], (DEC_BATCH, DEC_SEQ, D), 1.0),
        'state_hgrn': nrm(ks[2], (DEPTH, DEC_BATCH, A_HEADS, A_DK, A_DV), 0.5),
        'state_ret': nrm(ks[3], (DEPTH, DEC_BATCH, R_HEADS, R_DK, R_DV), 0.5),
        'c_prompt': nrm(ks[4], (BATCH, D), 1.0),
        'c_sample': nrm(ks[5], (DEC_BATCH, D), 1.0),
        'w_in': nrm(ks[6], (DEPTH, D, IN_COLS), D ** -0.5),
        'w_out': nrm(ks[7], (DEPTH, MIX_WIDTH, D), MIX_WIDTH ** -0.5),
        'lb_logits': nrm(ks[8], (DEPTH, A_WIDTH), 1.0),
        'hgrn_norm': 1.0 + nrm(ks[9], (DEPTH, A_WIDTH), 0.1),
        'ret_norm': 1.0 + nrm(ks[10], (DEPTH, R_WIDTH), 0.1),
        'norm_attn': 1.0 + nrm(ks[11], (DEPTH, D), 0.1),
        'norm_moe': 1.0 + nrm(ks[12], (DEPTH, D), 0.1),
        'w_ada': nrm(ks[13], (DEPTH, D, 6 * D), 0.5 * D ** -0.5),
        'b_ada': nrm(ks[14], (DEPTH, 6 * D), 0.02),
        'w_router': nrm(ks[15], (D, N_EXPERTS), D ** -0.5),
        'b_router': nrm(ks[16], (N_EXPERTS,), 0.01),
        'w1': nrm(ks[17], (DEPTH, N_EXPERTS, D, D_EXPERT), D ** -0.5),
        'w3': nrm(ks[18], (DEPTH, N_EXPERTS, D, D_EXPERT), D ** -0.5),
        'w2': nrm(ks[19], (DEPTH, N_EXPERTS, D_EXPERT, D), D_EXPERT ** -0.5),
        'norm_f': 1.0 + nrm(ks[20], (D,), 0.1),
    }


def reference(x_prompt, x_sample, state_hgrn, state_ret, c_prompt, c_sample, w_in, w_out, lb_logits,
              hgrn_norm, ret_norm, norm_attn, norm_moe, w_ada, b_ada, w_router, b_router, w1, w3, w2, norm_f):
    lb_cum = jnp.cumsum(jax.nn.softmax(lb_logits.astype(jnp.float32), axis=0), axis=0)
    lb_all = lb_cum - lb_cum[0:1]
    zeros_a = [jnp.zeros((BATCH, A_HEADS, A_DK, A_DV), jnp.float32) for _ in range(DEPTH)]
    zeros_r = [jnp.zeros((BATCH, R_HEADS, R_DK, R_DV), jnp.float32) for _ in range(DEPTH)]
    y_prompt, sa_prompt, sr_prompt = run_trunk(
        x_prompt, c_prompt, zeros_a, zeros_r, 0, w_in, w_out, lb_all, hgrn_norm, ret_norm,
        norm_attn, norm_moe, w_ada, b_ada, w_router, b_router, w1, w3, w2, norm_f)
    st_a = [state_hgrn[l].astype(jnp.float32) for l in range(DEPTH)]
    st_r = [state_ret[l].astype(jnp.float32) for l in range(DEPTH)]
    y_sample, sa_sample, sr_sample = run_trunk(
        x_sample, c_sample, st_a, st_r, PAST_LEN, w_in, w_out, lb_all, hgrn_norm, ret_norm,
        norm_attn, norm_moe, w_ada, b_ada, w_router, b_router, w1, w3, w2, norm_f)
    return (y_prompt, y_sample, sa_prompt, sr_prompt, sa_sample, sr_sample)
```

```python
import functools

import jax
import jax.numpy as jnp
from jax import lax
from jax.experimental import pallas as pl
from jax.experimental.pallas import tpu as pltpu

F32 = jnp.float32
BF16 = jnp.bfloat16
HIGHEST = lax.Precision.HIGHEST

D_MODEL = 1024
DEPTH = 2
N_HEADS = 4
D_HEAD = 128
G_WIDTH = N_HEADS * D_HEAD
IN_COLS = 8 * G_WIDTH
CHUNK = 64
SUB = 8
LEVELS = (32, 16, 8)
N_EXPERTS = 16
EXP_PER_GROUP = 4
N_GROUPS = 4
PAIRS_PER_GROUP = 6
N_COMBO = N_GROUPS * PAIRS_PER_GROUP
COMBO_ROWS = 32
ROUTE_ROWS = 8
EXT = 128
ROW_TILE = 512
MOE_BLOCK = 256
ADA_TILE = 1536
ROPE_BASE = 10000.0
PAST_LEN = 1024
EPS = 1e-6
VMEM_LIMIT = 56 * 1024 * 1024

_NT = (((1,), (1,)), ((), ()))
_TN = (((0,), (0,)), ((), ()))


def _dot(a, b):
    return jnp.dot(a, b, preferred_element_type=F32)


def _dot_nt(a, b):
    return lax.dot_general(a, b, _NT, preferred_element_type=F32)


def _dot_tn(a, b):
    return lax.dot_general(a, b, _TN, preferred_element_type=F32)


def _tiling(batch, seq):
    tq = min(seq, ROW_TILE)
    bb = ROW_TILE // tq
    assert tq % CHUNK == 0 and bb * tq == ROW_TILE and batch % bb == 0 and seq % tq == 0
    return bb, tq


def _ada_kernel(c_ref, w_ref, b_ref, o_ref):
    c = c_ref[...]
    cs = c * jax.nn.sigmoid(c)
    o_ref[0] = _dot(cs.astype(BF16), w_ref[0].astype(BF16)) + b_ref[0]


def _ada(c_all, w_ada, b_ada):
    m = c_all.shape[0]
    n = w_ada.shape[-1]
    return pl.pallas_call(
        _ada_kernel,
        grid=(DEPTH, n // ADA_TILE),
        in_specs=[
            pl.BlockSpec((m, D_MODEL), lambda l, j: (0, 0)),
            pl.BlockSpec((1, D_MODEL, ADA_TILE), lambda l, j: (l, 0, j)),
            pl.BlockSpec((1, 1, ADA_TILE), lambda l, j: (l, 0, j)),
        ],
        out_specs=pl.BlockSpec((1, m, ADA_TILE), lambda l, j: (l, 0, j)),
        out_shape=jax.ShapeDtypeStruct((DEPTH, m, n), F32),
        compiler_params=pltpu.CompilerParams(vmem_limit_bytes=VMEM_LIMIT),
        name="ada",
    )(c_all, w_ada, b_ada.reshape(DEPTH, 1, n))


def _route(lt, br):
    sc = jax.nn.sigmoid(lt)
    bi = sc + br
    s = [sc[e:e + 1] for e in range(N_EXPERTS)]
    v = [bi[e:e + 1] for e in range(N_EXPERTS)]
    gs = []
    for g in range(N_GROUPS):
        a = v[EXP_PER_GROUP * g:EXP_PER_GROUP * (g + 1)]
        m = a[0] + a[1]
        for i in range(EXP_PER_GROUP):
            for j in range(i + 1, EXP_PER_GROUP):
                if (i, j) != (0, 1):
                    m = jnp.maximum(m, a[i] + a[j])
        gs.append(m)
    best = gs[0]
    gsel = jnp.zeros_like(best)
    for g in range(1, N_GROUPS):
        upd = gs[g] > best
        best = jnp.where(upd, gs[g], best)
        gsel = jnp.where(upd, float(g), gsel)

    def pick(vals, key, n):
        out = vals[0]
        for g in range(1, n):
            out = jnp.where(key == float(g), vals[g], out)
        return out

    vv = [pick([v[EXP_PER_GROUP * g + j] for g in range(N_GROUPS)], gsel, N_GROUPS) for j in range(EXP_PER_GROUP)]
    ss = [pick([s[EXP_PER_GROUP * g + j] for g in range(N_GROUPS)], gsel, N_GROUPS) for j in range(EXP_PER_GROUP)]
    m1 = vv[0]
    i1 = jnp.zeros_like(m1)
    for j in range(1, EXP_PER_GROUP):
        upd = vv[j] > m1
        m1 = jnp.where(upd, vv[j], m1)
        i1 = jnp.where(upd, float(j), i1)
    m2 = jnp.full_like(m1, -jnp.inf)
    i2 = jnp.zeros_like(m1)
    for j in range(EXP_PER_GROUP):
        upd = (i1 != float(j)) & (vv[j] > m2)
        m2 = jnp.where(upd, vv[j], m2)
        i2 = jnp.where(upd, float(j), i2)
    s1 = pick(ss, i1, EXP_PER_GROUP)
    s2 = pick(ss, i2, EXP_PER_GROUP)
    den = s1 + s2
    w1 = s1 / den
    w2 = s2 / den
    first_lo = i1 < i2
    lo = jnp.where(first_lo, i1, i2)
    hi = jnp.where(first_lo, i2, i1)
    w_lo = jnp.where(first_lo, w1, w2)
    w_hi = jnp.where(first_lo, w2, w1)
    pair = jnp.where(lo == 0.0, hi - 1.0, jnp.where(lo == 1.0, hi + 1.0, 5.0))
    combo = gsel * float(PAIRS_PER_GROUP) + pair
    zero = jnp.zeros_like(combo)
    return jnp.concatenate([combo, w_lo, w_hi] + [zero] * (ROUTE_ROWS - 3), axis=0)


def _mixer_kernel(x_ref, sh1_ref, sc1_ref, g1_ref, sh2_ref, sc2_ref, na_ref, nm_ref, win_ref, wout_ref,
                  lb_ref, ga_ref, gr_ref, wrt_ref, br_ref, cos_ref, sin_ref, intra_ref, inter_ref, kdec_ref,
                  sdec_ref, ones_ref, tri_ref, sa0_ref, sr0_ref,
                  x1_ref, h2e_ref, route_ref, sa_ref, sr_ref,
                  proj_ref, ob_ref, sat_ref, *, bb_n, tq):
    tt = pl.program_id(1)
    rows = bb_n * tq
    n_chunks = tq // CHUNK

    x = x_ref[...]
    ms = jnp.mean(x * x, axis=-1, keepdims=True)
    h = x * lax.rsqrt(ms + EPS) * na_ref[...] * (1.0 + sc1_ref[...]) + sh1_ref[...]
    proj_ref[...] = _dot(h.reshape(rows, D_MODEL).astype(BF16), win_ref[...])

    @pl.when(tt == 0)
    def _():
        for b in range(bb_n):
            for hh in range(N_HEADS):
                sat_ref[b, hh] = sa0_ref[b, hh].T
        sr_ref[...] = sr0_ref[...]

    lb = lb_ref[...]
    log_lb = jnp.log(lb)
    log_1mlb = jnp.log1p(-lb)
    ga = ga_ref[...]
    gr = gr_ref[...]
    tri = tri_ref[...]
    t_i = lax.broadcasted_iota(jnp.int32, (CHUNK, CHUNK), 0)
    s_i = lax.broadcasted_iota(jnp.int32, (CHUNK, CHUNK), 1)
    level_masks = []
    for m in LEVELS:
        sh = m.bit_length() - 1
        level_masks.append(((t_i >> (sh + 1)) == (s_i >> (sh + 1)))
                           & (((t_i >> sh) & 1) == 1) & (((s_i >> sh) & 1) == 0))
    diag_mask = ((t_i >> 3) == (s_i >> 3)) & (s_i <= t_i)

    def chunk_body(idx, carry):
        b = idx // n_chunks
        c = idx - b * n_chunks
        r0 = pl.multiple_of(idx * CHUNK, CHUNK)
        t0 = pl.multiple_of(c * CHUNK, CHUNK)
        rs = pl.ds(r0, CHUNK)

        q = proj_ref[rs, 0 * G_WIDTH:1 * G_WIDTH]
        z = proj_ref[rs, 1 * G_WIDTH:2 * G_WIDTH]
        vi = proj_ref[rs, 2 * G_WIDTH:3 * G_WIDTH]
        ag = proj_ref[rs, 3 * G_WIDTH:4 * G_WIDTH]
        log_sig = jnp.minimum(z, 0.0) - jnp.log1p(jnp.exp(-jnp.abs(z)))
        cc = log_1mlb + log_sig
        log_f = jnp.maximum(log_lb, cc) + jnp.log1p(jnp.exp(-jnp.abs(log_lb - cc)))
        kk = 1.0 - jnp.exp(log_f)
        bcum = jnp.dot(tri, log_f, precision=HIGHEST, preferred_element_type=F32)
        b_last = bcum[CHUNK - 1:CHUNK, :]
        q_in = (q * jnp.exp(bcum)).astype(BF16)
        k_out = (kk * jnp.exp(b_last - bcum)).astype(BF16)
        e_last = jnp.exp(b_last)
        vb = vi.astype(BF16)
        q_lv, k_lv = [], []
        for m in LEVELS:
            ref_row = bcum.reshape(CHUNK // (2 * m), 2 * m, G_WIDTH)[:, m - 1:m, :]
            ref_row = jnp.broadcast_to(ref_row, (CHUNK // (2 * m), 2 * m, G_WIDTH)).reshape(CHUNK, G_WIDTH)
            e_m = jnp.exp(-jnp.abs(bcum - ref_row))
            q_lv.append((q * e_m).astype(BF16))
            k_lv.append((kk * e_m).astype(BF16))
        b3 = bcum.reshape(CHUNK // SUB, SUB, G_WIDTH)
        k3 = kk.reshape(CHUNK // SUB, SUB, G_WIDTH)
        p_sg = []
        for sg in range(SUB):
            bs = jnp.broadcast_to(b3[:, sg:sg + 1, :], b3.shape).reshape(CHUNK, G_WIDTH)
            ks = jnp.broadcast_to(k3[:, sg:sg + 1, :], k3.shape).reshape(CHUNK, G_WIDTH)
            p_sg.append((q * ks * jnp.exp(jnp.minimum(bcum - bs, 0.0))).astype(BF16))
        for hh in range(N_HEADS):
            sl = slice(hh * D_HEAD, (hh + 1) * D_HEAD)
            p_cat = jnp.concatenate([p[:, sl] for p in p_sg], axis=1)
            a = jnp.where(diag_mask, _dot(p_cat, ones_ref[...]), 0.0)
            for li in range(len(LEVELS)):
                a = jnp.where(level_masks[li], _dot_nt(q_lv[li][:, sl], k_lv[li][:, sl]), a)
            st = sat_ref[b, hh]
            o = _dot(a.astype(BF16), vb[:, sl]) + _dot_nt(q_in[:, sl], st.astype(BF16))
            sat_ref[b, hh] = st * e_last[:, sl] + _dot_tn(vb[:, sl], k_out[:, sl])
            o = o * lax.rsqrt(jnp.mean(o * o, axis=-1, keepdims=True) + EPS)
            o = o * ga[:, sl] * jax.nn.sigmoid(ag[:, sl])
            ob_ref[rs, sl] = o.astype(BF16)

        rq = proj_ref[rs, 4 * G_WIDTH:5 * G_WIDTH]
        rk = proj_ref[rs, 5 * G_WIDTH:6 * G_WIDTH]
        rv = proj_ref[rs, 6 * G_WIDTH:7 * G_WIDTH]
        rg = proj_ref[rs, 7 * G_WIDTH:8 * G_WIDTH]
        cs = cos_ref[pl.ds(t0, CHUNK), :]
        sn = sin_ref[pl.ds(t0, CHUNK), :]
        for hh in range(N_HEADS):
            sl = slice(hh * D_HEAD, (hh + 1) * D_HEAD)
            qh = rq[:, sl]
            kh = rk[:, sl]
            qr = qh * cs + pltpu.roll(qh, D_HEAD // 2, 1) * sn
            kr = (kh * cs + pltpu.roll(kh, D_HEAD // 2, 1) * sn) * (D_HEAD ** -0.5)
            qb = qr.astype(BF16)
            vh = rv[:, sl].astype(BF16)
            scores = _dot_nt(qb, kr.astype(BF16)) * intra_ref[hh]
            s_r = sr_ref[b, hh]
            o = _dot(scores.astype(BF16), vh) + _dot(qb, s_r.astype(BF16)) * inter_ref[hh]
            sr_ref[b, hh] = s_r * sdec_ref[hh] + _dot_tn((kr * kdec_ref[hh]).astype(BF16), vh)
            mu = jnp.mean(o, axis=-1, keepdims=True)
            oc = o - mu
            var = jnp.mean(oc * oc, axis=-1, keepdims=True)
            o = oc * lax.rsqrt(var + EPS) * gr[:, sl]
            g = rg[:, sl]
            o = o * (g * jax.nn.sigmoid(g))
            ob_ref[rs, slice(G_WIDTH + hh * D_HEAD, G_WIDTH + (hh + 1) * D_HEAD)] = o.astype(BF16)
        return carry

    lax.fori_loop(0, bb_n * n_chunks, chunk_body, 0)

    @pl.when(tt == pl.num_programs(1) - 1)
    def _():
        for b in range(bb_n):
            for hh in range(N_HEADS):
                sa_ref[b, hh] = sat_ref[b, hh].T

    out = _dot(ob_ref[...], wout_ref[...]).reshape(bb_n, tq, D_MODEL)
    x1 = x + g1_ref[...] * out
    x1_ref[...] = x1
    ms2 = jnp.mean(x1 * x1, axis=-1, keepdims=True)
    h2 = x1 * lax.rsqrt(ms2 + EPS) * nm_ref[...] * (1.0 + sc2_ref[...]) + sh2_ref[...]
    h2e_ref[:, :, 0:D_MODEL] = h2
    lt = lax.dot_general(wrt_ref[...], h2.reshape(rows, D_MODEL), _NT, precision=HIGHEST,
                         preferred_element_type=F32)
    route = _route(lt, br_ref[...])
    route_ref[0] = route
    rec = jnp.concatenate([route, jnp.zeros((EXT - ROUTE_ROWS, rows), F32)], axis=0).T
    h2e_ref[:, :, D_MODEL:D_MODEL + EXT] = rec.reshape(bb_n, tq, EXT)


def _mixer(x, mods, consts, lw, s_a0, s_r0, offset):
    batch, seq, _ = x.shape
    bb_n, tq = _tiling(batch, seq)
    nb, nt = batch // bb_n, seq // tq
    cos, sin = _rope_tables(seq, offset)
    sh1, sc1, g1, sh2, sc2 = mods

    def per_b(shape):
        return pl.BlockSpec(shape, lambda b, t: (b,) + (0,) * (len(shape) - 1))

    def const(shape):
        return pl.BlockSpec(shape, lambda b, t: (0,) * len(shape))

    mod_spec = per_b((bb_n, 1, D_MODEL))
    state_spec = per_b((bb_n, N_HEADS, D_HEAD, D_HEAD))
    in_specs = [
        pl.BlockSpec((bb_n, tq, D_MODEL), lambda b, t: (b, t, 0)),
        mod_spec, mod_spec, mod_spec, mod_spec, mod_spec,
        const((1, D_MODEL)), const((1, D_MODEL)),
        const((D_MODEL, IN_COLS)), const((2 * G_WIDTH, D_MODEL)),
        const((1, G_WIDTH)), const((1, G_WIDTH)), const((1, G_WIDTH)),
        const((N_EXPERTS, D_MODEL)), const((N_EXPERTS, 1)),
        pl.BlockSpec((tq, D_HEAD), lambda b, t: (t, 0)), pl.BlockSpec((tq, D_HEAD), lambda b, t: (t, 0)),
        const((N_HEADS, CHUNK, CHUNK)), const((N_HEADS, CHUNK, D_HEAD)), const((N_HEADS, CHUNK, D_HEAD)),
        const((N_HEADS, 1, D_HEAD)), const((SUB * D_HEAD, CHUNK)), const((CHUNK, CHUNK)),
        state_spec, state_spec,
    ]
    out_specs = [
        pl.BlockSpec((bb_n, tq, D_MODEL), lambda b, t: (b, t, 0)),
        pl.BlockSpec((bb_n, tq, D_MODEL + EXT), lambda b, t: (b, t, 0)),
        pl.BlockSpec((1, ROUTE_ROWS, ROW_TILE), lambda b, t: (b * nt + t, 0, 0)),
        state_spec, state_spec,
    ]
    out_shape = [
        jax.ShapeDtypeStruct((batch, seq, D_MODEL), F32),
        jax.ShapeDtypeStruct((batch, seq, D_MODEL + EXT), F32),
        jax.ShapeDtypeStruct((nb * nt, ROUTE_ROWS, ROW_TILE), F32),
        jax.ShapeDtypeStruct((batch, N_HEADS, D_HEAD, D_HEAD), F32),
        jax.ShapeDtypeStruct((batch, N_HEADS, D_HEAD, D_HEAD), F32),
    ]
    return pl.pallas_call(
        functools.partial(_mixer_kernel, bb_n=bb_n, tq=tq),
        grid=(nb, nt),
        in_specs=in_specs,
        out_specs=out_specs,
        out_shape=out_shape,
        scratch_shapes=[
            pltpu.VMEM((ROW_TILE, IN_COLS), F32),
            pltpu.VMEM((ROW_TILE, 2 * G_WIDTH), BF16),
            pltpu.VMEM((bb_n, N_HEADS, D_HEAD, D_HEAD), F32),
        ],
        compiler_params=pltpu.CompilerParams(
            dimension_semantics=("arbitrary", "arbitrary"), vmem_limit_bytes=VMEM_LIMIT),
        name="mixer",
    )(x, sh1, sc1, g1, sh2, sc2, lw["na"], lw["nm"], lw["w_in"], lw["w_out"], lw["lb"], lw["ga"], lw["gr"],
      consts["wrt"], consts["br"], cos, sin, consts["intra"], consts["inter"], consts["kdec"], consts["sdec"],
      consts["ones"], consts["tri"], s_a0, s_r0)


def _rope_tables(seq, offset):
    half = D_HEAD // 2
    freq = ROPE_BASE ** (-jnp.arange(half, dtype=F32) / half)
    ang = (jnp.arange(seq, dtype=F32) + offset)[:, None] * freq[None, :]
    cos, sin = jnp.cos(ang), jnp.sin(ang)
    return jnp.concatenate([cos, cos], axis=-1), jnp.concatenate([-sin, sin], axis=-1)


def _retention_tables():
    log_g = jnp.log(1.0 - 2.0 ** (-5.0 - jnp.arange(N_HEADS, dtype=F32)))
    idx = jnp.arange(CHUNK, dtype=F32)
    rel = idx[:, None] - idx[None, :]
    intra = jnp.where(rel >= 0, jnp.exp(jnp.maximum(rel, 0.0)[None] * log_g[:, None, None]), 0.0)
    inter = jnp.exp((idx[None, :] + 1.0) * log_g[:, None])[..., None]
    kdec = jnp.exp((CHUNK - 1.0 - idx[None, :]) * log_g[:, None])[..., None]
    sdec = jnp.exp(CHUNK * log_g)[:, None, None]
    wide = (N_HEADS, CHUNK, D_HEAD)
    return (intra, jnp.broadcast_to(inter, wide), jnp.broadcast_to(kdec, wide),
            jnp.broadcast_to(sdec, (N_HEADS, 1, D_HEAD)))


def _plan_kernel(route_ref, triu_ref, lstrict_ref, pos_ref, cnt_ref, carry_ref, base_ref):
    phase = pl.program_id(0)
    i = pl.program_id(1)
    combo = route_ref[0, 0:1, :]
    ids = lax.broadcasted_iota(jnp.int32, (COMBO_ROWS, ROW_TILE), 0).astype(F32)
    onehot = (ids == combo).astype(F32)
    tile_cnt = jnp.broadcast_to(jnp.sum(onehot, axis=1, keepdims=True), (COMBO_ROWS, 128))

    @pl.when((phase == 0) & (i == 0))
    def _():
        carry_ref[...] = jnp.zeros_like(carry_ref)

    @pl.when(phase == 0)
    def _():
        carry_ref[...] += tile_cnt

    @pl.when((phase == 1) & (i == 0))
    def _():
        cnt = carry_ref[...]
        cnt_ref[...] = cnt
        padded = jnp.floor((cnt + float(MOE_BLOCK - 1)) * (1.0 / MOE_BLOCK)) * float(MOE_BLOCK)
        base_ref[...] = jnp.dot(lstrict_ref[...], padded, precision=HIGHEST, preferred_element_type=F32)
        carry_ref[...] = jnp.zeros_like(carry_ref)

    @pl.when(phase == 1)
    def _():
        cum = _dot(onehot.astype(BF16), triu_ref[...])
        start = base_ref[:, 0:1] + carry_ref[:, 0:1]
        pos = jnp.sum(onehot * (cum - 1.0 + start), axis=0, keepdims=True)
        pos_ref[0] = jnp.broadcast_to(pos, (ROUTE_ROWS, ROW_TILE)).astype(jnp.int32)
        carry_ref[...] += tile_cnt


def _plan(route, consts):
    steps = route.shape[0]
    return pl.pallas_call(
        _plan_kernel,
        grid=(2, steps),
        in_specs=[
            pl.BlockSpec((1, ROUTE_ROWS, ROW_TILE), lambda p, i: (i, 0, 0)),
            pl.BlockSpec((ROW_TILE, ROW_TILE), lambda p, i: (0, 0)),
            pl.BlockSpec((COMBO_ROWS, COMBO_ROWS), lambda p, i: (0, 0)),
        ],
        out_specs=[
            pl.BlockSpec((1, ROUTE_ROWS, ROW_TILE), lambda p, i: (i * p, 0, 0)),
            pl.BlockSpec((COMBO_ROWS, 128), lambda p, i: (0, 0)),
        ],
        out_shape=[
            jax.ShapeDtypeStruct((steps, ROUTE_ROWS, ROW_TILE), jnp.int32),
            jax.ShapeDtypeStruct((COMBO_ROWS, 128), F32),
        ],
        scratch_shapes=[pltpu.VMEM((COMBO_ROWS, 128), F32), pltpu.VMEM((COMBO_ROWS, 128), F32)],
        compiler_params=pltpu.CompilerParams(dimension_semantics=("arbitrary", "arbitrary")),
        name="plan",
    )(route, consts["triu"], consts["lstrict"])


def _row_copy(src_ref, src_row, dst_ref, dst_row, sem):
    return pltpu.make_async_copy(src_ref.at[pl.ds(src_row, 1), :], dst_ref.at[pl.ds(dst_row, 1), :], sem)


def _scatter_kernel(pos_ref, h2e_ref, xs_in_ref, xs_ref, sem):
    del xs_in_ref
    base = pl.program_id(0) * ROW_TILE

    def issue(r, carry):
        _row_copy(h2e_ref, r, xs_ref, pos_ref[base + r], sem).start()
        return carry

    lax.fori_loop(0, ROW_TILE, issue, 0)
    pltpu.make_async_copy(h2e_ref, xs_ref.at[pl.ds(0, ROW_TILE), :], sem).wait()


def _scatter(pos, h2e, xs):
    n_tok = h2e.shape[0]
    width = h2e.shape[1]
    return pl.pallas_call(
        _scatter_kernel,
        grid_spec=pltpu.PrefetchScalarGridSpec(
            num_scalar_prefetch=1,
            grid=(n_tok // ROW_TILE,),
            in_specs=[
                pl.BlockSpec((ROW_TILE, width), lambda i, pos: (i, 0)),
                pl.BlockSpec(memory_space=pl.ANY),
            ],
            out_specs=pl.BlockSpec(memory_space=pl.ANY),
            scratch_shapes=[pltpu.SemaphoreType.DMA],
        ),
        out_shape=jax.ShapeDtypeStruct(xs.shape, xs.dtype),
        input_output_aliases={2: 0},
        compiler_params=pltpu.CompilerParams(dimension_semantics=("arbitrary",)),
        name="scatter",
    )(pos, h2e, xs)


def _moe_kernel(ea_ref, eb_ref, nu_ref, xs_ref, w1a, w3a, w2a, w1b, w3b, w2b, ys_ref):
    del ea_ref, eb_ref
    j = pl.program_id(0)

    @pl.when(j < nu_ref[0])
    def _():
        x = xs_ref[:, 0:D_MODEL].astype(BF16)
        w_lo = xs_ref[:, D_MODEL + 1:D_MODEL + 2]
        w_hi = xs_ref[:, D_MODEL + 2:D_MODEL + 3]

        def ffn(w1, w3, w2):
            h1 = _dot(x, w1[0])
            hid = (h1 * jax.nn.sigmoid(h1)) * _dot(x, w3[0])
            return _dot(hid.astype(BF16), w2[0])

        ys_ref[...] = ffn(w1a, w3a, w2a) * w_lo + ffn(w1b, w3b, w2b) * w_hi

    @pl.when(j >= nu_ref[0])
    def _():
        ys_ref[...] = jnp.zeros_like(ys_ref)


def _moe(ea, eb, n_used, xs, w1, w3, w2):
    n_rows, width = xs.shape
    n_blocks = n_rows // MOE_BLOCK

    def w_spec(which):
        if which == 0:
            return pl.BlockSpec((1, D_MODEL, D_MODEL), lambda j, ea, eb, nu: (ea[j], 0, 0))
        return pl.BlockSpec((1, D_MODEL, D_MODEL), lambda j, ea, eb, nu: (eb[j], 0, 0))

    return pl.pallas_call(
        _moe_kernel,
        grid_spec=pltpu.PrefetchScalarGridSpec(
            num_scalar_prefetch=3,
            grid=(n_blocks,),
            in_specs=[
                pl.BlockSpec((MOE_BLOCK, width), lambda j, ea, eb, nu: (jnp.minimum(j, nu[0] - 1), 0)),
                w_spec(0), w_spec(0), w_spec(0), w_spec(1), w_spec(1), w_spec(1),
            ],
            out_specs=pl.BlockSpec((MOE_BLOCK, D_MODEL), lambda j, ea, eb, nu: (j, 0)),
        ),
        out_shape=jax.ShapeDtypeStruct((n_rows, D_MODEL), F32),
        compiler_params=pltpu.CompilerParams(dimension_semantics=("arbitrary",), vmem_limit_bytes=VMEM_LIMIT),
        name="moe",
    )(ea, eb, n_used, xs, w1, w3, w2, w1, w3, w2)


def _combine_kernel(pos_ref, x1_ref, g2_ref, nf_ref, ys_ref, out_ref, buf_ref, sem, *, final):
    step = pl.program_id(0) * pl.num_programs(1) + pl.program_id(1)
    base = step * ROW_TILE

    def issue(r, carry):
        _row_copy(ys_ref, pos_ref[base + r], buf_ref, r, sem).start()
        return carry

    lax.fori_loop(0, ROW_TILE, issue, 0)
    pltpu.make_async_copy(ys_ref.at[pl.ds(0, ROW_TILE), :], buf_ref, sem).wait()
    x2 = x1_ref[...] + g2_ref[...] * buf_ref[...].reshape(x1_ref.shape)
    if final:
        x2 = x2 * lax.rsqrt(jnp.mean(x2 * x2, axis=-1, keepdims=True) + EPS) * nf_ref[...]
    out_ref[...] = x2


def _combine(pos, x1, g2, nf, ys, final):
    batch, seq, _ = x1.shape
    bb_n, tq = _tiling(batch, seq)
    tok_spec = pl.BlockSpec((bb_n, tq, D_MODEL), lambda b, t, pos: (b, t, 0))
    return pl.pallas_call(
        functools.partial(_combine_kernel, final=final),
        grid_spec=pltpu.PrefetchScalarGridSpec(
            num_scalar_prefetch=1,
            grid=(batch // bb_n, seq // tq),
            in_specs=[
                tok_spec,
                pl.BlockSpec((bb_n, 1, D_MODEL), lambda b, t, pos: (b, 0, 0)),
                pl.BlockSpec((1, D_MODEL), lambda b, t, pos: (0, 0)),
                pl.BlockSpec(memory_space=pl.ANY),
            ],
            out_specs=tok_spec,
            scratch_shapes=[pltpu.VMEM((ROW_TILE, D_MODEL), F32), pltpu.SemaphoreType.DMA],
        ),
        out_shape=jax.ShapeDtypeStruct(x1.shape, F32),
        compiler_params=pltpu.CompilerParams(dimension_semantics=("arbitrary", "arbitrary")),
        name="combine",
    )(pos, x1, g2, nf, ys)


def _block_table(cnt, n_blocks):
    counts = cnt[:N_COMBO, 0].astype(jnp.int32)
    padded = (counts + MOE_BLOCK - 1) // MOE_BLOCK * MOE_BLOCK
    pad_end = jnp.cumsum(padded)
    n_used = (pad_end[-1] // MOE_BLOCK).astype(jnp.int32)
    blk = jnp.minimum(jnp.arange(n_blocks, dtype=jnp.int32), n_used - 1)
    combo = jnp.minimum(jnp.searchsorted(pad_end, blk * MOE_BLOCK, side="right"), N_COMBO - 1).astype(jnp.int32)
    lo_tab = jnp.array([0, 0, 0, 1, 1, 2], jnp.int32)
    hi_tab = jnp.array([1, 2, 3, 2, 3, 3], jnp.int32)
    grp = combo // PAIRS_PER_GROUP
    pair = combo % PAIRS_PER_GROUP
    return grp * EXP_PER_GROUP + lo_tab[pair], grp * EXP_PER_GROUP + hi_tab[pair], n_used.reshape(1)


def kernel(x_prompt, x_sample, state_hgrn, state_ret, c_prompt, c_sample, w_in, w_out, lb_logits, hgrn_norm,
           ret_norm, norm_attn, norm_moe, w_ada, b_ada, w_router, b_router, w1, w3, w2, norm_f):
    bp, tp, _ = x_prompt.shape
    bs, ts, _ = x_sample.shape
    n_p, n_s = bp * tp, bs * ts
    n_tok = n_p + n_s
    n_blocks = n_tok // MOE_BLOCK + N_COMBO

    lb_cum = jnp.cumsum(jax.nn.softmax(lb_logits.astype(F32), axis=0), axis=0)
    lb_all = lb_cum - lb_cum[0:1]

    mod = _ada(jnp.concatenate([c_prompt, c_sample], axis=0), w_ada, b_ada)

    intra, inter, kdec, sdec = _retention_tables()
    col = jnp.arange(SUB * D_HEAD, dtype=jnp.int32)[:, None] // D_HEAD
    consts = {
        "wrt": w_router.T.astype(F32), "br": b_router.astype(F32).reshape(N_EXPERTS, 1),
        "intra": intra, "inter": inter, "kdec": kdec, "sdec": sdec,
        "ones": (col == (jnp.arange(CHUNK, dtype=jnp.int32)[None, :] % SUB)).astype(BF16),
        "tri": jnp.tril(jnp.ones((CHUNK, CHUNK), F32)),
        "triu": jnp.triu(jnp.ones((ROW_TILE, ROW_TILE), BF16)),
        "lstrict": jnp.tril(jnp.ones((COMBO_ROWS, COMBO_ROWS), F32), k=-1),
    }

    xp, xs_ = x_prompt, x_sample
    sa_p = jnp.zeros((bp, N_HEADS, D_HEAD, D_HEAD), F32)
    sr_p = jnp.zeros((bp, N_HEADS, D_HEAD, D_HEAD), F32)
    new_states = []
    for l in range(DEPTH):
        lw = {
            "na": norm_attn[l].reshape(1, D_MODEL), "nm": norm_moe[l].reshape(1, D_MODEL),
            "w_in": w_in[l].astype(BF16), "w_out": w_out[l].astype(BF16),
            "lb": lb_all[l].reshape(1, G_WIDTH), "ga": hgrn_norm[l].reshape(1, G_WIDTH),
            "gr": ret_norm[l].reshape(1, G_WIDTH),
        }
        parts = jnp.split(mod[l], 6, axis=-1)
        mods_p = [p[:bp, None, :] for p in parts]
        mods_s = [p[bp:, None, :] for p in parts]
        x1p, h2p, route_p, sa_np, sr_np = _mixer(xp, mods_p[:5], consts, lw, sa_p, sr_p, 0)
        x1s, h2s, route_s, sa_ns, sr_ns = _mixer(xs_, mods_s[:5], consts, lw, state_hgrn[l].astype(F32),
                                                 state_ret[l].astype(F32), PAST_LEN)
        new_states.append((sa_np, sr_np, sa_ns, sr_ns))

        pos3, cnt = _plan(jnp.concatenate([route_p, route_s], axis=0), consts)
        pos = pos3[:, 0, :].reshape(n_tok)
        ea, eb, n_used = _block_table(cnt, n_blocks)

        xs = jnp.zeros((n_blocks * MOE_BLOCK, D_MODEL + EXT), F32)
        xs = _scatter(pos[:n_p], h2p.reshape(n_p, D_MODEL + EXT), xs)
        xs = _scatter(pos[n_p:], h2s.reshape(n_s, D_MODEL + EXT), xs)
        ys = _moe(ea, eb, n_used, xs, w1[l].astype(BF16), w3[l].astype(BF16), w2[l].astype(BF16))

        final = l == DEPTH - 1
        nf = norm_f.reshape(1, D_MODEL)
        xp = _combine(pos[:n_p], x1p, mods_p[5], nf, ys, final)
        xs_ = _combine(pos[n_p:], x1s, mods_s[5], nf, ys, final)

    sa_prompt = jnp.stack([s[0] for s in new_states])
    sr_prompt = jnp.stack([s[1] for s in new_states])
    sa_sample = jnp.stack([s[2] for s in new_states])
    sr_sample = jnp.stack([s[3] for s in new_states])
    return (xp, xs_, sa_prompt, sr_prompt, sa_sample, sr_sample)
```

```python
import functools

import jax
import jax.numpy as jnp
from jax import lax
from jax.experimental import pallas as pl
from jax.experimental.pallas import tpu as pltpu

F32 = jnp.float32
BF16 = jnp.bfloat16
HIGHEST = lax.Precision.HIGHEST

D_MODEL = 1024
DEPTH = 2
N_HEADS = 4
D_HEAD = 128
G_WIDTH = N_HEADS * D_HEAD
IN_COLS = 8 * G_WIDTH
CHUNK = 64
CHUNK_UNROLL = 2
STAGE_GAP = (None,) * 3
SUB = 8
LEVELS = (32, 16, 8)
N_EXPERTS = 16
EXP_PER_GROUP = 4
N_GROUPS = 4
PAIRS_PER_GROUP = 6
N_COMBO = N_GROUPS * PAIRS_PER_GROUP
COMBO_ROWS = 32
ROUTE_ROWS = 8
EXT = 128
ROW_TILE = 512
MOE_BLOCK = 256
ADA_TILE = 1536
ROPE_BASE = 10000.0
PAST_LEN = 1024
EPS = 1e-6
LOG2E = 1.4426950408889634
VMEM_LIMIT = 56 * 1024 * 1024

_NT = (((1,), (1,)), ((), ()))
_TN = (((0,), (0,)), ((), ()))


def _dot(a, b):
    return jnp.dot(a, b, preferred_element_type=F32)


def _dot_nt(a, b):
    return lax.dot_general(a, b, _NT, preferred_element_type=F32)


def _dot_tn(a, b):
    return lax.dot_general(a, b, _TN, preferred_element_type=F32)


def _tiling(batch, seq):
    tq = min(seq, ROW_TILE)
    bb = ROW_TILE // tq
    assert tq % CHUNK == 0 and bb * tq == ROW_TILE and batch % bb == 0 and seq % tq == 0
    return bb, tq


def _ada_kernel(c_ref, w_ref, b_ref, o_ref):
    c = c_ref[...]
    cs = c * jax.nn.sigmoid(c)
    o_ref[0] = _dot(cs.astype(BF16), w_ref[0].astype(BF16)) + b_ref[0]


def _ada(c_all, w_ada, b_ada):
    m = c_all.shape[0]
    n = w_ada.shape[-1]
    return pl.pallas_call(
        _ada_kernel,
        grid=(DEPTH, n // ADA_TILE),
        in_specs=[
            pl.BlockSpec((m, D_MODEL), lambda l, j: (0, 0)),
            pl.BlockSpec((1, D_MODEL, ADA_TILE), lambda l, j: (l, 0, j)),
            pl.BlockSpec((1, 1, ADA_TILE), lambda l, j: (l, 0, j)),
        ],
        out_specs=pl.BlockSpec((1, m, ADA_TILE), lambda l, j: (l, 0, j)),
        out_shape=jax.ShapeDtypeStruct((DEPTH, m, n), F32),
        compiler_params=pltpu.CompilerParams(vmem_limit_bytes=VMEM_LIMIT),
        name="ada",
    )(c_all, w_ada, b_ada.reshape(DEPTH, 1, n))


def _route(lt, br):
    sc = jax.nn.sigmoid(lt)
    bi = sc + br
    s = [sc[e:e + 1] for e in range(N_EXPERTS)]
    v = [bi[e:e + 1] for e in range(N_EXPERTS)]
    gs = []
    for g in range(N_GROUPS):
        a = v[EXP_PER_GROUP * g:EXP_PER_GROUP * (g + 1)]
        m = a[0] + a[1]
        for i in range(EXP_PER_GROUP):
            for j in range(i + 1, EXP_PER_GROUP):
                if (i, j) != (0, 1):
                    m = jnp.maximum(m, a[i] + a[j])
        gs.append(m)
    best = gs[0]
    gsel = jnp.zeros_like(best)
    for g in range(1, N_GROUPS):
        upd = gs[g] > best
        best = jnp.where(upd, gs[g], best)
        gsel = jnp.where(upd, float(g), gsel)

    def pick(vals, key, n):
        out = vals[0]
        for g in range(1, n):
            out = jnp.where(key == float(g), vals[g], out)
        return out

    vv = [pick([v[EXP_PER_GROUP * g + j] for g in range(N_GROUPS)], gsel, N_GROUPS) for j in range(EXP_PER_GROUP)]
    ss = [pick([s[EXP_PER_GROUP * g + j] for g in range(N_GROUPS)], gsel, N_GROUPS) for j in range(EXP_PER_GROUP)]
    m1 = vv[0]
    i1 = jnp.zeros_like(m1)
    for j in range(1, EXP_PER_GROUP):
        upd = vv[j] > m1
        m1 = jnp.where(upd, vv[j], m1)
        i1 = jnp.where(upd, float(j), i1)
    m2 = jnp.full_like(m1, -jnp.inf)
    i2 = jnp.zeros_like(m1)
    for j in range(EXP_PER_GROUP):
        upd = (i1 != float(j)) & (vv[j] > m2)
        m2 = jnp.where(upd, vv[j], m2)
        i2 = jnp.where(upd, float(j), i2)
    s1 = pick(ss, i1, EXP_PER_GROUP)
    s2 = pick(ss, i2, EXP_PER_GROUP)
    den = s1 + s2
    w1 = s1 / den
    w2 = s2 / den
    first_lo = i1 < i2
    lo = jnp.where(first_lo, i1, i2)
    hi = jnp.where(first_lo, i2, i1)
    w_lo = jnp.where(first_lo, w1, w2)
    w_hi = jnp.where(first_lo, w2, w1)
    pair = jnp.where(lo == 0.0, hi - 1.0, jnp.where(lo == 1.0, hi + 1.0, 5.0))
    combo = gsel * float(PAIRS_PER_GROUP) + pair
    zero = jnp.zeros_like(combo)
    return jnp.concatenate([combo, w_lo, w_hi] + [zero] * (ROUTE_ROWS - 3), axis=0)


def _mixer_kernel(x_ref, sh1_ref, sc1_ref, g1_ref, sh2_ref, sc2_ref, na_ref, nm_ref, win_ref, wout_ref,
                  lb_ref, ga_ref, gr_ref, wrt_ref, br_ref, cos_ref, sin_ref, intra_ref, inter_ref, kdec_ref,
                  sdec_ref, tri_ref, sa0_ref, sr0_ref,
                  x1_ref, h2e_ref, route_ref, sa_ref, sr_ref,
                  proj_ref, ob_ref, sat_ref, *, bb_n, tq):
    tt = pl.program_id(1)
    rows = bb_n * tq
    n_chunks = tq // CHUNK

    x = x_ref[...]
    ms = jnp.mean(x * x, axis=-1, keepdims=True)
    h = x * lax.rsqrt(ms + EPS) * na_ref[...] * (1.0 + sc1_ref[...]) + sh1_ref[...]
    proj_ref[...] = _dot(h.reshape(rows, D_MODEL).astype(BF16), win_ref[...])

    @pl.when(tt == 0)
    def _():
        for b in range(bb_n):
            for hh in range(N_HEADS):
                sat_ref[b, hh] = sa0_ref[b, hh].T
        sr_ref[...] = sr0_ref[...]

    lb = lb_ref[...]
    ln_1mlb = jnp.log1p(-lb)
    ga = ga_ref[...]
    gr = gr_ref[...]
    tri3 = tri_ref[...]
    t_i = lax.broadcasted_iota(jnp.int32, (CHUNK, CHUNK), 0)
    s_i = lax.broadcasted_iota(jnp.int32, (CHUNK, CHUNK), 1)
    level_masks = []
    for m in LEVELS:
        sh = m.bit_length() - 1
        level_masks.append(((t_i >> (sh + 1)) == (s_i >> (sh + 1)))
                           & (((t_i >> sh) & 1) == 1) & (((s_i >> sh) & 1) == 0))
    diag_mask = ((t_i >> 3) == (s_i >> 3)) & (s_i <= t_i)
    col_in_sub = s_i & (SUB - 1)

    def col(group, hh):
        return slice(group * G_WIDTH + hh * D_HEAD, group * G_WIDTH + (hh + 1) * D_HEAD)

    def hgrn_head(b, rs, hh):
        sl = slice(hh * D_HEAD, (hh + 1) * D_HEAD)
        z = proj_ref[rs, col(1, hh)]
        lbh = lb[:, sl]
        u = jnp.exp(-jnp.abs(z))
        ln_1pu = jnp.log(1.0 + u)
        num = jnp.where(z >= 0.0, 1.0 + lbh * u, lbh + u)
        l2f = (jnp.where(num > 0.0, jnp.log(num), z) - ln_1pu) * LOG2E
        l2k = (ln_1mlb[:, sl] - jnp.maximum(z, 0.0) - ln_1pu) * LOG2E
        hi = l2f.astype(BF16)
        rem = l2f - hi.astype(F32)
        mid = rem.astype(BF16)
        lo = (rem - mid.astype(F32)).astype(BF16)
        bcum = _dot(tri3, jnp.concatenate([hi, mid, lo], axis=0))
        yield from STAGE_GAP
        q = proj_ref[rs, col(0, hh)]
        vb = proj_ref[rs, col(2, hh)].astype(BF16)
        b_last = bcum[CHUNK - 1:CHUNK, :]
        q_in = (q * jnp.exp2(bcum)).astype(BF16)
        k_out = jnp.exp2(l2k + (b_last - bcum)).astype(BF16)
        upd = _dot_tn(vb, k_out)
        lvl = []
        for m in LEVELS:
            ref_row = bcum.reshape(CHUNK // (2 * m), 2 * m, D_HEAD)[:, m - 1:m, :]
            ref_row = jnp.broadcast_to(ref_row, (CHUNK // (2 * m), 2 * m, D_HEAD)).reshape(CHUNK, D_HEAD)
            dist = jnp.abs(bcum - ref_row)
            lvl.append(_dot_nt((q * jnp.exp2(-dist)).astype(BF16), jnp.exp2(l2k - dist).astype(BF16)))
        src3 = (l2k - bcum).reshape(CHUNK // SUB, SUB, D_HEAD)
        diag = jnp.zeros((CHUNK, CHUNK), F32)
        for sg in range(SUB):
            src = jnp.broadcast_to(src3[:, sg:sg + 1, :], src3.shape).reshape(CHUNK, D_HEAD)
            row_sum = jnp.sum(q * jnp.exp2(jnp.minimum(bcum + src, 0.0)), axis=-1, keepdims=True)
            diag = jnp.where(col_in_sub == sg, row_sum, diag)
        yield from STAGE_GAP
        a = jnp.where(diag_mask, diag, 0.0)
        for li in range(len(LEVELS)):
            a = jnp.where(level_masks[li], lvl[li], a)
        st = sat_ref[b, hh]
        o = _dot(a.astype(BF16), vb) + _dot_nt(q_in, st.astype(BF16))
        sat_ref[b, hh] = st * jnp.exp2(b_last) + upd
        yield from STAGE_GAP
        o = o * lax.rsqrt(jnp.mean(o * o, axis=-1, keepdims=True) + EPS)
        o = o * ga[:, sl] * jax.nn.sigmoid(proj_ref[rs, col(3, hh)])
        ob_ref[rs, sl] = o.astype(BF16)

    def ret_head(b, rs, t0, hh):
        sl = slice(hh * D_HEAD, (hh + 1) * D_HEAD)
        cs = cos_ref[pl.ds(t0, CHUNK), :]
        sn = sin_ref[pl.ds(t0, CHUNK), :]
        qh = proj_ref[rs, col(4, hh)]
        kh = proj_ref[rs, col(5, hh)]
        qr = qh * cs + pltpu.roll(qh, D_HEAD // 2, 1) * sn
        kr = (kh * cs + pltpu.roll(kh, D_HEAD // 2, 1) * sn) * (D_HEAD ** -0.5)
        qb = qr.astype(BF16)
        vh = proj_ref[rs, col(6, hh)].astype(BF16)
        sc = _dot_nt(qb, kr.astype(BF16))
        upd = _dot_tn((kr * kdec_ref[hh]).astype(BF16), vh)
        yield from STAGE_GAP
        s_r = sr_ref[b, hh]
        o = _dot((sc * intra_ref[hh]).astype(BF16), vh) + _dot(qb, s_r.astype(BF16)) * inter_ref[hh]
        sr_ref[b, hh] = s_r * sdec_ref[hh] + upd
        yield from STAGE_GAP
        mu = jnp.mean(o, axis=-1, keepdims=True)
        oc = o - mu
        var = jnp.mean(oc * oc, axis=-1, keepdims=True)
        o = oc * lax.rsqrt(var + EPS) * gr[:, sl]
        g = proj_ref[rs, col(7, hh)]
        o = o * (g * jax.nn.sigmoid(g))
        ob_ref[rs, col(1, hh)] = o.astype(BF16)

    def chunk_body(i, carry):
        heads = []
        for j in range(CHUNK_UNROLL):
            idx = i * CHUNK_UNROLL + j
            b = idx // n_chunks
            rs = pl.ds(pl.multiple_of(idx * CHUNK, CHUNK), CHUNK)
            t0 = pl.multiple_of((idx - b * n_chunks) * CHUNK, CHUNK)
            for hh in range(N_HEADS):
                heads.append(hgrn_head(b, rs, hh))
                heads.append(ret_head(b, rs, t0, hh))
        live = []
        while heads or live:
            if heads:
                live.append(heads.pop(0))
            nxt = []
            for h in live:
                try:
                    next(h)
                    nxt.append(h)
                except StopIteration:
                    pass
            live = nxt
        return carry

    lax.fori_loop(0, bb_n * n_chunks // CHUNK_UNROLL, chunk_body, 0)

    @pl.when(tt == pl.num_programs(1) - 1)
    def _():
        for b in range(bb_n):
            for hh in range(N_HEADS):
                sa_ref[b, hh] = sat_ref[b, hh].T

    out = _dot(ob_ref[...], wout_ref[...]).reshape(bb_n, tq, D_MODEL)
    x1 = x + g1_ref[...] * out
    x1_ref[...] = x1
    ms2 = jnp.mean(x1 * x1, axis=-1, keepdims=True)
    h2 = x1 * lax.rsqrt(ms2 + EPS) * nm_ref[...] * (1.0 + sc2_ref[...]) + sh2_ref[...]
    h2e_ref[:, :, 0:D_MODEL] = h2
    lt = lax.dot_general(wrt_ref[...], h2.reshape(rows, D_MODEL), _NT, precision=HIGHEST,
                         preferred_element_type=F32)
    route = _route(lt, br_ref[...])
    route_ref[0] = route
    rec = jnp.concatenate([route, jnp.zeros((EXT - ROUTE_ROWS, rows), F32)], axis=0).T
    h2e_ref[:, :, D_MODEL:D_MODEL + EXT] = rec.reshape(bb_n, tq, EXT)


def _mixer(x, mods, consts, lw, s_a0, s_r0, offset):
    batch, seq, _ = x.shape
    bb_n, tq = _tiling(batch, seq)
    nb, nt = batch // bb_n, seq // tq
    cos, sin = _rope_tables(seq, offset)
    sh1, sc1, g1, sh2, sc2 = mods

    def per_b(shape):
        return pl.BlockSpec(shape, lambda b, t: (b,) + (0,) * (len(shape) - 1))

    def const(shape):
        return pl.BlockSpec(shape, lambda b, t: (0,) * len(shape))

    mod_spec = per_b((bb_n, 1, D_MODEL))
    state_spec = per_b((bb_n, N_HEADS, D_HEAD, D_HEAD))
    in_specs = [
        pl.BlockSpec((bb_n, tq, D_MODEL), lambda b, t: (b, t, 0)),
        mod_spec, mod_spec, mod_spec, mod_spec, mod_spec,
        const((1, D_MODEL)), const((1, D_MODEL)),
        const((D_MODEL, IN_COLS)), const((2 * G_WIDTH, D_MODEL)),
        const((1, G_WIDTH)), const((1, G_WIDTH)), const((1, G_WIDTH)),
        const((N_EXPERTS, D_MODEL)), const((N_EXPERTS, 1)),
        pl.BlockSpec((tq, D_HEAD), lambda b, t: (t, 0)), pl.BlockSpec((tq, D_HEAD), lambda b, t: (t, 0)),
        const((N_HEADS, CHUNK, CHUNK)), const((N_HEADS, CHUNK, D_HEAD)), const((N_HEADS, CHUNK, D_HEAD)),
        const((N_HEADS, 1, D_HEAD)), const((CHUNK, 3 * CHUNK)),
        state_spec, state_spec,
    ]
    out_specs = [
        pl.BlockSpec((bb_n, tq, D_MODEL), lambda b, t: (b, t, 0)),
        pl.BlockSpec((bb_n, tq, D_MODEL + EXT), lambda b, t: (b, t, 0)),
        pl.BlockSpec((1, ROUTE_ROWS, ROW_TILE), lambda b, t: (b * nt + t, 0, 0)),
        state_spec, state_spec,
    ]
    out_shape = [
        jax.ShapeDtypeStruct((batch, seq, D_MODEL), F32),
        jax.ShapeDtypeStruct((batch, seq, D_MODEL + EXT), F32),
        jax.ShapeDtypeStruct((nb * nt, ROUTE_ROWS, ROW_TILE), F32),
        jax.ShapeDtypeStruct((batch, N_HEADS, D_HEAD, D_HEAD), F32),
        jax.ShapeDtypeStruct((batch, N_HEADS, D_HEAD, D_HEAD), F32),
    ]
    return pl.pallas_call(
        functools.partial(_mixer_kernel, bb_n=bb_n, tq=tq),
        grid=(nb, nt),
        in_specs=in_specs,
        out_specs=out_specs,
        out_shape=out_shape,
        scratch_shapes=[
            pltpu.VMEM((ROW_TILE, IN_COLS), F32),
            pltpu.VMEM((ROW_TILE, 2 * G_WIDTH), BF16),
            pltpu.VMEM((bb_n, N_HEADS, D_HEAD, D_HEAD), F32),
        ],
        compiler_params=pltpu.CompilerParams(
            dimension_semantics=("arbitrary", "arbitrary"), vmem_limit_bytes=VMEM_LIMIT),
        name="mixer",
    )(x, sh1, sc1, g1, sh2, sc2, lw["na"], lw["nm"], lw["w_in"], lw["w_out"], lw["lb"], lw["ga"], lw["gr"],
      consts["wrt"], consts["br"], cos, sin, consts["intra"], consts["inter"], consts["kdec"], consts["sdec"],
      consts["tri"], s_a0, s_r0)


def _rope_tables(seq, offset):
    half = D_HEAD // 2
    freq = ROPE_BASE ** (-jnp.arange(half, dtype=F32) / half)
    ang = (jnp.arange(seq, dtype=F32) + offset)[:, None] * freq[None, :]
    cos, sin = jnp.cos(ang), jnp.sin(ang)
    return jnp.concatenate([cos, cos], axis=-1), jnp.concatenate([-sin, sin], axis=-1)


def _retention_tables():
    log_g = jnp.log(1.0 - 2.0 ** (-5.0 - jnp.arange(N_HEADS, dtype=F32)))
    idx = jnp.arange(CHUNK, dtype=F32)
    rel = idx[:, None] - idx[None, :]
    intra = jnp.where(rel >= 0, jnp.exp(jnp.maximum(rel, 0.0)[None] * log_g[:, None, None]), 0.0)
    inter = jnp.exp((idx[None, :] + 1.0) * log_g[:, None])[..., None]
    kdec = jnp.exp((CHUNK - 1.0 - idx[None, :]) * log_g[:, None])[..., None]
    sdec = jnp.exp(CHUNK * log_g)[:, None, None]
    wide = (N_HEADS, CHUNK, D_HEAD)
    return (intra, jnp.broadcast_to(inter, wide), jnp.broadcast_to(kdec, wide),
            jnp.broadcast_to(sdec, (N_HEADS, 1, D_HEAD)))


def _plan_kernel(route_ref, triu_ref, lstrict_ref, pos_ref, cnt_ref, carry_ref, base_ref):
    phase = pl.program_id(0)
    i = pl.program_id(1)
    combo = route_ref[0, 0:1, :]
    ids = lax.broadcasted_iota(jnp.int32, (COMBO_ROWS, ROW_TILE), 0).astype(F32)
    onehot = (ids == combo).astype(F32)
    tile_cnt = jnp.broadcast_to(jnp.sum(onehot, axis=1, keepdims=True), (COMBO_ROWS, 128))

    @pl.when((phase == 0) & (i == 0))
    def _():
        carry_ref[...] = jnp.zeros_like(carry_ref)

    @pl.when(phase == 0)
    def _():
        carry_ref[...] += tile_cnt

    @pl.when((phase == 1) & (i == 0))
    def _():
        cnt = carry_ref[...]
        cnt_ref[...] = cnt
        padded = jnp.floor((cnt + float(MOE_BLOCK - 1)) * (1.0 / MOE_BLOCK)) * float(MOE_BLOCK)
        base_ref[...] = jnp.dot(lstrict_ref[...], padded, precision=HIGHEST, preferred_element_type=F32)
        carry_ref[...] = jnp.zeros_like(carry_ref)

    @pl.when(phase == 1)
    def _():
        cum = _dot(onehot.astype(BF16), triu_ref[...])
        start = base_ref[:, 0:1] + carry_ref[:, 0:1]
        pos = jnp.sum(onehot * (cum - 1.0 + start), axis=0, keepdims=True)
        pos_ref[0] = jnp.broadcast_to(pos, (ROUTE_ROWS, ROW_TILE)).astype(jnp.int32)
        carry_ref[...] += tile_cnt


def _plan(route, consts):
    steps = route.shape[0]
    return pl.pallas_call(
        _plan_kernel,
        grid=(2, steps),
        in_specs=[
            pl.BlockSpec((1, ROUTE_ROWS, ROW_TILE), lambda p, i: (i, 0, 0)),
            pl.BlockSpec((ROW_TILE, ROW_TILE), lambda p, i: (0, 0)),
            pl.BlockSpec((COMBO_ROWS, COMBO_ROWS), lambda p, i: (0, 0)),
        ],
        out_specs=[
            pl.BlockSpec((1, ROUTE_ROWS, ROW_TILE), lambda p, i: (i * p, 0, 0)),
            pl.BlockSpec((COMBO_ROWS, 128), lambda p, i: (0, 0)),
        ],
        out_shape=[
            jax.ShapeDtypeStruct((steps, ROUTE_ROWS, ROW_TILE), jnp.int32),
            jax.ShapeDtypeStruct((COMBO_ROWS, 128), F32),
        ],
        scratch_shapes=[pltpu.VMEM((COMBO_ROWS, 128), F32), pltpu.VMEM((COMBO_ROWS, 128), F32)],
        compiler_params=pltpu.CompilerParams(dimension_semantics=("arbitrary", "arbitrary")),
        name="plan",
    )(route, consts["triu"], consts["lstrict"])


def _row_copy(src_ref, src_row, dst_ref, dst_row, sem):
    return pltpu.make_async_copy(src_ref.at[pl.ds(src_row, 1), :], dst_ref.at[pl.ds(dst_row, 1), :], sem)


def _scatter_kernel(pos_ref, h2e_ref, xs_in_ref, xs_ref, sem):
    del xs_in_ref
    base = pl.program_id(0) * ROW_TILE

    def issue(r, carry):
        _row_copy(h2e_ref, r, xs_ref, pos_ref[base + r], sem).start()
        return carry

    lax.fori_loop(0, ROW_TILE, issue, 0)
    pltpu.make_async_copy(h2e_ref, xs_ref.at[pl.ds(0, ROW_TILE), :], sem).wait()


def _scatter(pos, h2e, xs):
    n_tok = h2e.shape[0]
    width = h2e.shape[1]
    return pl.pallas_call(
        _scatter_kernel,
        grid_spec=pltpu.PrefetchScalarGridSpec(
            num_scalar_prefetch=1,
            grid=(n_tok // ROW_TILE,),
            in_specs=[
                pl.BlockSpec((ROW_TILE, width), lambda i, pos: (i, 0)),
                pl.BlockSpec(memory_space=pl.ANY),
            ],
            out_specs=pl.BlockSpec(memory_space=pl.ANY),
            scratch_shapes=[pltpu.SemaphoreType.DMA],
        ),
        out_shape=jax.ShapeDtypeStruct(xs.shape, xs.dtype),
        input_output_aliases={2: 0},
        compiler_params=pltpu.CompilerParams(dimension_semantics=("arbitrary",)),
        name="scatter",
    )(pos, h2e, xs)


def _moe_kernel(ea_ref, eb_ref, nu_ref, xs_ref, w1a, w3a, w2a, w1b, w3b, w2b, ys_ref):
    del ea_ref, eb_ref
    j = pl.program_id(0)

    @pl.when(j < nu_ref[0])
    def _():
        x = xs_ref[:, 0:D_MODEL].astype(BF16)
        w_lo = xs_ref[:, D_MODEL + 1:D_MODEL + 2]
        w_hi = xs_ref[:, D_MODEL + 2:D_MODEL + 3]

        def ffn(w1, w3, w2):
            h1 = _dot(x, w1[0])
            hid = (h1 * jax.nn.sigmoid(h1)) * _dot(x, w3[0])
            return _dot(hid.astype(BF16), w2[0])

        ys_ref[...] = ffn(w1a, w3a, w2a) * w_lo + ffn(w1b, w3b, w2b) * w_hi

    @pl.when(j >= nu_ref[0])
    def _():
        ys_ref[...] = jnp.zeros_like(ys_ref)


def _moe(ea, eb, n_used, xs, w1, w3, w2):
    n_rows, width = xs.shape
    n_blocks = n_rows // MOE_BLOCK

    def w_spec(which):
        if which == 0:
            return pl.BlockSpec((1, D_MODEL, D_MODEL), lambda j, ea, eb, nu: (ea[j], 0, 0))
        return pl.BlockSpec((1, D_MODEL, D_MODEL), lambda j, ea, eb, nu: (eb[j], 0, 0))

    return pl.pallas_call(
        _moe_kernel,
        grid_spec=pltpu.PrefetchScalarGridSpec(
            num_scalar_prefetch=3,
            grid=(n_blocks,),
            in_specs=[
                pl.BlockSpec((MOE_BLOCK, width), lambda j, ea, eb, nu: (jnp.minimum(j, nu[0] - 1), 0)),
                w_spec(0), w_spec(0), w_spec(0), w_spec(1), w_spec(1), w_spec(1),
            ],
            out_specs=pl.BlockSpec((MOE_BLOCK, D_MODEL), lambda j, ea, eb, nu: (j, 0)),
        ),
        out_shape=jax.ShapeDtypeStruct((n_rows, D_MODEL), F32),
        compiler_params=pltpu.CompilerParams(dimension_semantics=("arbitrary",), vmem_limit_bytes=VMEM_LIMIT),
        name="moe",
    )(ea, eb, n_used, xs, w1, w3, w2, w1, w3, w2)


def _combine_kernel(pos_ref, x1_ref, g2_ref, nf_ref, ys_ref, out_ref, buf_ref, sem, *, final):
    step = pl.program_id(0) * pl.num_programs(1) + pl.program_id(1)
    base = step * ROW_TILE

    def issue(r, carry):
        _row_copy(ys_ref, pos_ref[base + r], buf_ref, r, sem).start()
        return carry

    lax.fori_loop(0, ROW_TILE, issue, 0)
    pltpu.make_async_copy(ys_ref.at[pl.ds(0, ROW_TILE), :], buf_ref, sem).wait()
    x2 = x1_ref[...] + g2_ref[...] * buf_ref[...].reshape(x1_ref.shape)
    if final:
        x2 = x2 * lax.rsqrt(jnp.mean(x2 * x2, axis=-1, keepdims=True) + EPS) * nf_ref[...]
    out_ref[...] = x2


def _combine(pos, x1, g2, nf, ys, final):
    batch, seq, _ = x1.shape
    bb_n, tq = _tiling(batch, seq)
    tok_spec = pl.BlockSpec((bb_n, tq, D_MODEL), lambda b, t, pos: (b, t, 0))
    return pl.pallas_call(
        functools.partial(_combine_kernel, final=final),
        grid_spec=pltpu.PrefetchScalarGridSpec(
            num_scalar_prefetch=1,
            grid=(batch // bb_n, seq // tq),
            in_specs=[
                tok_spec,
                pl.BlockSpec((bb_n, 1, D_MODEL), lambda b, t, pos: (b, 0, 0)),
                pl.BlockSpec((1, D_MODEL), lambda b, t, pos: (0, 0)),
                pl.BlockSpec(memory_space=pl.ANY),
            ],
            out_specs=tok_spec,
            scratch_shapes=[pltpu.VMEM((ROW_TILE, D_MODEL), F32), pltpu.SemaphoreType.DMA],
        ),
        out_shape=jax.ShapeDtypeStruct(x1.shape, F32),
        compiler_params=pltpu.CompilerParams(dimension_semantics=("arbitrary", "arbitrary")),
        name="combine",
    )(pos, x1, g2, nf, ys)


def _block_table(cnt, n_blocks):
    counts = cnt[:N_COMBO, 0].astype(jnp.int32)
    padded = (counts + MOE_BLOCK - 1) // MOE_BLOCK * MOE_BLOCK
    pad_end = jnp.cumsum(padded)
    n_used = (pad_end[-1] // MOE_BLOCK).astype(jnp.int32)
    blk = jnp.minimum(jnp.arange(n_blocks, dtype=jnp.int32), n_used - 1)
    combo = jnp.sum((pad_end[None, :] <= (blk * MOE_BLOCK)[:, None]).astype(jnp.int32), axis=1)
    combo = jnp.minimum(combo, N_COMBO - 1)
    lo_tab = jnp.array([0, 0, 0, 1, 1, 2], jnp.int32)
    hi_tab = jnp.array([1, 2, 3, 2, 3, 3], jnp.int32)
    grp = combo // PAIRS_PER_GROUP
    pair = combo % PAIRS_PER_GROUP
    return grp * EXP_PER_GROUP + lo_tab[pair], grp * EXP_PER_GROUP + hi_tab[pair], n_used.reshape(1)


def kernel(x_prompt, x_sample, state_hgrn, state_ret, c_prompt, c_sample, w_in, w_out, lb_logits, hgrn_norm,
           ret_norm, norm_attn, norm_moe, w_ada, b_ada, w_router, b_router, w1, w3, w2, norm_f):
    bp, tp, _ = x_prompt.shape
    bs, ts, _ = x_sample.shape
    n_p, n_s = bp * tp, bs * ts
    n_tok = n_p + n_s
    n_blocks = n_tok // MOE_BLOCK + N_COMBO

    lb_cum = jnp.cumsum(jax.nn.softmax(lb_logits.astype(F32), axis=0), axis=0)
    lb_all = lb_cum - lb_cum[0:1]

    mod = _ada(jnp.concatenate([c_prompt, c_sample], axis=0), w_ada, b_ada)

    intra, inter, kdec, sdec = _retention_tables()
    consts = {
        "wrt": w_router.T.astype(F32), "br": b_router.astype(F32).reshape(N_EXPERTS, 1),
        "intra": intra, "inter": inter, "kdec": kdec, "sdec": sdec,
        "tri": jnp.tile(jnp.tril(jnp.ones((CHUNK, CHUNK), BF16)), (1, 3)),
        "triu": jnp.triu(jnp.ones((ROW_TILE, ROW_TILE), BF16)),
        "lstrict": jnp.tril(jnp.ones((COMBO_ROWS, COMBO_ROWS), F32), k=-1),
    }

    xp, xs_ = x_prompt, x_sample
    sa_p = jnp.zeros((bp, N_HEADS, D_HEAD, D_HEAD), F32)
    sr_p = jnp.zeros((bp, N_HEADS, D_HEAD, D_HEAD), F32)
    new_states = []
    for l in range(DEPTH):
        lw = {
            "na": norm_attn[l].reshape(1, D_MODEL), "nm": norm_moe[l].reshape(1, D_MODEL),
            "w_in": w_in[l].astype(BF16), "w_out": w_out[l].astype(BF16),
            "lb": lb_all[l].reshape(1, G_WIDTH), "ga": hgrn_norm[l].reshape(1, G_WIDTH),
            "gr": ret_norm[l].reshape(1, G_WIDTH),
        }
        parts = jnp.split(mod[l], 6, axis=-1)
        mods_p = [p[:bp, None, :] for p in parts]
        mods_s = [p[bp:, None, :] for p in parts]
        x1p, h2p, route_p, sa_np, sr_np = _mixer(xp, mods_p[:5], consts, lw, sa_p, sr_p, 0)
        x1s, h2s, route_s, sa_ns, sr_ns = _mixer(xs_, mods_s[:5], consts, lw, state_hgrn[l].astype(F32),
                                                 state_ret[l].astype(F32), PAST_LEN)
        new_states.append((sa_np, sr_np, sa_ns, sr_ns))

        pos3, cnt = _plan(jnp.concatenate([route_p, route_s], axis=0), consts)
        pos = pos3[:, 0, :].reshape(n_tok)
        ea, eb, n_used = _block_table(cnt, n_blocks)

        xs = jnp.zeros((n_blocks * MOE_BLOCK, D_MODEL + EXT), F32)
        xs = _scatter(pos[:n_p], h2p.reshape(n_p, D_MODEL + EXT), xs)
        xs = _scatter(pos[n_p:], h2s.reshape(n_s, D_MODEL + EXT), xs)
        ys = _moe(ea, eb, n_used, xs, w1[l].astype(BF16), w3[l].astype(BF16), w2[l].astype(BF16))

        final = l == DEPTH - 1
        nf = norm_f.reshape(1, D_MODEL)
        xp = _combine(pos[:n_p], x1p, mods_p[5], nf, ys, final)
        xs_ = _combine(pos[n_p:], x1s, mods_s[5], nf, ys, final)

    sa_prompt = jnp.stack([s[0] for s in new_states])
    sr_prompt = jnp.stack([s[1] for s in new_states])
    sa_sample = jnp.stack([s[2] for s in new_states])
    sr_sample = jnp.stack([s[3] for s in new_states])
    return (xp, xs_, sa_prompt, sr_prompt, sa_sample, sr_sample)
```

```python
import functools

import jax
import jax.numpy as jnp
from jax import lax
from jax.experimental import pallas as pl
from jax.experimental.pallas import tpu as pltpu

F32 = jnp.float32
BF16 = jnp.bfloat16
HIGHEST = lax.Precision.HIGHEST

D_MODEL = 1024
DEPTH = 2
N_HEADS = 4
D_HEAD = 128
G_WIDTH = N_HEADS * D_HEAD
IN_COLS = 8 * G_WIDTH
CHUNK = 64
CHUNK_UNROLL = 2
STAGE_GAP = (None,) * 3
SUB = 8
LEVELS = (32, 16, 8)
N_EXPERTS = 16
EXP_PER_GROUP = 4
N_GROUPS = 4
PAIRS_PER_GROUP = 6
N_COMBO = N_GROUPS * PAIRS_PER_GROUP
COMBO_ROWS = 32
ROUTE_ROWS = 8
EXT = 128
ROW_TILE = 512
ISSUE_UNROLL = 8
MOE_BLOCK = 256
ADA_TILE = 1536
ROPE_BASE = 10000.0
PAST_LEN = 1024
EPS = 1e-6
LOG2E = 1.4426950408889634
VMEM_LIMIT = 56 * 1024 * 1024

_NT = (((1,), (1,)), ((), ()))
_TN = (((0,), (0,)), ((), ()))


def _dot(a, b):
    return jnp.dot(a, b, preferred_element_type=F32)


def _dot_nt(a, b):
    return lax.dot_general(a, b, _NT, preferred_element_type=F32)


def _dot_tn(a, b):
    return lax.dot_general(a, b, _TN, preferred_element_type=F32)


def _tiling(batch, seq):
    tq = min(seq, ROW_TILE)
    bb = ROW_TILE // tq
    assert tq % CHUNK == 0 and bb * tq == ROW_TILE and batch % bb == 0 and seq % tq == 0
    return bb, tq


def _ada_kernel(c_ref, w_ref, b_ref, o_ref):
    c = c_ref[...]
    cs = c * jax.nn.sigmoid(c)
    o_ref[0] = _dot(cs.astype(BF16), w_ref[0].astype(BF16)) + b_ref[0]


def _ada(c_all, w_ada, b_ada):
    m = c_all.shape[0]
    n = w_ada.shape[-1]
    return pl.pallas_call(
        _ada_kernel,
        grid=(DEPTH, n // ADA_TILE),
        in_specs=[
            pl.BlockSpec((m, D_MODEL), lambda l, j: (0, 0)),
            pl.BlockSpec((1, D_MODEL, ADA_TILE), lambda l, j: (l, 0, j)),
            pl.BlockSpec((1, 1, ADA_TILE), lambda l, j: (l, 0, j)),
        ],
        out_specs=pl.BlockSpec((1, m, ADA_TILE), lambda l, j: (l, 0, j)),
        out_shape=jax.ShapeDtypeStruct((DEPTH, m, n), F32),
        compiler_params=pltpu.CompilerParams(vmem_limit_bytes=VMEM_LIMIT),
        name="ada",
    )(c_all, w_ada, b_ada.reshape(DEPTH, 1, n))


def _route(lt, br):
    sc = jax.nn.sigmoid(lt)
    bi = sc + br
    s = [sc[e:e + 1] for e in range(N_EXPERTS)]
    v = [bi[e:e + 1] for e in range(N_EXPERTS)]
    gs = []
    for g in range(N_GROUPS):
        a = v[EXP_PER_GROUP * g:EXP_PER_GROUP * (g + 1)]
        m = a[0] + a[1]
        for i in range(EXP_PER_GROUP):
            for j in range(i + 1, EXP_PER_GROUP):
                if (i, j) != (0, 1):
                    m = jnp.maximum(m, a[i] + a[j])
        gs.append(m)
    best = gs[0]
    gsel = jnp.zeros_like(best)
    for g in range(1, N_GROUPS):
        upd = gs[g] > best
        best = jnp.where(upd, gs[g], best)
        gsel = jnp.where(upd, float(g), gsel)

    def pick(vals, key, n):
        out = vals[0]
        for g in range(1, n):
            out = jnp.where(key == float(g), vals[g], out)
        return out

    vv = [pick([v[EXP_PER_GROUP * g + j] for g in range(N_GROUPS)], gsel, N_GROUPS) for j in range(EXP_PER_GROUP)]
    ss = [pick([s[EXP_PER_GROUP * g + j] for g in range(N_GROUPS)], gsel, N_GROUPS) for j in range(EXP_PER_GROUP)]
    m1 = vv[0]
    i1 = jnp.zeros_like(m1)
    for j in range(1, EXP_PER_GROUP):
        upd = vv[j] > m1
        m1 = jnp.where(upd, vv[j], m1)
        i1 = jnp.where(upd, float(j), i1)
    m2 = jnp.full_like(m1, -jnp.inf)
    i2 = jnp.zeros_like(m1)
    for j in range(EXP_PER_GROUP):
        upd = (i1 != float(j)) & (vv[j] > m2)
        m2 = jnp.where(upd, vv[j], m2)
        i2 = jnp.where(upd, float(j), i2)
    s1 = pick(ss, i1, EXP_PER_GROUP)
    s2 = pick(ss, i2, EXP_PER_GROUP)
    den = s1 + s2
    w1 = s1 / den
    w2 = s2 / den
    first_lo = i1 < i2
    lo = jnp.where(first_lo, i1, i2)
    hi = jnp.where(first_lo, i2, i1)
    w_lo = jnp.where(first_lo, w1, w2)
    w_hi = jnp.where(first_lo, w2, w1)
    pair = jnp.where(lo == 0.0, hi - 1.0, jnp.where(lo == 1.0, hi + 1.0, 5.0))
    combo = gsel * float(PAIRS_PER_GROUP) + pair
    zero = jnp.zeros_like(combo)
    return jnp.concatenate([combo, w_lo, w_hi] + [zero] * (ROUTE_ROWS - 3), axis=0)


def _mixer_kernel(pos_ref, x_ref, g2p_ref, ys_ref, sh1_ref, sc1_ref, g1_ref, sh2_ref, sc2_ref, na_ref, nm_ref,
                  win_ref, wout_ref, lb_ref, ga_ref, gr_ref, wrt_ref, br_ref, cos_ref, sin_ref, intra_ref,
                  inter_ref, kdec_ref, sdec_ref, tri_ref, sa0_ref, sr0_ref,
                  x1_ref, h2e_ref, route_ref, sa_ref, sr_ref,
                  proj_ref, ob_ref, sat_ref, gbuf_ref, gsem, ssem, *, bb_n, tq, gather):
    tt = pl.program_id(1)
    rows = bb_n * tq
    n_chunks = tq // CHUNK
    n_iter = bb_n * n_chunks // CHUNK_UNROLL
    step = pl.program_id(0) * pl.num_programs(1) + tt
    last_step = pl.num_programs(0) * pl.num_programs(1) - 1

    x = x_ref[...]
    if gather:
        slot = step % 2

        def fetch(tile, dst_slot, r):
            _row_copy(ys_ref, pos_ref[tile * ROW_TILE + r], gbuf_ref.at[dst_slot], r, gsem.at[dst_slot]).start()

        @pl.when(step == 0)
        def _():
            def issue(g, carry):
                r0 = pl.multiple_of(g * ISSUE_UNROLL, ISSUE_UNROLL)
                for j in range(ISSUE_UNROLL):
                    fetch(0, 0, r0 + j)
                return carry
            lax.fori_loop(0, ROW_TILE // ISSUE_UNROLL, issue, 0)

        pltpu.make_async_copy(ys_ref.at[pl.ds(0, ROW_TILE), :], gbuf_ref.at[slot], gsem.at[slot]).wait()
        x = x + g2p_ref[...] * gbuf_ref[slot].reshape(bb_n, tq, D_MODEL)
    ms = jnp.mean(x * x, axis=-1, keepdims=True)
    h = x * lax.rsqrt(ms + EPS) * na_ref[...] * (1.0 + sc1_ref[...]) + sh1_ref[...]
    proj_ref[...] = _dot(h.reshape(rows, D_MODEL).astype(BF16), win_ref[...])

    @pl.when(tt == 0)
    def _():
        b0 = pl.program_id(0) * bb_n
        load_a = pltpu.make_async_copy(sa0_ref.at[pl.ds(b0, bb_n)], sa_ref, ssem.at[0])
        load_r = pltpu.make_async_copy(sr0_ref.at[pl.ds(b0, bb_n)], sr_ref, ssem.at[1])
        load_a.start()
        load_r.start()
        load_a.wait()
        load_r.wait()
        for b in range(bb_n):
            for hh in range(N_HEADS):
                sat_ref[b, hh] = sa_ref[b, hh].T

    lb = lb_ref[...]
    ln_1mlb = jnp.log1p(-lb)
    ga = ga_ref[...]
    gr = gr_ref[...]
    tri3 = tri_ref[...]
    t_i = lax.broadcasted_iota(jnp.int32, (CHUNK, CHUNK), 0)
    s_i = lax.broadcasted_iota(jnp.int32, (CHUNK, CHUNK), 1)
    level_masks = []
    for m in LEVELS:
        sh = m.bit_length() - 1
        level_masks.append(((t_i >> (sh + 1)) == (s_i >> (sh + 1)))
                           & (((t_i >> sh) & 1) == 1) & (((s_i >> sh) & 1) == 0))
    diag_mask = ((t_i >> 3) == (s_i >> 3)) & (s_i <= t_i)
    col_in_sub = s_i & (SUB - 1)

    def col(group, hh):
        return slice(group * G_WIDTH + hh * D_HEAD, group * G_WIDTH + (hh + 1) * D_HEAD)

    def hgrn_head(b, rs, hh):
        sl = slice(hh * D_HEAD, (hh + 1) * D_HEAD)
        z = proj_ref[rs, col(1, hh)]
        lbh = lb[:, sl]
        u = jnp.exp(-jnp.abs(z))
        ln_1pu = jnp.log(1.0 + u)
        num = jnp.where(z >= 0.0, 1.0 + lbh * u, lbh + u)
        l2f = (jnp.where(num > 0.0, jnp.log(num), z) - ln_1pu) * LOG2E
        l2k = (ln_1mlb[:, sl] - jnp.maximum(z, 0.0) - ln_1pu) * LOG2E
        hi = l2f.astype(BF16)
        rem = l2f - hi.astype(F32)
        mid = rem.astype(BF16)
        lo = (rem - mid.astype(F32)).astype(BF16)
        bcum = _dot(tri3, jnp.concatenate([hi, mid, lo], axis=0))
        yield from STAGE_GAP
        q = proj_ref[rs, col(0, hh)]
        vb = proj_ref[rs, col(2, hh)].astype(BF16)
        b_last = bcum[CHUNK - 1:CHUNK, :]
        q_in = (q * jnp.exp2(bcum)).astype(BF16)
        k_out = jnp.exp2(l2k + (b_last - bcum)).astype(BF16)
        upd = _dot_tn(vb, k_out)
        lvl = []
        for m in LEVELS:
            ref_row = bcum.reshape(CHUNK // (2 * m), 2 * m, D_HEAD)[:, m - 1:m, :]
            ref_row = jnp.broadcast_to(ref_row, (CHUNK // (2 * m), 2 * m, D_HEAD)).reshape(CHUNK, D_HEAD)
            dist = jnp.abs(bcum - ref_row)
            lvl.append(_dot_nt((q * jnp.exp2(-dist)).astype(BF16), jnp.exp2(l2k - dist).astype(BF16)))
        src3 = (l2k - bcum).reshape(CHUNK // SUB, SUB, D_HEAD)
        diag = jnp.zeros((CHUNK, CHUNK), F32)
        for sg in range(SUB):
            src = jnp.broadcast_to(src3[:, sg:sg + 1, :], src3.shape).reshape(CHUNK, D_HEAD)
            row_sum = jnp.sum(q * jnp.exp2(jnp.minimum(bcum + src, 0.0)), axis=-1, keepdims=True)
            diag = jnp.where(col_in_sub == sg, row_sum, diag)
        yield from STAGE_GAP
        a = jnp.where(diag_mask, diag, 0.0)
        for li in range(len(LEVELS)):
            a = jnp.where(level_masks[li], lvl[li], a)
        st = sat_ref[b, hh]
        o = _dot(a.astype(BF16), vb) + _dot_nt(q_in, st.astype(BF16))
        sat_ref[b, hh] = st * jnp.exp2(b_last) + upd
        yield from STAGE_GAP
        o = o * lax.rsqrt(jnp.mean(o * o, axis=-1, keepdims=True) + EPS)
        o = o * ga[:, sl] * jax.nn.sigmoid(proj_ref[rs, col(3, hh)])
        ob_ref[rs, sl] = o.astype(BF16)

    def ret_head(b, rs, t0, hh):
        sl = slice(hh * D_HEAD, (hh + 1) * D_HEAD)
        cs = cos_ref[pl.ds(t0, CHUNK), :]
        sn = sin_ref[pl.ds(t0, CHUNK), :]
        qh = proj_ref[rs, col(4, hh)]
        kh = proj_ref[rs, col(5, hh)]
        qr = qh * cs + pltpu.roll(qh, D_HEAD // 2, 1) * sn
        kr = (kh * cs + pltpu.roll(kh, D_HEAD // 2, 1) * sn) * (D_HEAD ** -0.5)
        qb = qr.astype(BF16)
        vh = proj_ref[rs, col(6, hh)].astype(BF16)
        sc = _dot_nt(qb, kr.astype(BF16))
        upd = _dot_tn((kr * kdec_ref[hh]).astype(BF16), vh)
        yield from STAGE_GAP
        s_r = sr_ref[b, hh]
        o = _dot((sc * intra_ref[hh]).astype(BF16), vh) + _dot(qb, s_r.astype(BF16)) * inter_ref[hh]
        sr_ref[b, hh] = s_r * sdec_ref[hh] + upd
        yield from STAGE_GAP
        mu = jnp.mean(o, axis=-1, keepdims=True)
        oc = o - mu
        var = jnp.mean(oc * oc, axis=-1, keepdims=True)
        o = oc * lax.rsqrt(var + EPS) * gr[:, sl]
        g = proj_ref[rs, col(7, hh)]
        o = o * (g * jax.nn.sigmoid(g))
        ob_ref[rs, col(1, hh)] = o.astype(BF16)

    def chunk_body(i, carry):
        if gather:
            next_tile = jnp.minimum(step + 1, last_step)
            per_iter = ROW_TILE // n_iter
            for j in range(per_iter):
                fetch(next_tile, 1 - slot, i * per_iter + j)
        heads = []
        for j in range(CHUNK_UNROLL):
            idx = i * CHUNK_UNROLL + j
            b = idx // n_chunks
            rs = pl.ds(pl.multiple_of(idx * CHUNK, CHUNK), CHUNK)
            t0 = pl.multiple_of((idx - b * n_chunks) * CHUNK, CHUNK)
            for hh in range(N_HEADS):
                heads.append(hgrn_head(b, rs, hh))
                heads.append(ret_head(b, rs, t0, hh))
        live = []
        while heads or live:
            if heads:
                live.append(heads.pop(0))
            nxt = []
            for h in live:
                try:
                    next(h)
                    nxt.append(h)
                except StopIteration:
                    pass
            live = nxt
        return carry

    lax.fori_loop(0, n_iter, chunk_body, 0)

    if gather:
        @pl.when(step == last_step)
        def _():
            pltpu.make_async_copy(ys_ref.at[pl.ds(0, ROW_TILE), :], gbuf_ref.at[1 - slot], gsem.at[1 - slot]).wait()

    @pl.when(tt == pl.num_programs(1) - 1)
    def _():
        for b in range(bb_n):
            for hh in range(N_HEADS):
                sa_ref[b, hh] = sat_ref[b, hh].T

    out = _dot(ob_ref[...], wout_ref[...]).reshape(bb_n, tq, D_MODEL)
    x1 = x + g1_ref[...] * out
    x1_ref[...] = x1
    ms2 = jnp.mean(x1 * x1, axis=-1, keepdims=True)
    h2 = x1 * lax.rsqrt(ms2 + EPS) * nm_ref[...] * (1.0 + sc2_ref[...]) + sh2_ref[...]
    h2e_ref[:, :, 0:D_MODEL] = h2
    lt = lax.dot_general(wrt_ref[...], h2.reshape(rows, D_MODEL), _NT, precision=HIGHEST,
                         preferred_element_type=F32)
    route = _route(lt, br_ref[...])
    route_ref[0] = route
    rec = jnp.concatenate([route, jnp.zeros((EXT - ROUTE_ROWS, rows), F32)], axis=0).T
    h2e_ref[:, :, D_MODEL:D_MODEL + EXT] = rec.reshape(bb_n, tq, EXT)


def _mixer(x, mods, consts, lw, s_a0, s_r0, offset, prev=None):
    batch, seq, _ = x.shape
    bb_n, tq = _tiling(batch, seq)
    nb, nt = batch // bb_n, seq // tq
    cos, sin = _rope_tables(seq, offset)
    sh1, sc1, g1, sh2, sc2 = mods
    gather = prev is not None
    if gather:
        pos, g2p, ys = prev
    else:
        pos = jnp.zeros((1,), jnp.int32)
        g2p = jnp.zeros((batch, 1, D_MODEL), F32)
        ys = jnp.zeros((ROUTE_ROWS, D_MODEL), F32)

    def per_b(shape):
        return pl.BlockSpec(shape, lambda b, t, pos: (b,) + (0,) * (len(shape) - 1))

    def const(shape):
        return pl.BlockSpec(shape, lambda b, t, pos: (0,) * len(shape))

    def resident(shape):
        return pl.BlockSpec(shape, lambda b, t, pos: (0,) * len(shape), pipeline_mode=pl.Buffered(1))

    mod_spec = per_b((bb_n, 1, D_MODEL))
    state_spec = per_b((bb_n, N_HEADS, D_HEAD, D_HEAD))
    rope_spec = pl.BlockSpec((tq, D_HEAD), lambda b, t, pos: (t, 0))
    in_specs = [
        pl.BlockSpec((bb_n, tq, D_MODEL), lambda b, t, pos: (b, t, 0)),
        mod_spec, pl.BlockSpec(memory_space=pl.ANY),
        mod_spec, mod_spec, mod_spec, mod_spec, mod_spec,
        const((1, D_MODEL)), const((1, D_MODEL)),
        resident((D_MODEL, IN_COLS)), resident((2 * G_WIDTH, D_MODEL)),
        const((1, G_WIDTH)), const((1, G_WIDTH)), const((1, G_WIDTH)),
        const((N_EXPERTS, D_MODEL)), const((N_EXPERTS, 1)),
        rope_spec, rope_spec,
        const((N_HEADS, CHUNK, CHUNK)), const((N_HEADS, CHUNK, D_HEAD)), const((N_HEADS, CHUNK, D_HEAD)),
        const((N_HEADS, 1, D_HEAD)), const((CHUNK, 3 * CHUNK)),
        pl.BlockSpec(memory_space=pl.ANY), pl.BlockSpec(memory_space=pl.ANY),
    ]
    out_specs = [
        pl.BlockSpec((bb_n, tq, D_MODEL), lambda b, t, pos: (b, t, 0)),
        pl.BlockSpec((bb_n, tq, D_MODEL + EXT), lambda b, t, pos: (b, t, 0)),
        pl.BlockSpec((1, ROUTE_ROWS, ROW_TILE), lambda b, t, pos: (b * nt + t, 0, 0)),
        state_spec, state_spec,
    ]
    out_shape = [
        jax.ShapeDtypeStruct((batch, seq, D_MODEL), F32),
        jax.ShapeDtypeStruct((batch, seq, D_MODEL + EXT), F32),
        jax.ShapeDtypeStruct((nb * nt, ROUTE_ROWS, ROW_TILE), F32),
        jax.ShapeDtypeStruct((batch, N_HEADS, D_HEAD, D_HEAD), F32),
        jax.ShapeDtypeStruct((batch, N_HEADS, D_HEAD, D_HEAD), F32),
    ]
    gather_rows = ROW_TILE if gather else ROUTE_ROWS
    return pl.pallas_call(
        functools.partial(_mixer_kernel, bb_n=bb_n, tq=tq, gather=gather),
        grid_spec=pltpu.PrefetchScalarGridSpec(
            num_scalar_prefetch=1,
            grid=(nb, nt),
            in_specs=in_specs,
            out_specs=out_specs,
            scratch_shapes=[
                pltpu.VMEM((ROW_TILE, IN_COLS), F32),
                pltpu.VMEM((ROW_TILE, 2 * G_WIDTH), BF16),
                pltpu.VMEM((bb_n, N_HEADS, D_HEAD, D_HEAD), F32),
                pltpu.VMEM((2, gather_rows, D_MODEL), F32),
                pltpu.SemaphoreType.DMA((2,)),
                pltpu.SemaphoreType.DMA((2,)),
            ],
        ),
        out_shape=out_shape,
        compiler_params=pltpu.CompilerParams(
            dimension_semantics=("arbitrary", "arbitrary"), vmem_limit_bytes=VMEM_LIMIT),
        name="mixer",
    )(pos, x, g2p, ys, sh1, sc1, g1, sh2, sc2, lw["na"], lw["nm"], lw["w_in"], lw["w_out"], lw["lb"], lw["ga"],
      lw["gr"], consts["wrt"], consts["br"], cos, sin, consts["intra"], consts["inter"], consts["kdec"],
      consts["sdec"], consts["tri"], s_a0, s_r0)


def _rope_tables(seq, offset):
    half = D_HEAD // 2
    freq = ROPE_BASE ** (-jnp.arange(half, dtype=F32) / half)
    ang = (jnp.arange(seq, dtype=F32) + offset)[:, None] * freq[None, :]
    cos, sin = jnp.cos(ang), jnp.sin(ang)
    return jnp.concatenate([cos, cos], axis=-1), jnp.concatenate([-sin, sin], axis=-1)


def _retention_tables():
    log_g = jnp.log(1.0 - 2.0 ** (-5.0 - jnp.arange(N_HEADS, dtype=F32)))
    idx = jnp.arange(CHUNK, dtype=F32)
    rel = idx[:, None] - idx[None, :]
    intra = jnp.where(rel >= 0, jnp.exp(jnp.maximum(rel, 0.0)[None] * log_g[:, None, None]), 0.0)
    inter = jnp.exp((idx[None, :] + 1.0) * log_g[:, None])[..., None]
    kdec = jnp.exp((CHUNK - 1.0 - idx[None, :]) * log_g[:, None])[..., None]
    sdec = jnp.exp(CHUNK * log_g)[:, None, None]
    wide = (N_HEADS, CHUNK, D_HEAD)
    return (intra, jnp.broadcast_to(inter, wide), jnp.broadcast_to(kdec, wide),
            jnp.broadcast_to(sdec, (N_HEADS, 1, D_HEAD)))


def _plan_kernel(route_ref, triu_ref, lstrict_ref, pos_ref, cnt_ref, carry_ref, base_ref):
    phase = pl.program_id(0)
    i = pl.program_id(1)
    combo = route_ref[0, 0:1, :]
    ids = lax.broadcasted_iota(jnp.int32, (COMBO_ROWS, ROW_TILE), 0).astype(F32)
    onehot = (ids == combo).astype(F32)
    tile_cnt = jnp.broadcast_to(jnp.sum(onehot, axis=1, keepdims=True), (COMBO_ROWS, 128))

    @pl.when((phase == 0) & (i == 0))
    def _():
        carry_ref[...] = jnp.zeros_like(carry_ref)

    @pl.when(phase == 0)
    def _():
        carry_ref[...] += tile_cnt

    @pl.when((phase == 1) & (i == 0))
    def _():
        cnt = carry_ref[...]
        cnt_ref[...] = cnt
        padded = jnp.floor((cnt + float(MOE_BLOCK - 1)) * (1.0 / MOE_BLOCK)) * float(MOE_BLOCK)
        base_ref[...] = jnp.dot(lstrict_ref[...], padded, precision=HIGHEST, preferred_element_type=F32)
        carry_ref[...] = jnp.zeros_like(carry_ref)

    @pl.when(phase == 1)
    def _():
        cum = _dot(onehot.astype(BF16), triu_ref[...])
        start = base_ref[:, 0:1] + carry_ref[:, 0:1]
        pos = jnp.sum(onehot * (cum - 1.0 + start), axis=0, keepdims=True)
        pos_ref[0] = jnp.broadcast_to(pos, (ROUTE_ROWS, ROW_TILE)).astype(jnp.int32)
        carry_ref[...] += tile_cnt


def _plan(route, consts):
    steps = route.shape[0]
    return pl.pallas_call(
        _plan_kernel,
        grid=(2, steps),
        in_specs=[
            pl.BlockSpec((1, ROUTE_ROWS, ROW_TILE), lambda p, i: (i, 0, 0)),
            pl.BlockSpec((ROW_TILE, ROW_TILE), lambda p, i: (0, 0)),
            pl.BlockSpec((COMBO_ROWS, COMBO_ROWS), lambda p, i: (0, 0)),
        ],
        out_specs=[
            pl.BlockSpec((1, ROUTE_ROWS, ROW_TILE), lambda p, i: (i * p, 0, 0)),
            pl.BlockSpec((COMBO_ROWS, 128), lambda p, i: (0, 0)),
        ],
        out_shape=[
            jax.ShapeDtypeStruct((steps, ROUTE_ROWS, ROW_TILE), jnp.int32),
            jax.ShapeDtypeStruct((COMBO_ROWS, 128), F32),
        ],
        scratch_shapes=[pltpu.VMEM((COMBO_ROWS, 128), F32), pltpu.VMEM((COMBO_ROWS, 128), F32)],
        compiler_params=pltpu.CompilerParams(dimension_semantics=("arbitrary", "arbitrary")),
        name="plan",
    )(route, consts["triu"], consts["lstrict"])


def _row_copy(src_ref, src_row, dst_ref, dst_row, sem):
    return pltpu.make_async_copy(src_ref.at[pl.ds(src_row, 1), :], dst_ref.at[pl.ds(dst_row, 1), :], sem)


def _scatter_kernel(pos_ref, h2e_ref, xs_in_ref, xs_ref, sem):
    del xs_in_ref
    base = pl.program_id(0) * ROW_TILE

    def issue(g, carry):
        r0 = pl.multiple_of(g * ISSUE_UNROLL, ISSUE_UNROLL)
        for j in range(ISSUE_UNROLL):
            _row_copy(h2e_ref, r0 + j, xs_ref, pos_ref[base + r0 + j], sem).start()
        return carry

    lax.fori_loop(0, ROW_TILE // ISSUE_UNROLL, issue, 0)
    pltpu.make_async_copy(h2e_ref, xs_ref.at[pl.ds(0, ROW_TILE), :], sem).wait()


def _scatter(pos, h2e, xs):
    n_tok = h2e.shape[0]
    width = h2e.shape[1]
    return pl.pallas_call(
        _scatter_kernel,
        grid_spec=pltpu.PrefetchScalarGridSpec(
            num_scalar_prefetch=1,
            grid=(n_tok // ROW_TILE,),
            in_specs=[
                pl.BlockSpec((ROW_TILE, width), lambda i, pos: (i, 0)),
                pl.BlockSpec(memory_space=pl.ANY),
            ],
            out_specs=pl.BlockSpec(memory_space=pl.ANY),
            scratch_shapes=[pltpu.SemaphoreType.DMA],
        ),
        out_shape=jax.ShapeDtypeStruct(xs.shape, xs.dtype),
        input_output_aliases={2: 0},
        compiler_params=pltpu.CompilerParams(dimension_semantics=("arbitrary",)),
        name="scatter",
    )(pos, h2e, xs)


def _moe_kernel(ea_ref, eb_ref, nu_ref, xs_ref, w1a, w3a, w2a, w1b, w3b, w2b, ys_ref):
    del ea_ref, eb_ref
    j = pl.program_id(0)

    @pl.when(j < nu_ref[0])
    def _():
        x = xs_ref[:, 0:D_MODEL].astype(BF16)
        w_lo = xs_ref[:, D_MODEL + 1:D_MODEL + 2]
        w_hi = xs_ref[:, D_MODEL + 2:D_MODEL + 3]

        def ffn(w1, w3, w2):
            h1 = _dot(x, w1[0])
            hid = (h1 * jax.nn.sigmoid(h1)) * _dot(x, w3[0])
            return _dot(hid.astype(BF16), w2[0])

        ys_ref[...] = ffn(w1a, w3a, w2a) * w_lo + ffn(w1b, w3b, w2b) * w_hi

    @pl.when(j >= nu_ref[0])
    def _():
        ys_ref[...] = jnp.zeros_like(ys_ref)


def _moe(ea, eb, n_used, xs, w1, w3, w2):
    n_rows, width = xs.shape
    n_blocks = n_rows // MOE_BLOCK

    def w_spec(which):
        if which == 0:
            return pl.BlockSpec((1, D_MODEL, D_MODEL), lambda j, ea, eb, nu: (ea[j], 0, 0))
        return pl.BlockSpec((1, D_MODEL, D_MODEL), lambda j, ea, eb, nu: (eb[j], 0, 0))

    return pl.pallas_call(
        _moe_kernel,
        grid_spec=pltpu.PrefetchScalarGridSpec(
            num_scalar_prefetch=3,
            grid=(n_blocks,),
            in_specs=[
                pl.BlockSpec((MOE_BLOCK, width), lambda j, ea, eb, nu: (jnp.maximum(jnp.minimum(j, nu[0] - 1), 0), 0)),
                w_spec(0), w_spec(0), w_spec(0), w_spec(1), w_spec(1), w_spec(1),
            ],
            out_specs=pl.BlockSpec((MOE_BLOCK, D_MODEL), lambda j, ea, eb, nu: (j, 0)),
        ),
        out_shape=jax.ShapeDtypeStruct((n_rows, D_MODEL), F32),
        compiler_params=pltpu.CompilerParams(dimension_semantics=("arbitrary",), vmem_limit_bytes=VMEM_LIMIT),
        name="moe",
    )(ea, eb, n_used, xs, w1, w3, w2, w1, w3, w2)


def _combine_kernel(pos_ref, x1_ref, g2_ref, nf_ref, ys_ref, out_ref, buf_ref, sem):
    step = pl.program_id(0) * pl.num_programs(1) + pl.program_id(1)
    base = step * ROW_TILE

    def issue(g, carry):
        r0 = pl.multiple_of(g * ISSUE_UNROLL, ISSUE_UNROLL)
        for j in range(ISSUE_UNROLL):
            _row_copy(ys_ref, pos_ref[base + r0 + j], buf_ref, r0 + j, sem).start()
        return carry

    lax.fori_loop(0, ROW_TILE // ISSUE_UNROLL, issue, 0)
    pltpu.make_async_copy(ys_ref.at[pl.ds(0, ROW_TILE), :], buf_ref, sem).wait()
    x2 = x1_ref[...] + g2_ref[...] * buf_ref[...].reshape(x1_ref.shape)
    out_ref[...] = x2 * lax.rsqrt(jnp.mean(x2 * x2, axis=-1, keepdims=True) + EPS) * nf_ref[...]


def _combine(pos, x1, g2, nf, ys):
    batch, seq, _ = x1.shape
    bb_n, tq = _tiling(batch, seq)
    tok_spec = pl.BlockSpec((bb_n, tq, D_MODEL), lambda b, t, pos: (b, t, 0))
    return pl.pallas_call(
        _combine_kernel,
        grid_spec=pltpu.PrefetchScalarGridSpec(
            num_scalar_prefetch=1,
            grid=(batch // bb_n, seq // tq),
            in_specs=[
                tok_spec,
                pl.BlockSpec((bb_n, 1, D_MODEL), lambda b, t, pos: (b, 0, 0)),
                pl.BlockSpec((1, D_MODEL), lambda b, t, pos: (0, 0)),
                pl.BlockSpec(memory_space=pl.ANY),
            ],
            out_specs=tok_spec,
            scratch_shapes=[pltpu.VMEM((ROW_TILE, D_MODEL), F32), pltpu.SemaphoreType.DMA],
        ),
        out_shape=jax.ShapeDtypeStruct(x1.shape, F32),
        compiler_params=pltpu.CompilerParams(dimension_semantics=("arbitrary", "arbitrary")),
        name="combine",
    )(pos, x1, g2, nf, ys)


def _block_table(cnt, n_blocks):
    counts = cnt[:N_COMBO, 0].astype(jnp.int32)
    padded = (counts + MOE_BLOCK - 1) // MOE_BLOCK * MOE_BLOCK
    pad_end = jnp.cumsum(padded)
    n_used = (pad_end[-1] // MOE_BLOCK).astype(jnp.int32)
    blk = jnp.minimum(jnp.arange(n_blocks, dtype=jnp.int32), n_used - 1)
    combo = jnp.sum((pad_end[None, :] <= (blk * MOE_BLOCK)[:, None]).astype(jnp.int32), axis=1)
    combo = jnp.minimum(combo, N_COMBO - 1)
    lo_tab = jnp.array([0, 0, 0, 1, 1, 2], jnp.int32)
    hi_tab = jnp.array([1, 2, 3, 2, 3, 3], jnp.int32)
    grp = combo // PAIRS_PER_GROUP
    pair = combo % PAIRS_PER_GROUP
    return grp * EXP_PER_GROUP + lo_tab[pair], grp * EXP_PER_GROUP + hi_tab[pair], n_used.reshape(1)


def kernel(x_prompt, x_sample, state_hgrn, state_ret, c_prompt, c_sample, w_in, w_out, lb_logits, hgrn_norm,
           ret_norm, norm_attn, norm_moe, w_ada, b_ada, w_router, b_router, w1, w3, w2, norm_f):
    bp, tp, _ = x_prompt.shape
    bs, ts, _ = x_sample.shape
    n_p, n_s = bp * tp, bs * ts
    n_tok = n_p + n_s
    n_blocks = n_tok // MOE_BLOCK + N_COMBO

    lb_cum = jnp.cumsum(jax.nn.softmax(lb_logits.astype(F32), axis=0), axis=0)
    lb_all = lb_cum - lb_cum[0:1]

    mod = _ada(jnp.concatenate([c_prompt, c_sample], axis=0), w_ada, b_ada)

    intra, inter, kdec, sdec = _retention_tables()
    consts = {
        "wrt": w_router.T.astype(F32), "br": b_router.astype(F32).reshape(N_EXPERTS, 1),
        "intra": intra, "inter": inter, "kdec": kdec, "sdec": sdec,
        "tri": jnp.tile(jnp.tril(jnp.ones((CHUNK, CHUNK), BF16)), (1, 3)),
        "triu": jnp.triu(jnp.ones((ROW_TILE, ROW_TILE), BF16)),
        "lstrict": jnp.tril(jnp.ones((COMBO_ROWS, COMBO_ROWS), F32), k=-1),
    }

    xp, xs_ = x_prompt, x_sample
    sa_p = jnp.zeros((bp, N_HEADS, D_HEAD, D_HEAD), F32)
    sr_p = jnp.zeros((bp, N_HEADS, D_HEAD, D_HEAD), F32)
    new_states = []
    prev_p = prev_s = None
    for l in range(DEPTH):
        lw = {
            "na": norm_attn[l].reshape(1, D_MODEL), "nm": norm_moe[l].reshape(1, D_MODEL),
            "w_in": w_in[l].astype(BF16), "w_out": w_out[l].astype(BF16),
            "lb": lb_all[l].reshape(1, G_WIDTH), "ga": hgrn_norm[l].reshape(1, G_WIDTH),
            "gr": ret_norm[l].reshape(1, G_WIDTH),
        }
        parts = jnp.split(mod[l], 6, axis=-1)
        mods_p = [p[:bp, None, :] for p in parts]
        mods_s = [p[bp:, None, :] for p in parts]
        x1p, h2p, route_p, sa_np, sr_np = _mixer(xp, mods_p[:5], consts, lw, sa_p, sr_p, 0, prev_p)
        x1s, h2s, route_s, sa_ns, sr_ns = _mixer(xs_, mods_s[:5], consts, lw, state_hgrn[l].astype(F32),
                                                 state_ret[l].astype(F32), PAST_LEN, prev_s)
        new_states.append((sa_np, sr_np, sa_ns, sr_ns))

        pos3, cnt = _plan(jnp.concatenate([route_p, route_s], axis=0), consts)
        pos = pos3[:, 0, :].reshape(n_tok)
        ea, eb, n_used = _block_table(cnt, n_blocks)

        xs = jnp.zeros((n_blocks * MOE_BLOCK, D_MODEL + EXT), F32)
        xs = _scatter(pos[:n_p], h2p.reshape(n_p, D_MODEL + EXT), xs)
        xs = _scatter(pos[n_p:], h2s.reshape(n_s, D_MODEL + EXT), xs)
        ys = _moe(ea, eb, n_used, xs, w1[l].astype(BF16), w3[l].astype(BF16), w2[l].astype(BF16))

        xp, xs_ = x1p, x1s
        prev_p = (pos[:n_p], mods_p[5], ys)
        prev_s = (pos[n_p:], mods_s[5], ys)

    nf = norm_f.reshape(1, D_MODEL)
    xp = _combine(prev_p[0], xp, prev_p[1], nf, prev_p[2])
    xs_ = _combine(prev_s[0], xs_, prev_s[1], nf, prev_s[2])

    sa_prompt = jnp.stack([s[0] for s in new_states])
    sr_prompt = jnp.stack([s[1] for s in new_states])
    sa_sample = jnp.stack([s[2] for s in new_states])
    sr_sample = jnp.stack([s[3] for s in new_states])
    return (xp, xs_, sa_prompt, sr_prompt, sa_sample, sr_sample)
```

```python
import functools

import jax
import jax.numpy as jnp
from jax import lax
from jax.experimental import pallas as pl
from jax.experimental.pallas import tpu as pltpu

F32 = jnp.float32
BF16 = jnp.bfloat16
HIGHEST = lax.Precision.HIGHEST

D_MODEL = 1024
DEPTH = 2
N_HEADS = 4
D_HEAD = 128
G_WIDTH = N_HEADS * D_HEAD
IN_COLS = 8 * G_WIDTH
CHUNK = 64
CHUNK_UNROLL = 2
STAGE_GAP = (None,) * 3
SUB = 8
LEVELS = (32, 16, 8)
N_EXPERTS = 16
EXP_PER_GROUP = 4
N_GROUPS = 4
PAIRS_PER_GROUP = 6
N_COMBO = N_GROUPS * PAIRS_PER_GROUP
COMBO_ROWS = 32
ROUTE_ROWS = 8
EXT = 128
ROW_TILE = 512
ISSUE_UNROLL = 8
MOE_BLOCK = 256
ADA_TILE = 1536
ROPE_BASE = 10000.0
PAST_LEN = 1024
EPS = 1e-6
LOG2E = 1.4426950408889634
VMEM_LIMIT = 56 * 1024 * 1024

_NT = (((1,), (1,)), ((), ()))
_TN = (((0,), (0,)), ((), ()))


def _dot(a, b):
    return jnp.dot(a, b, preferred_element_type=F32)


def _dot_nt(a, b):
    return lax.dot_general(a, b, _NT, preferred_element_type=F32)


def _dot_tn(a, b):
    return lax.dot_general(a, b, _TN, preferred_element_type=F32)


def _tiling(batch, seq):
    tq = min(seq, ROW_TILE)
    bb = ROW_TILE // tq
    assert tq % CHUNK == 0 and bb * tq == ROW_TILE and batch % bb == 0 and seq % tq == 0
    return bb, tq


def _ada_kernel(c_ref, w_ref, b_ref, o_ref):
    c = c_ref[...]
    cs = c * jax.nn.sigmoid(c)
    o_ref[0] = _dot(cs.astype(BF16), w_ref[0].astype(BF16)) + b_ref[0]


def _ada(c_all, w_ada, b_ada):
    m = c_all.shape[0]
    n = w_ada.shape[-1]
    return pl.pallas_call(
        _ada_kernel,
        grid=(DEPTH, n // ADA_TILE),
        in_specs=[
            pl.BlockSpec((m, D_MODEL), lambda l, j: (0, 0)),
            pl.BlockSpec((1, D_MODEL, ADA_TILE), lambda l, j: (l, 0, j)),
            pl.BlockSpec((1, 1, ADA_TILE), lambda l, j: (l, 0, j)),
        ],
        out_specs=pl.BlockSpec((1, m, ADA_TILE), lambda l, j: (l, 0, j)),
        out_shape=jax.ShapeDtypeStruct((DEPTH, m, n), F32),
        compiler_params=pltpu.CompilerParams(vmem_limit_bytes=VMEM_LIMIT),
        name="ada",
    )(c_all, w_ada, b_ada.reshape(DEPTH, 1, n))


def _route(lt, br):
    sc = jax.nn.sigmoid(lt)
    bi = sc + br
    s = [sc[e:e + 1] for e in range(N_EXPERTS)]
    v = [bi[e:e + 1] for e in range(N_EXPERTS)]
    gs = []
    for g in range(N_GROUPS):
        a = v[EXP_PER_GROUP * g:EXP_PER_GROUP * (g + 1)]
        m = a[0] + a[1]
        for i in range(EXP_PER_GROUP):
            for j in range(i + 1, EXP_PER_GROUP):
                if (i, j) != (0, 1):
                    m = jnp.maximum(m, a[i] + a[j])
        gs.append(m)
    best = gs[0]
    gsel = jnp.zeros_like(best)
    for g in range(1, N_GROUPS):
        upd = gs[g] > best
        best = jnp.where(upd, gs[g], best)
        gsel = jnp.where(upd, float(g), gsel)

    def pick(vals, key, n):
        out = vals[0]
        for g in range(1, n):
            out = jnp.where(key == float(g), vals[g], out)
        return out

    vv = [pick([v[EXP_PER_GROUP * g + j] for g in range(N_GROUPS)], gsel, N_GROUPS) for j in range(EXP_PER_GROUP)]
    ss = [pick([s[EXP_PER_GROUP * g + j] for g in range(N_GROUPS)], gsel, N_GROUPS) for j in range(EXP_PER_GROUP)]
    m1 = vv[0]
    i1 = jnp.zeros_like(m1)
    for j in range(1, EXP_PER_GROUP):
        upd = vv[j] > m1
        m1 = jnp.where(upd, vv[j], m1)
        i1 = jnp.where(upd, float(j), i1)
    m2 = jnp.full_like(m1, -jnp.inf)
    i2 = jnp.zeros_like(m1)
    for j in range(EXP_PER_GROUP):
        upd = (i1 != float(j)) & (vv[j] > m2)
        m2 = jnp.where(upd, vv[j], m2)
        i2 = jnp.where(upd, float(j), i2)
    s1 = pick(ss, i1, EXP_PER_GROUP)
    s2 = pick(ss, i2, EXP_PER_GROUP)
    den = s1 + s2
    w1 = s1 / den
    w2 = s2 / den
    first_lo = i1 < i2
    lo = jnp.where(first_lo, i1, i2)
    hi = jnp.where(first_lo, i2, i1)
    w_lo = jnp.where(first_lo, w1, w2)
    w_hi = jnp.where(first_lo, w2, w1)
    pair = jnp.where(lo == 0.0, hi - 1.0, jnp.where(lo == 1.0, hi + 1.0, 5.0))
    combo = gsel * float(PAIRS_PER_GROUP) + pair
    zero = jnp.zeros_like(combo)
    return jnp.concatenate([combo, w_lo, w_hi] + [zero] * (ROUTE_ROWS - 3), axis=0)


def _mixer_kernel(pos_ref, x_ref, g2p_ref, ys_ref, sh1_ref, sc1_ref, g1_ref, sh2_ref, sc2_ref, na_ref, nm_ref,
                  win_ref, wout_ref, lb_ref, ga_ref, gr_ref, wrt_ref, br_ref, cos_ref, sin_ref, intra_ref,
                  inter_ref, kdec_ref, sdec_ref, tri_ref, sa0_ref, sr0_ref,
                  x1_ref, h2e_ref, route_ref, sa_ref, sr_ref,
                  proj_ref, ob_ref, sat_ref, gbuf_ref, gsem, ssem, *, bb_n, tq, gather):
    tt = pl.program_id(1)
    rows = bb_n * tq
    n_chunks = tq // CHUNK
    n_iter = bb_n * n_chunks // CHUNK_UNROLL
    step = pl.program_id(0) * pl.num_programs(1) + tt
    last_step = pl.num_programs(0) * pl.num_programs(1) - 1

    x = x_ref[...]
    if gather:
        slot = step % 2

        def fetch(tile, dst_slot, r):
            _row_copy(ys_ref, pos_ref[tile * ROW_TILE + r], gbuf_ref.at[dst_slot], r, gsem.at[dst_slot]).start()

        @pl.when(step == 0)
        def _():
            def issue(g, carry):
                r0 = pl.multiple_of(g * ISSUE_UNROLL, ISSUE_UNROLL)
                for j in range(ISSUE_UNROLL):
                    fetch(0, 0, r0 + j)
                return carry
            lax.fori_loop(0, ROW_TILE // ISSUE_UNROLL, issue, 0)

        pltpu.make_async_copy(ys_ref.at[pl.ds(0, ROW_TILE), :], gbuf_ref.at[slot], gsem.at[slot]).wait()
        x = x + g2p_ref[...] * gbuf_ref[slot].reshape(bb_n, tq, D_MODEL)
    ms = jnp.mean(x * x, axis=-1, keepdims=True)
    h = x * lax.rsqrt(ms + EPS) * na_ref[...] * (1.0 + sc1_ref[...]) + sh1_ref[...]
    proj_ref[...] = _dot(h.reshape(rows, D_MODEL).astype(BF16), win_ref[...])

    @pl.when(tt == 0)
    def _():
        b0 = pl.program_id(0) * bb_n
        load_a = pltpu.make_async_copy(sa0_ref.at[pl.ds(b0, bb_n)], sa_ref, ssem.at[0])
        load_r = pltpu.make_async_copy(sr0_ref.at[pl.ds(b0, bb_n)], sr_ref, ssem.at[1])
        load_a.start()
        load_r.start()
        load_a.wait()
        load_r.wait()
        for b in range(bb_n):
            for hh in range(N_HEADS):
                sat_ref[b, hh] = sa_ref[b, hh].T

    lb = lb_ref[...]
    ln_1mlb = jnp.log1p(-lb)
    ga = ga_ref[...]
    gr = gr_ref[...]
    tri3 = tri_ref[...]
    t_i = lax.broadcasted_iota(jnp.int32, (CHUNK, CHUNK), 0)
    s_i = lax.broadcasted_iota(jnp.int32, (CHUNK, CHUNK), 1)
    level_masks = []
    for m in LEVELS:
        sh = m.bit_length() - 1
        level_masks.append(((t_i >> (sh + 1)) == (s_i >> (sh + 1)))
                           & (((t_i >> sh) & 1) == 1) & (((s_i >> sh) & 1) == 0))
    diag_mask = ((t_i >> 3) == (s_i >> 3)) & (s_i <= t_i)
    col_in_sub = s_i & (SUB - 1)

    def col(group, hh):
        return slice(group * G_WIDTH + hh * D_HEAD, group * G_WIDTH + (hh + 1) * D_HEAD)

    def hgrn_head(b, rs, hh):
        sl = slice(hh * D_HEAD, (hh + 1) * D_HEAD)
        z = proj_ref[rs, col(1, hh)]
        lbh = lb[:, sl]
        u = jnp.exp(-jnp.abs(z))
        ln_1pu = jnp.log(1.0 + u)
        num = jnp.where(z >= 0.0, 1.0 + lbh * u, lbh + u)
        l2f = (jnp.where(num > 0.0, jnp.log(num), z) - ln_1pu) * LOG2E
        l2k = (ln_1mlb[:, sl] - jnp.maximum(z, 0.0) - ln_1pu) * LOG2E
        hi = l2f.astype(BF16)
        rem = l2f - hi.astype(F32)
        mid = rem.astype(BF16)
        lo = (rem - mid.astype(F32)).astype(BF16)
        bcum = _dot(tri3, jnp.concatenate([hi, mid, lo], axis=0))
        yield from STAGE_GAP
        q = proj_ref[rs, col(0, hh)]
        vb = proj_ref[rs, col(2, hh)].astype(BF16)
        b_last = bcum[CHUNK - 1:CHUNK, :]
        q_in = (q * jnp.exp2(bcum)).astype(BF16)
        k_out = jnp.exp2(l2k + (b_last - bcum)).astype(BF16)
        upd = _dot_tn(vb, k_out)
        lvl = []
        for m in LEVELS:
            ref_row = bcum.reshape(CHUNK // (2 * m), 2 * m, D_HEAD)[:, m - 1:m, :]
            ref_row = jnp.broadcast_to(ref_row, (CHUNK // (2 * m), 2 * m, D_HEAD)).reshape(CHUNK, D_HEAD)
            dist = jnp.abs(bcum - ref_row)
            lvl.append(_dot_nt((q * jnp.exp2(-dist)).astype(BF16), jnp.exp2(l2k - dist).astype(BF16)))
        src3 = (l2k - bcum).reshape(CHUNK // SUB, SUB, D_HEAD)
        diag = jnp.zeros((CHUNK, CHUNK), F32)
        for sg in range(SUB):
            src = jnp.broadcast_to(src3[:, sg:sg + 1, :], src3.shape).reshape(CHUNK, D_HEAD)
            row_sum = jnp.sum(q * jnp.exp2(jnp.minimum(bcum + src, 0.0)), axis=-1, keepdims=True)
            diag = jnp.where(col_in_sub == sg, row_sum, diag)
        yield from STAGE_GAP
        a = jnp.where(diag_mask, diag, 0.0)
        for li in range(len(LEVELS)):
            a = jnp.where(level_masks[li], lvl[li], a)
        st = sat_ref[b, hh]
        o = _dot(a.astype(BF16), vb) + _dot_nt(q_in, st.astype(BF16))
        sat_ref[b, hh] = st * jnp.exp2(b_last) + upd
        yield from STAGE_GAP
        o = o * lax.rsqrt(jnp.mean(o * o, axis=-1, keepdims=True) + EPS)
        o = o * ga[:, sl] * jax.nn.sigmoid(proj_ref[rs, col(3, hh)])
        ob_ref[rs, sl] = o.astype(BF16)

    def ret_head(b, rs, t0, hh):
        sl = slice(hh * D_HEAD, (hh + 1) * D_HEAD)
        cs = cos_ref[pl.ds(t0, CHUNK), :]
        sn = sin_ref[pl.ds(t0, CHUNK), :]
        qh = proj_ref[rs, col(4, hh)]
        kh = proj_ref[rs, col(5, hh)]
        qr = qh * cs + pltpu.roll(qh, D_HEAD // 2, 1) * sn
        kr = (kh * cs + pltpu.roll(kh, D_HEAD // 2, 1) * sn) * (D_HEAD ** -0.5)
        qb = qr.astype(BF16)
        vh = proj_ref[rs, col(6, hh)].astype(BF16)
        sc = _dot_nt(qb, kr.astype(BF16))
        upd = _dot_tn((kr * kdec_ref[hh]).astype(BF16), vh)
        yield from STAGE_GAP
        s_r = sr_ref[b, hh]
        o = _dot((sc * intra_ref[hh]).astype(BF16), vh) + _dot(qb, s_r.astype(BF16)) * inter_ref[hh]
        sr_ref[b, hh] = s_r * sdec_ref[hh] + upd
        yield from STAGE_GAP
        mu = jnp.mean(o, axis=-1, keepdims=True)
        oc = o - mu
        var = jnp.mean(oc * oc, axis=-1, keepdims=True)
        o = oc * lax.rsqrt(var + EPS) * gr[:, sl]
        g = proj_ref[rs, col(7, hh)]
        o = o * (g * jax.nn.sigmoid(g))
        ob_ref[rs, col(1, hh)] = o.astype(BF16)

    def chunk_body(i, carry):
        if gather:
            next_tile = jnp.minimum(step + 1, last_step)
            per_iter = ROW_TILE // n_iter
            for j in range(per_iter):
                fetch(next_tile, 1 - slot, i * per_iter + j)
        heads = []
        for j in range(CHUNK_UNROLL):
            idx = i * CHUNK_UNROLL + j
            b = idx // n_chunks
            rs = pl.ds(pl.multiple_of(idx * CHUNK, CHUNK), CHUNK)
            t0 = pl.multiple_of((idx - b * n_chunks) * CHUNK, CHUNK)
            for hh in range(N_HEADS):
                heads.append(hgrn_head(b, rs, hh))
                heads.append(ret_head(b, rs, t0, hh))
        live = []
        while heads or live:
            if heads:
                live.append(heads.pop(0))
            nxt = []
            for h in live:
                try:
                    next(h)
                    nxt.append(h)
                except StopIteration:
                    pass
            live = nxt
        return carry

    lax.fori_loop(0, n_iter, chunk_body, 0)

    if gather:
        @pl.when(step == last_step)
        def _():
            pltpu.make_async_copy(ys_ref.at[pl.ds(0, ROW_TILE), :], gbuf_ref.at[1 - slot], gsem.at[1 - slot]).wait()

    @pl.when(tt == pl.num_programs(1) - 1)
    def _():
        for b in range(bb_n):
            for hh in range(N_HEADS):
                sa_ref[b, hh] = sat_ref[b, hh].T

    out = _dot(ob_ref[...], wout_ref[...]).reshape(bb_n, tq, D_MODEL)
    x1 = x + g1_ref[...] * out
    x1_ref[...] = x1
    ms2 = jnp.mean(x1 * x1, axis=-1, keepdims=True)
    h2 = x1 * lax.rsqrt(ms2 + EPS) * nm_ref[...] * (1.0 + sc2_ref[...]) + sh2_ref[...]
    h2e_ref[:, :, 0:D_MODEL] = h2
    lt = _dot_nt(wrt_ref[...], h2.reshape(rows, D_MODEL).astype(BF16))
    route = _route(lt, br_ref[...])
    route_ref[0] = route
    rec = jnp.concatenate([route, jnp.zeros((EXT - ROUTE_ROWS, rows), F32)], axis=0).T
    h2e_ref[:, :, D_MODEL:D_MODEL + EXT] = rec.reshape(bb_n, tq, EXT)


def _mixer(x, mods, consts, lw, s_a0, s_r0, offset, prev=None):
    batch, seq, _ = x.shape
    bb_n, tq = _tiling(batch, seq)
    nb, nt = batch // bb_n, seq // tq
    cos, sin = _rope_tables(seq, offset)
    sh1, sc1, g1, sh2, sc2 = mods
    gather = prev is not None
    if gather:
        pos, g2p, ys = prev
    else:
        pos = jnp.zeros((1,), jnp.int32)
        g2p = jnp.zeros((batch, 1, D_MODEL), F32)
        ys = jnp.zeros((ROUTE_ROWS, D_MODEL), F32)

    def per_b(shape):
        return pl.BlockSpec(shape, lambda b, t, pos: (b,) + (0,) * (len(shape) - 1))

    def const(shape):
        return pl.BlockSpec(shape, lambda b, t, pos: (0,) * len(shape))

    def resident(shape):
        return pl.BlockSpec(shape, lambda b, t, pos: (0,) * len(shape), pipeline_mode=pl.Buffered(1))

    mod_spec = per_b((bb_n, 1, D_MODEL))
    state_spec = per_b((bb_n, N_HEADS, D_HEAD, D_HEAD))
    rope_spec = pl.BlockSpec((tq, D_HEAD), lambda b, t, pos: (t, 0))
    in_specs = [
        pl.BlockSpec((bb_n, tq, D_MODEL), lambda b, t, pos: (b, t, 0)),
        mod_spec, pl.BlockSpec(memory_space=pl.ANY),
        mod_spec, mod_spec, mod_spec, mod_spec, mod_spec,
        const((1, D_MODEL)), const((1, D_MODEL)),
        resident((D_MODEL, IN_COLS)), resident((2 * G_WIDTH, D_MODEL)),
        const((1, G_WIDTH)), const((1, G_WIDTH)), const((1, G_WIDTH)),
        const((N_EXPERTS, D_MODEL)), const((N_EXPERTS, 1)),
        rope_spec, rope_spec,
        const((N_HEADS, CHUNK, CHUNK)), const((N_HEADS, CHUNK, D_HEAD)), const((N_HEADS, CHUNK, D_HEAD)),
        const((N_HEADS, 1, D_HEAD)), const((CHUNK, 3 * CHUNK)),
        pl.BlockSpec(memory_space=pl.ANY), pl.BlockSpec(memory_space=pl.ANY),
    ]
    out_specs = [
        pl.BlockSpec((bb_n, tq, D_MODEL), lambda b, t, pos: (b, t, 0)),
        pl.BlockSpec((bb_n, tq, D_MODEL + EXT), lambda b, t, pos: (b, t, 0)),
        pl.BlockSpec((1, ROUTE_ROWS, ROW_TILE), lambda b, t, pos: (b * nt + t, 0, 0)),
        state_spec, state_spec,
    ]
    out_shape = [
        jax.ShapeDtypeStruct((batch, seq, D_MODEL), F32),
        jax.ShapeDtypeStruct((batch, seq, D_MODEL + EXT), F32),
        jax.ShapeDtypeStruct((nb * nt, ROUTE_ROWS, ROW_TILE), F32),
        jax.ShapeDtypeStruct((batch, N_HEADS, D_HEAD, D_HEAD), F32),
        jax.ShapeDtypeStruct((batch, N_HEADS, D_HEAD, D_HEAD), F32),
    ]
    gather_rows = ROW_TILE if gather else ROUTE_ROWS
    return pl.pallas_call(
        functools.partial(_mixer_kernel, bb_n=bb_n, tq=tq, gather=gather),
        grid_spec=pltpu.PrefetchScalarGridSpec(
            num_scalar_prefetch=1,
            grid=(nb, nt),
            in_specs=in_specs,
            out_specs=out_specs,
            scratch_shapes=[
                pltpu.VMEM((ROW_TILE, IN_COLS), F32),
                pltpu.VMEM((ROW_TILE, 2 * G_WIDTH), BF16),
                pltpu.VMEM((bb_n, N_HEADS, D_HEAD, D_HEAD), F32),
                pltpu.VMEM((2, gather_rows, D_MODEL), F32),
                pltpu.SemaphoreType.DMA((2,)),
                pltpu.SemaphoreType.DMA((2,)),
            ],
        ),
        out_shape=out_shape,
        compiler_params=pltpu.CompilerParams(
            dimension_semantics=("arbitrary", "arbitrary"), vmem_limit_bytes=VMEM_LIMIT),
        name="mixer",
    )(pos, x, g2p, ys, sh1, sc1, g1, sh2, sc2, lw["na"], lw["nm"], lw["w_in"], lw["w_out"], lw["lb"], lw["ga"],
      lw["gr"], consts["wrt"], consts["br"], cos, sin, consts["intra"], consts["inter"], consts["kdec"],
      consts["sdec"], consts["tri"], s_a0, s_r0)


def _rope_tables(seq, offset):
    half = D_HEAD // 2
    freq = ROPE_BASE ** (-jnp.arange(half, dtype=F32) / half)
    ang = (jnp.arange(seq, dtype=F32) + offset)[:, None] * freq[None, :]
    cos, sin = jnp.cos(ang), jnp.sin(ang)
    return jnp.concatenate([cos, cos], axis=-1), jnp.concatenate([-sin, sin], axis=-1)


def _retention_tables():
    log_g = jnp.log(1.0 - 2.0 ** (-5.0 - jnp.arange(N_HEADS, dtype=F32)))
    idx = jnp.arange(CHUNK, dtype=F32)
    rel = idx[:, None] - idx[None, :]
    intra = jnp.where(rel >= 0, jnp.exp(jnp.maximum(rel, 0.0)[None] * log_g[:, None, None]), 0.0)
    inter = jnp.exp((idx[None, :] + 1.0) * log_g[:, None])[..., None]
    kdec = jnp.exp((CHUNK - 1.0 - idx[None, :]) * log_g[:, None])[..., None]
    sdec = jnp.exp(CHUNK * log_g)[:, None, None]
    wide = (N_HEADS, CHUNK, D_HEAD)
    return (intra, jnp.broadcast_to(inter, wide), jnp.broadcast_to(kdec, wide),
            jnp.broadcast_to(sdec, (N_HEADS, 1, D_HEAD)))


def _plan_kernel(route_ref, triu_ref, lstrict_ref, pos_ref, cnt_ref, carry_ref, base_ref):
    phase = pl.program_id(0)
    i = pl.program_id(1)
    combo = route_ref[0, 0:1, :]
    ids = lax.broadcasted_iota(jnp.int32, (COMBO_ROWS, ROW_TILE), 0).astype(F32)
    onehot = (ids == combo).astype(F32)
    tile_cnt = jnp.broadcast_to(jnp.sum(onehot, axis=1, keepdims=True), (COMBO_ROWS, 128))

    @pl.when((phase == 0) & (i == 0))
    def _():
        carry_ref[...] = jnp.zeros_like(carry_ref)

    @pl.when(phase == 0)
    def _():
        carry_ref[...] += tile_cnt

    @pl.when((phase == 1) & (i == 0))
    def _():
        cnt = carry_ref[...]
        cnt_ref[...] = cnt
        padded = jnp.floor((cnt + float(MOE_BLOCK - 1)) * (1.0 / MOE_BLOCK)) * float(MOE_BLOCK)
        base_ref[...] = jnp.dot(lstrict_ref[...], padded, precision=HIGHEST, preferred_element_type=F32)
        carry_ref[...] = jnp.zeros_like(carry_ref)

    @pl.when(phase == 1)
    def _():
        cum = _dot(onehot.astype(BF16), triu_ref[...])
        start = base_ref[:, 0:1] + carry_ref[:, 0:1]
        pos = jnp.sum(onehot * (cum - 1.0 + start), axis=0, keepdims=True)
        pos_ref[0] = jnp.broadcast_to(pos, (ROUTE_ROWS, ROW_TILE)).astype(jnp.int32)
        carry_ref[...] += tile_cnt


def _plan(route, consts):
    steps = route.shape[0]
    return pl.pallas_call(
        _plan_kernel,
        grid=(2, steps),
        in_specs=[
            pl.BlockSpec((1, ROUTE_ROWS, ROW_TILE), lambda p, i: (i, 0, 0)),
            pl.BlockSpec((ROW_TILE, ROW_TILE), lambda p, i: (0, 0)),
            pl.BlockSpec((COMBO_ROWS, COMBO_ROWS), lambda p, i: (0, 0)),
        ],
        out_specs=[
            pl.BlockSpec((1, ROUTE_ROWS, ROW_TILE), lambda p, i: (i * p, 0, 0)),
            pl.BlockSpec((COMBO_ROWS, 128), lambda p, i: (0, 0)),
        ],
        out_shape=[
            jax.ShapeDtypeStruct((steps, ROUTE_ROWS, ROW_TILE), jnp.int32),
            jax.ShapeDtypeStruct((COMBO_ROWS, 128), F32),
        ],
        scratch_shapes=[pltpu.VMEM((COMBO_ROWS, 128), F32), pltpu.VMEM((COMBO_ROWS, 128), F32)],
        compiler_params=pltpu.CompilerParams(dimension_semantics=("arbitrary", "arbitrary")),
        name="plan",
    )(route, consts["triu"], consts["lstrict"])


def _row_copy(src_ref, src_row, dst_ref, dst_row, sem):
    return pltpu.make_async_copy(src_ref.at[pl.ds(src_row, 1), :], dst_ref.at[pl.ds(dst_row, 1), :], sem)


def _scatter_kernel(pos_ref, h2e_ref, xs_in_ref, xs_ref, sem):
    del xs_in_ref
    base = pl.program_id(0) * ROW_TILE

    def issue(g, carry):
        r0 = pl.multiple_of(g * ISSUE_UNROLL, ISSUE_UNROLL)
        for j in range(ISSUE_UNROLL):
            _row_copy(h2e_ref, r0 + j, xs_ref, pos_ref[base + r0 + j], sem).start()
        return carry

    lax.fori_loop(0, ROW_TILE // ISSUE_UNROLL, issue, 0)
    pltpu.make_async_copy(h2e_ref, xs_ref.at[pl.ds(0, ROW_TILE), :], sem).wait()


def _scatter(pos, h2e, xs):
    n_tok = h2e.shape[0]
    width = h2e.shape[1]
    return pl.pallas_call(
        _scatter_kernel,
        grid_spec=pltpu.PrefetchScalarGridSpec(
            num_scalar_prefetch=1,
            grid=(n_tok // ROW_TILE,),
            in_specs=[
                pl.BlockSpec((ROW_TILE, width), lambda i, pos: (i, 0)),
                pl.BlockSpec(memory_space=pl.ANY),
            ],
            out_specs=pl.BlockSpec(memory_space=pl.ANY),
            scratch_shapes=[pltpu.SemaphoreType.DMA],
        ),
        out_shape=jax.ShapeDtypeStruct(xs.shape, xs.dtype),
        input_output_aliases={2: 0},
        compiler_params=pltpu.CompilerParams(dimension_semantics=("arbitrary",)),
        name="scatter",
    )(pos, h2e, xs)


def _moe_kernel(ea_ref, eb_ref, nu_ref, xs_ref, w1a, w3a, w2a, w1b, w3b, w2b, ys_ref):
    del ea_ref, eb_ref
    j = pl.program_id(0)

    @pl.when(j < nu_ref[0])
    def _():
        x = xs_ref[:, 0:D_MODEL].astype(BF16)
        w_lo = xs_ref[:, D_MODEL + 1:D_MODEL + 2]
        w_hi = xs_ref[:, D_MODEL + 2:D_MODEL + 3]

        def ffn(w1, w3, w2):
            h1 = _dot(x, w1[0])
            hid = (h1 * jax.nn.sigmoid(h1)) * _dot(x, w3[0])
            return _dot(hid.astype(BF16), w2[0])

        ys_ref[...] = ffn(w1a, w3a, w2a) * w_lo + ffn(w1b, w3b, w2b) * w_hi

    @pl.when(j >= nu_ref[0])
    def _():
        ys_ref[...] = jnp.zeros_like(ys_ref)


def _moe(ea, eb, n_used, xs, w1, w3, w2):
    n_rows, width = xs.shape
    n_blocks = n_rows // MOE_BLOCK

    def w_spec(which):
        if which == 0:
            return pl.BlockSpec((1, D_MODEL, D_MODEL), lambda j, ea, eb, nu: (ea[j], 0, 0))
        return pl.BlockSpec((1, D_MODEL, D_MODEL), lambda j, ea, eb, nu: (eb[j], 0, 0))

    return pl.pallas_call(
        _moe_kernel,
        grid_spec=pltpu.PrefetchScalarGridSpec(
            num_scalar_prefetch=3,
            grid=(n_blocks,),
            in_specs=[
                pl.BlockSpec((MOE_BLOCK, width), lambda j, ea, eb, nu: (jnp.maximum(jnp.minimum(j, nu[0] - 1), 0), 0)),
                w_spec(0), w_spec(0), w_spec(0), w_spec(1), w_spec(1), w_spec(1),
            ],
            out_specs=pl.BlockSpec((MOE_BLOCK, D_MODEL), lambda j, ea, eb, nu: (j, 0)),
        ),
        out_shape=jax.ShapeDtypeStruct((n_rows, D_MODEL), F32),
        compiler_params=pltpu.CompilerParams(dimension_semantics=("arbitrary",), vmem_limit_bytes=VMEM_LIMIT),
        name="moe",
    )(ea, eb, n_used, xs, w1, w3, w2, w1, w3, w2)


def _combine_kernel(pos_ref, x1_ref, g2_ref, nf_ref, ys_ref, out_ref, buf_ref, sem):
    step = pl.program_id(0) * pl.num_programs(1) + pl.program_id(1)
    base = step * ROW_TILE

    def issue(g, carry):
        r0 = pl.multiple_of(g * ISSUE_UNROLL, ISSUE_UNROLL)
        for j in range(ISSUE_UNROLL):
            _row_copy(ys_ref, pos_ref[base + r0 + j], buf_ref, r0 + j, sem).start()
        return carry

    lax.fori_loop(0, ROW_TILE // ISSUE_UNROLL, issue, 0)
    pltpu.make_async_copy(ys_ref.at[pl.ds(0, ROW_TILE), :], buf_ref, sem).wait()
    x2 = x1_ref[...] + g2_ref[...] * buf_ref[...].reshape(x1_ref.shape)
    out_ref[...] = x2 * lax.rsqrt(jnp.mean(x2 * x2, axis=-1, keepdims=True) + EPS) * nf_ref[...]


def _combine(pos, x1, g2, nf, ys):
    batch, seq, _ = x1.shape
    bb_n, tq = _tiling(batch, seq)
    tok_spec = pl.BlockSpec((bb_n, tq, D_MODEL), lambda b, t, pos: (b, t, 0))
    return pl.pallas_call(
        _combine_kernel,
        grid_spec=pltpu.PrefetchScalarGridSpec(
            num_scalar_prefetch=1,
            grid=(batch // bb_n, seq // tq),
            in_specs=[
                tok_spec,
                pl.BlockSpec((bb_n, 1, D_MODEL), lambda b, t, pos: (b, 0, 0)),
                pl.BlockSpec((1, D_MODEL), lambda b, t, pos: (0, 0)),
                pl.BlockSpec(memory_space=pl.ANY),
            ],
            out_specs=tok_spec,
            scratch_shapes=[pltpu.VMEM((ROW_TILE, D_MODEL), F32), pltpu.SemaphoreType.DMA],
        ),
        out_shape=jax.ShapeDtypeStruct(x1.shape, F32),
        compiler_params=pltpu.CompilerParams(dimension_semantics=("arbitrary", "arbitrary")),
        name="combine",
    )(pos, x1, g2, nf, ys)


def _block_table(cnt, n_blocks):
    counts = cnt[:N_COMBO, 0].astype(jnp.int32)
    padded = (counts + MOE_BLOCK - 1) // MOE_BLOCK * MOE_BLOCK
    pad_end = jnp.cumsum(padded)
    n_used = (pad_end[-1] // MOE_BLOCK).astype(jnp.int32)
    blk = jnp.minimum(jnp.arange(n_blocks, dtype=jnp.int32), n_used - 1)
    combo = jnp.sum((pad_end[None, :] <= (blk * MOE_BLOCK)[:, None]).astype(jnp.int32), axis=1)
    combo = jnp.minimum(combo, N_COMBO - 1)
    lo_tab = jnp.array([0, 0, 0, 1, 1, 2], jnp.int32)
    hi_tab = jnp.array([1, 2, 3, 2, 3, 3], jnp.int32)
    grp = combo // PAIRS_PER_GROUP
    pair = combo % PAIRS_PER_GROUP
    return grp * EXP_PER_GROUP + lo_tab[pair], grp * EXP_PER_GROUP + hi_tab[pair], n_used.reshape(1)


def kernel(x_prompt, x_sample, state_hgrn, state_ret, c_prompt, c_sample, w_in, w_out, lb_logits, hgrn_norm,
           ret_norm, norm_attn, norm_moe, w_ada, b_ada, w_router, b_router, w1, w3, w2, norm_f):
    bp, tp, _ = x_prompt.shape
    bs, ts, _ = x_sample.shape
    n_p, n_s = bp * tp, bs * ts
    n_tok = n_p + n_s
    n_blocks = n_tok // MOE_BLOCK + N_COMBO

    lb_cum = jnp.cumsum(jax.nn.softmax(lb_logits.astype(F32), axis=0), axis=0)
    lb_all = lb_cum - lb_cum[0:1]

    mod = _ada(jnp.concatenate([c_prompt, c_sample], axis=0), w_ada, b_ada)

    intra, inter, kdec, sdec = _retention_tables()
    consts = {
        "wrt": w_router.T.astype(BF16), "br": b_router.astype(F32).reshape(N_EXPERTS, 1),
        "intra": intra, "inter": inter, "kdec": kdec, "sdec": sdec,
        "tri": jnp.tile(jnp.tril(jnp.ones((CHUNK, CHUNK), BF16)), (1, 3)),
        "triu": jnp.triu(jnp.ones((ROW_TILE, ROW_TILE), BF16)),
        "lstrict": jnp.tril(jnp.ones((COMBO_ROWS, COMBO_ROWS), F32), k=-1),
    }

    xp, xs_ = x_prompt, x_sample
    sa_p = jnp.zeros((bp, N_HEADS, D_HEAD, D_HEAD), F32)
    sr_p = jnp.zeros((bp, N_HEADS, D_HEAD, D_HEAD), F32)
    new_states = []
    prev_p = prev_s = None
    for l in range(DEPTH):
        lw = {
            "na": norm_attn[l].reshape(1, D_MODEL), "nm": norm_moe[l].reshape(1, D_MODEL),
            "w_in": w_in[l].astype(BF16), "w_out": w_out[l].astype(BF16),
            "lb": lb_all[l].reshape(1, G_WIDTH), "ga": hgrn_norm[l].reshape(1, G_WIDTH),
            "gr": ret_norm[l].reshape(1, G_WIDTH),
        }
        parts = jnp.split(mod[l], 6, axis=-1)
        mods_p = [p[:bp, None, :] for p in parts]
        mods_s = [p[bp:, None, :] for p in parts]
        x1p, h2p, route_p, sa_np, sr_np = _mixer(xp, mods_p[:5], consts, lw, sa_p, sr_p, 0, prev_p)
        x1s, h2s, route_s, sa_ns, sr_ns = _mixer(xs_, mods_s[:5], consts, lw, state_hgrn[l].astype(F32),
                                                 state_ret[l].astype(F32), PAST_LEN, prev_s)
        new_states.append((sa_np, sr_np, sa_ns, sr_ns))

        pos3, cnt = _plan(jnp.concatenate([route_p, route_s], axis=0), consts)
        pos = pos3[:, 0, :].reshape(n_tok)
        ea, eb, n_used = _block_table(cnt, n_blocks)

        xs = jnp.zeros((n_blocks * MOE_BLOCK, D_MODEL + EXT), F32)
        xs = _scatter(pos[:n_p], h2p.reshape(n_p, D_MODEL + EXT), xs)
        xs = _scatter(pos[n_p:], h2s.reshape(n_s, D_MODEL + EXT), xs)
        ys = _moe(ea, eb, n_used, xs, w1[l].astype(BF16), w3[l].astype(BF16), w2[l].astype(BF16))

        xp, xs_ = x1p, x1s
        prev_p = (pos[:n_p], mods_p[5], ys)
        prev_s = (pos[n_p:], mods_s[5], ys)

    nf = norm_f.reshape(1, D_MODEL)
    xp = _combine(prev_p[0], xp, prev_p[1], nf, prev_p[2])
    xs_ = _combine(prev_s[0], xs_, prev_s[1], nf, prev_s[2])

    sa_prompt = jnp.stack([s[0] for s in new_states])
    sr_prompt = jnp.stack([s[1] for s in new_states])
    sa_sample = jnp.stack([s[2] for s in new_states])
    sr_sample = jnp.stack([s[3] for s in new_states])
    return (xp, xs_, sa_prompt, sr_prompt, sa_sample, sr_sample)
```

```python
import functools

import jax
import jax.numpy as jnp
from jax import lax
from jax.experimental import pallas as pl
from jax.experimental.pallas import tpu as pltpu

F32 = jnp.float32
BF16 = jnp.bfloat16

D_MODEL = 1024
DEPTH = 2
N_HEADS = 4
D_HEAD = 128
G_WIDTH = N_HEADS * D_HEAD
IN_COLS = 8 * G_WIDTH
CHUNK = 64
CHUNK_UNROLL = 2
STAGE_GAP = (None,) * 3
SUB = 8
LEVELS = (32, 16, 8)
N_EXPERTS = 16
EXP_PER_GROUP = 4
N_GROUPS = 4
PAIRS_PER_GROUP = 6
N_COMBO = N_GROUPS * PAIRS_PER_GROUP
COMBO_ROWS = 32
ROUTE_ROWS = 8
EXT = 128
ROW_TILE = 512
ISSUE_UNROLL = 8
MOE_BLOCK = 256
BLOCK_LANES = 384
ADA_TILE = 1536
ROPE_BASE = 10000.0
PAST_LEN = 1024
EPS = 1e-6
LOG2E = 1.4426950408889634
VMEM_LIMIT = 56 * 1024 * 1024

_NT = (((1,), (1,)), ((), ()))
_TN = (((0,), (0,)), ((), ()))


def _dot(a, b):
    return jnp.dot(a, b, preferred_element_type=F32)


def _dot_nt(a, b):
    return lax.dot_general(a, b, _NT, preferred_element_type=F32)


def _dot_tn(a, b):
    return lax.dot_general(a, b, _TN, preferred_element_type=F32)


def _tiling(batch, seq):
    tq = min(seq, ROW_TILE)
    bb = ROW_TILE // tq
    assert tq % CHUNK == 0 and bb * tq == ROW_TILE and batch % bb == 0 and seq % tq == 0
    return bb, tq


def _ada_kernel(c_ref, w_ref, b_ref, o_ref):
    c = c_ref[...]
    cs = c * jax.nn.sigmoid(c)
    o_ref[0] = _dot(cs.astype(BF16), w_ref[0].astype(BF16)) + b_ref[0]


def _ada(c_all, w_ada, b_ada):
    m = c_all.shape[0]
    n = w_ada.shape[-1]
    return pl.pallas_call(
        _ada_kernel,
        grid=(DEPTH, n // ADA_TILE),
        in_specs=[
            pl.BlockSpec((m, D_MODEL), lambda l, j: (0, 0)),
            pl.BlockSpec((1, D_MODEL, ADA_TILE), lambda l, j: (l, 0, j)),
            pl.BlockSpec((1, 1, ADA_TILE), lambda l, j: (l, 0, j)),
        ],
        out_specs=pl.BlockSpec((1, m, ADA_TILE), lambda l, j: (l, 0, j)),
        out_shape=jax.ShapeDtypeStruct((DEPTH, m, n), F32),
        compiler_params=pltpu.CompilerParams(vmem_limit_bytes=VMEM_LIMIT),
        name="ada",
    )(c_all, w_ada, b_ada.reshape(DEPTH, 1, n))


def _route(lt, br):
    sc = jax.nn.sigmoid(lt)
    bi = sc + br
    s = [sc[e:e + 1] for e in range(N_EXPERTS)]
    v = [bi[e:e + 1] for e in range(N_EXPERTS)]
    gs = []
    for g in range(N_GROUPS):
        a = v[EXP_PER_GROUP * g:EXP_PER_GROUP * (g + 1)]
        m = a[0] + a[1]
        for i in range(EXP_PER_GROUP):
            for j in range(i + 1, EXP_PER_GROUP):
                if (i, j) != (0, 1):
                    m = jnp.maximum(m, a[i] + a[j])
        gs.append(m)
    best = gs[0]
    gsel = jnp.zeros_like(best)
    for g in range(1, N_GROUPS):
        upd = gs[g] > best
        best = jnp.where(upd, gs[g], best)
        gsel = jnp.where(upd, float(g), gsel)

    def pick(vals, key, n):
        out = vals[0]
        for g in range(1, n):
            out = jnp.where(key == float(g), vals[g], out)
        return out

    vv = [pick([v[EXP_PER_GROUP * g + j] for g in range(N_GROUPS)], gsel, N_GROUPS) for j in range(EXP_PER_GROUP)]
    ss = [pick([s[EXP_PER_GROUP * g + j] for g in range(N_GROUPS)], gsel, N_GROUPS) for j in range(EXP_PER_GROUP)]
    m1 = vv[0]
    i1 = jnp.zeros_like(m1)
    for j in range(1, EXP_PER_GROUP):
        upd = vv[j] > m1
        m1 = jnp.where(upd, vv[j], m1)
        i1 = jnp.where(upd, float(j), i1)
    m2 = jnp.full_like(m1, -jnp.inf)
    i2 = jnp.zeros_like(m1)
    for j in range(EXP_PER_GROUP):
        upd = (i1 != float(j)) & (vv[j] > m2)
        m2 = jnp.where(upd, vv[j], m2)
        i2 = jnp.where(upd, float(j), i2)
    s1 = pick(ss, i1, EXP_PER_GROUP)
    s2 = pick(ss, i2, EXP_PER_GROUP)
    den = s1 + s2
    w1 = s1 / den
    w2 = s2 / den
    first_lo = i1 < i2
    lo = jnp.where(first_lo, i1, i2)
    hi = jnp.where(first_lo, i2, i1)
    w_lo = jnp.where(first_lo, w1, w2)
    w_hi = jnp.where(first_lo, w2, w1)
    pair = jnp.where(lo == 0.0, hi - 1.0, jnp.where(lo == 1.0, hi + 1.0, 5.0))
    combo = gsel * float(PAIRS_PER_GROUP) + pair
    zero = jnp.zeros_like(combo)
    return jnp.concatenate([combo, w_lo, w_hi] + [zero] * (ROUTE_ROWS - 3), axis=0)


def _mixer_kernel(pos_ref, x_ref, g2p_ref, ys_ref, sh1_ref, sc1_ref, g1_ref, sh2_ref, sc2_ref, na_ref, nm_ref,
                  win_ref, wout_ref, lb_ref, ga_ref, gr_ref, wrt_ref, br_ref, cos_ref, sin_ref, intra_ref,
                  inter_ref, kdec_ref, sdec_ref, tri_ref, triu_ref, lstrict_ref, sa0_ref, sr0_ref,
                  dsp0_ref, bc0_ref, xs_in_ref,
                  x1_ref, pos3_ref, sa_ref, sr_ref, dsp_ref, bc_ref, xs_ref,
                  proj_ref, ob_ref, sat_ref, gbuf_ref, hbuf_ref, posv_ref, poss_ref, gsem, ssem, psem, scsem,
                  *, bb_n, tq, gather, dummy_row):
    del xs_in_ref
    tt = pl.program_id(1)
    rows = bb_n * tq
    n_chunks = tq // CHUNK
    n_iter = bb_n * n_chunks // CHUNK_UNROLL
    step = pl.program_id(0) * pl.num_programs(1) + tt
    last_step = pl.num_programs(0) * pl.num_programs(1) - 1

    x = x_ref[...]
    if gather:
        slot = step % 2

        def fetch(tile, dst_slot, r):
            _row_copy(ys_ref, pos_ref[tile * ROW_TILE + r], gbuf_ref.at[dst_slot], r, gsem.at[dst_slot]).start()

        @pl.when(step == 0)
        def _():
            def issue(g, carry):
                r0 = pl.multiple_of(g * ISSUE_UNROLL, ISSUE_UNROLL)
                for j in range(ISSUE_UNROLL):
                    fetch(0, 0, r0 + j)
                return carry
            lax.fori_loop(0, ROW_TILE // ISSUE_UNROLL, issue, 0)

        pltpu.make_async_copy(ys_ref.at[pl.ds(0, ROW_TILE), :], gbuf_ref.at[slot], gsem.at[slot]).wait()
        x = x + g2p_ref[...] * gbuf_ref[slot].reshape(bb_n, tq, D_MODEL)
    ms = jnp.mean(x * x, axis=-1, keepdims=True)
    h = x * lax.rsqrt(ms + EPS) * na_ref[...] * (1.0 + sc1_ref[...]) + sh1_ref[...]
    proj_ref[...] = _dot(h.reshape(rows, D_MODEL).astype(BF16), win_ref[...])

    @pl.when(tt == 0)
    def _():
        b0 = pl.program_id(0) * bb_n
        load_a = pltpu.make_async_copy(sa0_ref.at[pl.ds(b0, bb_n)], sa_ref, ssem.at[0])
        load_r = pltpu.make_async_copy(sr0_ref.at[pl.ds(b0, bb_n)], sr_ref, ssem.at[1])
        load_a.start()
        load_r.start()
        load_a.wait()
        load_r.wait()
        for b in range(bb_n):
            for hh in range(N_HEADS):
                sat_ref[b, hh] = sa_ref[b, hh].T

    @pl.when(step == 0)
    def _():
        dsp_ref[...] = dsp0_ref[...]
        bc_ref[...] = bc0_ref[...]
        hbuf_ref[...] = jnp.zeros_like(hbuf_ref)
        posv_ref[...] = jnp.zeros_like(posv_ref)
        init = pltpu.make_async_copy(posv_ref.at[0], poss_ref, psem)
        init.start()
        init.wait()

    def put(r):
        dst = jnp.where(step > 0, poss_ref[r], dummy_row + r)
        _row_copy(hbuf_ref, r, xs_ref, dst, scsem).start()

    lb = lb_ref[...]
    ln_1mlb = jnp.log1p(-lb)
    ga = ga_ref[...]
    gr = gr_ref[...]
    tri3 = tri_ref[...]
    t_i = lax.broadcasted_iota(jnp.int32, (CHUNK, CHUNK), 0)
    s_i = lax.broadcasted_iota(jnp.int32, (CHUNK, CHUNK), 1)
    level_masks = []
    for m in LEVELS:
        sh = m.bit_length() - 1
        level_masks.append(((t_i >> (sh + 1)) == (s_i >> (sh + 1)))
                           & (((t_i >> sh) & 1) == 1) & (((s_i >> sh) & 1) == 0))
    diag_mask = ((t_i >> 3) == (s_i >> 3)) & (s_i <= t_i)
    col_in_sub = s_i & (SUB - 1)

    def col(group, hh):
        return slice(group * G_WIDTH + hh * D_HEAD, group * G_WIDTH + (hh + 1) * D_HEAD)

    def hgrn_head(b, rs, hh):
        sl = slice(hh * D_HEAD, (hh + 1) * D_HEAD)
        z = proj_ref[rs, col(1, hh)]
        lbh = lb[:, sl]
        u = jnp.exp(-jnp.abs(z))
        ln_1pu = jnp.log(1.0 + u)
        num = jnp.where(z >= 0.0, 1.0 + lbh * u, lbh + u)
        l2f = (jnp.where(num > 0.0, jnp.log(num), z) - ln_1pu) * LOG2E
        l2k = (ln_1mlb[:, sl] - jnp.maximum(z, 0.0) - ln_1pu) * LOG2E
        hi = l2f.astype(BF16)
        rem = l2f - hi.astype(F32)
        mid = rem.astype(BF16)
        lo = (rem - mid.astype(F32)).astype(BF16)
        bcum = _dot(tri3, jnp.concatenate([hi, mid, lo], axis=0))
        yield from STAGE_GAP
        q = proj_ref[rs, col(0, hh)]
        vb = proj_ref[rs, col(2, hh)].astype(BF16)
        b_last = bcum[CHUNK - 1:CHUNK, :]
        q_in = (q * jnp.exp2(bcum)).astype(BF16)
        k_out = jnp.exp2(l2k + (b_last - bcum)).astype(BF16)
        upd = _dot_tn(vb, k_out)
        lvl = []
        for m in LEVELS:
            ref_row = bcum.reshape(CHUNK // (2 * m), 2 * m, D_HEAD)[:, m - 1:m, :]
            ref_row = jnp.broadcast_to(ref_row, (CHUNK // (2 * m), 2 * m, D_HEAD)).reshape(CHUNK, D_HEAD)
            dist = jnp.abs(bcum - ref_row)
            lvl.append(_dot_nt((q * jnp.exp2(-dist)).astype(BF16), jnp.exp2(l2k - dist).astype(BF16)))
        src3 = (l2k - bcum).reshape(CHUNK // SUB, SUB, D_HEAD)
        diag = jnp.zeros((CHUNK, CHUNK), F32)
        for sg in range(SUB):
            src = jnp.broadcast_to(src3[:, sg:sg + 1, :], src3.shape).reshape(CHUNK, D_HEAD)
            row_sum = jnp.sum(q * jnp.exp2(jnp.minimum(bcum + src, 0.0)), axis=-1, keepdims=True)
            diag = jnp.where(col_in_sub == sg, row_sum, diag)
        yield from STAGE_GAP
        a = jnp.where(diag_mask, diag, 0.0)
        for li in range(len(LEVELS)):
            a = jnp.where(level_masks[li], lvl[li], a)
        st = sat_ref[b, hh]
        o = _dot(a.astype(BF16), vb) + _dot_nt(q_in, st.astype(BF16))
        sat_ref[b, hh] = st * jnp.exp2(b_last) + upd
        yield from STAGE_GAP
        o = o * lax.rsqrt(jnp.mean(o * o, axis=-1, keepdims=True) + EPS)
        o = o * ga[:, sl] * jax.nn.sigmoid(proj_ref[rs, col(3, hh)])
        ob_ref[rs, sl] = o.astype(BF16)

    def ret_head(b, rs, t0, hh):
        sl = slice(hh * D_HEAD, (hh + 1) * D_HEAD)
        cs = cos_ref[pl.ds(t0, CHUNK), :]
        sn = sin_ref[pl.ds(t0, CHUNK), :]
        qh = proj_ref[rs, col(4, hh)]
        kh = proj_ref[rs, col(5, hh)]
        qr = qh * cs + pltpu.roll(qh, D_HEAD // 2, 1) * sn
        kr = (kh * cs + pltpu.roll(kh, D_HEAD // 2, 1) * sn) * (D_HEAD ** -0.5)
        qb = qr.astype(BF16)
        vh = proj_ref[rs, col(6, hh)].astype(BF16)
        sc = _dot_nt(qb, kr.astype(BF16))
        upd = _dot_tn((kr * kdec_ref[hh]).astype(BF16), vh)
        yield from STAGE_GAP
        s_r = sr_ref[b, hh]
        o = _dot((sc * intra_ref[hh]).astype(BF16), vh) + _dot(qb, s_r.astype(BF16)) * inter_ref[hh]
        sr_ref[b, hh] = s_r * sdec_ref[hh] + upd
        yield from STAGE_GAP
        mu = jnp.mean(o, axis=-1, keepdims=True)
        oc = o - mu
        var = jnp.mean(oc * oc, axis=-1, keepdims=True)
        o = oc * lax.rsqrt(var + EPS) * gr[:, sl]
        g = proj_ref[rs, col(7, hh)]
        o = o * (g * jax.nn.sigmoid(g))
        ob_ref[rs, col(1, hh)] = o.astype(BF16)

    def chunk_body(i, carry):
        per_iter = ROW_TILE // n_iter
        for j in range(per_iter):
            put(i * per_iter + j)
        if gather:
            next_tile = jnp.minimum(step + 1, last_step)
            for j in range(per_iter):
                fetch(next_tile, 1 - slot, i * per_iter + j)
        heads = []
        for j in range(CHUNK_UNROLL):
            idx = i * CHUNK_UNROLL + j
            b = idx // n_chunks
            rs = pl.ds(pl.multiple_of(idx * CHUNK, CHUNK), CHUNK)
            t0 = pl.multiple_of((idx - b * n_chunks) * CHUNK, CHUNK)
            for hh in range(N_HEADS):
                heads.append(hgrn_head(b, rs, hh))
                heads.append(ret_head(b, rs, t0, hh))
        live = []
        while heads or live:
            if heads:
                live.append(heads.pop(0))
            nxt = []
            for h in live:
                try:
                    next(h)
                    nxt.append(h)
                except StopIteration:
                    pass
            live = nxt
        return carry

    lax.fori_loop(0, n_iter, chunk_body, 0)
    pltpu.make_async_copy(hbuf_ref, xs_ref.at[pl.ds(0, ROW_TILE), :], scsem).wait()

    if gather:
        @pl.when(step == last_step)
        def _():
            pltpu.make_async_copy(ys_ref.at[pl.ds(0, ROW_TILE), :], gbuf_ref.at[1 - slot], gsem.at[1 - slot]).wait()

    @pl.when(tt == pl.num_programs(1) - 1)
    def _():
        for b in range(bb_n):
            for hh in range(N_HEADS):
                sa_ref[b, hh] = sat_ref[b, hh].T

    out = _dot(ob_ref[...], wout_ref[...]).reshape(bb_n, tq, D_MODEL)
    x1 = x + g1_ref[...] * out
    x1_ref[...] = x1
    ms2 = jnp.mean(x1 * x1, axis=-1, keepdims=True)
    h2 = x1 * lax.rsqrt(ms2 + EPS) * nm_ref[...] * (1.0 + sc2_ref[...]) + sh2_ref[...]
    hbuf_ref[:, 0:D_MODEL] = h2.reshape(rows, D_MODEL)
    lt = _dot_nt(wrt_ref[...], h2.reshape(rows, D_MODEL).astype(BF16))
    route = _route(lt, br_ref[...])
    rec = jnp.concatenate([route, jnp.zeros((EXT - ROUTE_ROWS, rows), F32)], axis=0).T
    hbuf_ref[:, D_MODEL:D_MODEL + EXT] = rec

    combo = route[0:1, :]
    ids = lax.broadcasted_iota(jnp.int32, (COMBO_ROWS, ROW_TILE), 0).astype(F32)
    onehot = (ids == combo).astype(F32)
    cum = _dot(onehot.astype(BF16), triu_ref[...])
    total = dsp_ref[0][:, 0:1]
    cur = dsp_ref[1][:, 0:1]
    n_alloc = dsp_ref[2][:, 0:1]
    n_c = cum[:, ROW_TILE - 1:ROW_TILE]
    inv_blk = 1.0 / MOE_BLOCK
    before = jnp.floor((total + float(MOE_BLOCK - 1)) * inv_blk)
    after = jnp.floor((total + n_c + float(MOE_BLOCK - 1)) * inv_blk)
    new = after - before
    new_wide = jnp.broadcast_to(new, (COMBO_ROWS, 128))
    first_new = n_alloc + _dot(lstrict_ref[...], new_wide.astype(BF16))[:, 0:1]
    ordinal = total + cum - 1.0
    lblk = jnp.floor(ordinal * inv_blk)
    offs = ordinal - lblk * float(MOE_BLOCK)
    blk = jnp.where(lblk < before, cur, first_new + (lblk - before))
    pos = jnp.sum(onehot * (blk * float(MOE_BLOCK) + offs), axis=0, keepdims=True)
    pos_i = jnp.broadcast_to(pos, (ROUTE_ROWS, ROW_TILE)).astype(jnp.int32)
    pos3_ref[0] = pos_i
    posv_ref[...] = pos_i
    wide = (COMBO_ROWS, 128)
    dsp_ref[0] = jnp.broadcast_to(total + n_c, wide)
    dsp_ref[1] = jnp.broadcast_to(jnp.where(new > 0.0, first_new + new - 1.0, cur), wide)
    dsp_ref[2] = jnp.broadcast_to(n_alloc + jnp.sum(new, axis=0, keepdims=True), wide)
    blk_id = lax.broadcasted_iota(jnp.int32, (COMBO_ROWS, BLOCK_LANES), 1).astype(F32)
    pair_id = lax.broadcasted_iota(jnp.int32, (COMBO_ROWS, BLOCK_LANES), 0).astype(F32)
    fresh = (blk_id >= first_new) & (blk_id < first_new + new)
    owner = jnp.sum(jnp.where(fresh, pair_id, 0.0), axis=0, keepdims=True)
    bc_ref[...] += jnp.broadcast_to(owner, bc_ref.shape)
    to_smem = pltpu.make_async_copy(posv_ref.at[0], poss_ref, psem)
    to_smem.start()
    to_smem.wait()

    @pl.when(step == last_step)
    def _():
        def issue(g, carry):
            r0 = pl.multiple_of(g * ISSUE_UNROLL, ISSUE_UNROLL)
            for j in range(ISSUE_UNROLL):
                _row_copy(hbuf_ref, r0 + j, xs_ref, poss_ref[r0 + j], scsem).start()
            return carry
        lax.fori_loop(0, ROW_TILE // ISSUE_UNROLL, issue, 0)
        pltpu.make_async_copy(hbuf_ref, xs_ref.at[pl.ds(0, ROW_TILE), :], scsem).wait()


def _mixer(x, mods, consts, lw, s_a0, s_r0, offset, dispatch, prev=None):
    batch, seq, _ = x.shape
    bb_n, tq = _tiling(batch, seq)
    nb, nt = batch // bb_n, seq // tq
    cos, sin = _rope_tables(seq, offset)
    sh1, sc1, g1, sh2, sc2 = mods
    dsp, bc, xs = dispatch
    gather = prev is not None
    if gather:
        pos, g2p, ys = prev
    else:
        pos = jnp.zeros((1,), jnp.int32)
        g2p = jnp.zeros((batch, 1, D_MODEL), F32)
        ys = jnp.zeros((ROUTE_ROWS, D_MODEL), F32)

    def per_b(shape):
        return pl.BlockSpec(shape, lambda b, t, pos: (b,) + (0,) * (len(shape) - 1))

    def const(shape):
        return pl.BlockSpec(shape, lambda b, t, pos: (0,) * len(shape))

    any_spec = pl.BlockSpec(memory_space=pl.ANY)
    mod_spec = per_b((bb_n, 1, D_MODEL))
    state_spec = per_b((bb_n, N_HEADS, D_HEAD, D_HEAD))
    rope_spec = pl.BlockSpec((tq, D_HEAD), lambda b, t, pos: (t, 0))
    operands = [
        (x, pl.BlockSpec((bb_n, tq, D_MODEL), lambda b, t, pos: (b, t, 0))),
        (g2p, mod_spec), (ys, any_spec),
        (sh1, mod_spec), (sc1, mod_spec), (g1, mod_spec), (sh2, mod_spec), (sc2, mod_spec),
        (lw["na"], const((1, D_MODEL))), (lw["nm"], const((1, D_MODEL))),
        (lw["w_in"], const((D_MODEL, IN_COLS))), (lw["w_out"], const((2 * G_WIDTH, D_MODEL))),
        (lw["lb"], const((1, G_WIDTH))), (lw["ga"], const((1, G_WIDTH))), (lw["gr"], const((1, G_WIDTH))),
        (consts["wrt"], const((N_EXPERTS, D_MODEL))), (consts["br"], const((N_EXPERTS, 1))),
        (cos, rope_spec), (sin, rope_spec),
        (consts["intra"], const((N_HEADS, CHUNK, CHUNK))), (consts["inter"], const((N_HEADS, CHUNK, D_HEAD))),
        (consts["kdec"], const((N_HEADS, CHUNK, D_HEAD))), (consts["sdec"], const((N_HEADS, 1, D_HEAD))),
        (consts["tri"], const((CHUNK, 3 * CHUNK))), (consts["triu"], const((ROW_TILE, ROW_TILE))),
        (consts["lstrict"], const((COMBO_ROWS, COMBO_ROWS))),
        (s_a0, any_spec), (s_r0, any_spec),
        (dsp, const(dsp.shape)), (bc, const(bc.shape)), (xs, any_spec),
    ]
    out_specs = [
        pl.BlockSpec((bb_n, tq, D_MODEL), lambda b, t, pos: (b, t, 0)),
        pl.BlockSpec((1, ROUTE_ROWS, ROW_TILE), lambda b, t, pos: (b * nt + t, 0, 0)),
        state_spec, state_spec,
        const(dsp.shape), const(bc.shape), any_spec,
    ]
    out_shape = [
        jax.ShapeDtypeStruct((batch, seq, D_MODEL), F32),
        jax.ShapeDtypeStruct((nb * nt, ROUTE_ROWS, ROW_TILE), jnp.int32),
        jax.ShapeDtypeStruct((batch, N_HEADS, D_HEAD, D_HEAD), F32),
        jax.ShapeDtypeStruct((batch, N_HEADS, D_HEAD, D_HEAD), F32),
        jax.ShapeDtypeStruct(dsp.shape, F32),
        jax.ShapeDtypeStruct(bc.shape, F32),
        jax.ShapeDtypeStruct(xs.shape, F32),
    ]
    gather_rows = ROW_TILE if gather else ROUTE_ROWS
    x1, pos3, s_a, s_r, dsp, bc, xs = pl.pallas_call(
        functools.partial(_mixer_kernel, bb_n=bb_n, tq=tq, gather=gather, dummy_row=xs.shape[0] - ROW_TILE),
        grid_spec=pltpu.PrefetchScalarGridSpec(
            num_scalar_prefetch=1,
            grid=(nb, nt),
            in_specs=[spec for _, spec in operands],
            out_specs=out_specs,
            scratch_shapes=[
                pltpu.VMEM((ROW_TILE, IN_COLS), F32),
                pltpu.VMEM((ROW_TILE, 2 * G_WIDTH), BF16),
                pltpu.VMEM((bb_n, N_HEADS, D_HEAD, D_HEAD), F32),
                pltpu.VMEM((2, gather_rows, D_MODEL), F32),
                pltpu.VMEM((ROW_TILE, D_MODEL + EXT), F32),
                pltpu.VMEM((ROUTE_ROWS, ROW_TILE), jnp.int32),
                pltpu.SMEM((ROW_TILE,), jnp.int32),
                pltpu.SemaphoreType.DMA((2,)),
                pltpu.SemaphoreType.DMA((2,)),
                pltpu.SemaphoreType.DMA,
                pltpu.SemaphoreType.DMA,
            ],
        ),
        out_shape=out_shape,
        input_output_aliases={len(operands): 6},
        compiler_params=pltpu.CompilerParams(
            dimension_semantics=("arbitrary", "arbitrary"), vmem_limit_bytes=VMEM_LIMIT),
        name="mixer",
    )(pos, *[arr for arr, _ in operands])
    return x1, pos3[:, 0, :].reshape(batch * seq), s_a, s_r, (dsp, bc, xs)


def _rope_tables(seq, offset):
    half = D_HEAD // 2
    freq = ROPE_BASE ** (-jnp.arange(half, dtype=F32) / half)
    ang = (jnp.arange(seq, dtype=F32) + offset)[:, None] * freq[None, :]
    cos, sin = jnp.cos(ang), jnp.sin(ang)
    return jnp.concatenate([cos, cos], axis=-1), jnp.concatenate([-sin, sin], axis=-1)


def _retention_tables():
    log_g = jnp.log(1.0 - 2.0 ** (-5.0 - jnp.arange(N_HEADS, dtype=F32)))
    idx = jnp.arange(CHUNK, dtype=F32)
    rel = idx[:, None] - idx[None, :]
    intra = jnp.where(rel >= 0, jnp.exp(jnp.maximum(rel, 0.0)[None] * log_g[:, None, None]), 0.0)
    inter = jnp.exp((idx[None, :] + 1.0) * log_g[:, None])[..., None]
    kdec = jnp.exp((CHUNK - 1.0 - idx[None, :]) * log_g[:, None])[..., None]
    sdec = jnp.exp(CHUNK * log_g)[:, None, None]
    wide = (N_HEADS, CHUNK, D_HEAD)
    return (intra, jnp.broadcast_to(inter, wide), jnp.broadcast_to(kdec, wide),
            jnp.broadcast_to(sdec, (N_HEADS, 1, D_HEAD)))


def _row_copy(src_ref, src_row, dst_ref, dst_row, sem):
    return pltpu.make_async_copy(src_ref.at[pl.ds(src_row, 1), :], dst_ref.at[pl.ds(dst_row, 1), :], sem)


def _moe_kernel(order_ref, ea_ref, eb_ref, nu_ref, xs_ref, w1a, w3a, w2a, w1b, w3b, w2b, ys_ref):
    del order_ref, ea_ref, eb_ref
    j = pl.program_id(0)

    @pl.when(j < nu_ref[0])
    def _():
        x = xs_ref[:, 0:D_MODEL].astype(BF16)
        w_lo = xs_ref[:, D_MODEL + 1:D_MODEL + 2]
        w_hi = xs_ref[:, D_MODEL + 2:D_MODEL + 3]

        def ffn(w1, w3, w2):
            h1 = _dot(x, w1[0])
            hid = (h1 * jax.nn.sigmoid(h1)) * _dot(x, w3[0])
            return _dot(hid.astype(BF16), w2[0])

        ys_ref[...] = ffn(w1a, w3a, w2a) * w_lo + ffn(w1b, w3b, w2b) * w_hi

    @pl.when(j >= nu_ref[0])
    def _():
        ys_ref[...] = jnp.zeros_like(ys_ref)


def _moe(order, ea, eb, n_used, xs, w1, w3, w2):
    n_blocks = order.shape[0]
    width = xs.shape[1]

    def w_spec(which):
        if which == 0:
            return pl.BlockSpec((1, D_MODEL, D_MODEL), lambda j, order, ea, eb, nu: (ea[j], 0, 0))
        return pl.BlockSpec((1, D_MODEL, D_MODEL), lambda j, order, ea, eb, nu: (eb[j], 0, 0))

    return pl.pallas_call(
        _moe_kernel,
        grid_spec=pltpu.PrefetchScalarGridSpec(
            num_scalar_prefetch=4,
            grid=(n_blocks,),
            in_specs=[
                pl.BlockSpec((MOE_BLOCK, width), lambda j, order, ea, eb, nu: (order[j], 0)),
                w_spec(0), w_spec(0), w_spec(0), w_spec(1), w_spec(1), w_spec(1),
            ],
            out_specs=pl.BlockSpec((MOE_BLOCK, D_MODEL), lambda j, order, ea, eb, nu: (order[j], 0)),
        ),
        out_shape=jax.ShapeDtypeStruct((n_blocks * MOE_BLOCK, D_MODEL), F32),
        compiler_params=pltpu.CompilerParams(dimension_semantics=("arbitrary",), vmem_limit_bytes=VMEM_LIMIT),
        name="moe",
    )(order, ea, eb, n_used, xs, w1, w3, w2, w1, w3, w2)


def _combine_kernel(pos_ref, x1_ref, g2_ref, nf_ref, ys_ref, out_ref, buf_ref, sem):
    step = pl.program_id(0) * pl.num_programs(1) + pl.program_id(1)
    base = step * ROW_TILE

    def issue(g, carry):
        r0 = pl.multiple_of(g * ISSUE_UNROLL, ISSUE_UNROLL)
        for j in range(ISSUE_UNROLL):
            _row_copy(ys_ref, pos_ref[base + r0 + j], buf_ref, r0 + j, sem).start()
        return carry

    lax.fori_loop(0, ROW_TILE // ISSUE_UNROLL, issue, 0)
    pltpu.make_async_copy(ys_ref.at[pl.ds(0, ROW_TILE), :], buf_ref, sem).wait()
    x2 = x1_ref[...] + g2_ref[...] * buf_ref[...].reshape(x1_ref.shape)
    out_ref[...] = x2 * lax.rsqrt(jnp.mean(x2 * x2, axis=-1, keepdims=True) + EPS) * nf_ref[...]


def _combine(pos, x1, g2, nf, ys):
    batch, seq, _ = x1.shape
    bb_n, tq = _tiling(batch, seq)
    tok_spec = pl.BlockSpec((bb_n, tq, D_MODEL), lambda b, t, pos: (b, t, 0))
    return pl.pallas_call(
        _combine_kernel,
        grid_spec=pltpu.PrefetchScalarGridSpec(
            num_scalar_prefetch=1,
            grid=(batch // bb_n, seq // tq),
            in_specs=[
                tok_spec,
                pl.BlockSpec((bb_n, 1, D_MODEL), lambda b, t, pos: (b, 0, 0)),
                pl.BlockSpec((1, D_MODEL), lambda b, t, pos: (0, 0)),
                pl.BlockSpec(memory_space=pl.ANY),
            ],
            out_specs=tok_spec,
            scratch_shapes=[pltpu.VMEM((ROW_TILE, D_MODEL), F32), pltpu.SemaphoreType.DMA],
        ),
        out_shape=jax.ShapeDtypeStruct(x1.shape, F32),
        compiler_params=pltpu.CompilerParams(dimension_semantics=("arbitrary", "arbitrary")),
        name="combine",
    )(pos, x1, g2, nf, ys)


def _block_order(dsp, bc, n_blocks):
    n_used = dsp[2, 0, 0].astype(jnp.int32)
    blk = jnp.arange(n_blocks, dtype=jnp.int32)
    owner = jnp.where(blk < n_used, bc[0, :n_blocks].astype(jnp.int32), N_COMBO)
    order = jnp.argsort(owner, stable=True).astype(jnp.int32)
    combo = jnp.minimum(owner[order], N_COMBO - 1)
    last = combo[jnp.maximum(n_used - 1, 0)]
    combo = jnp.where(blk < n_used, combo, last)
    lo_tab = jnp.array([0, 0, 0, 1, 1, 2], jnp.int32)
    hi_tab = jnp.array([1, 2, 3, 2, 3, 3], jnp.int32)
    grp = combo // PAIRS_PER_GROUP
    pair = combo % PAIRS_PER_GROUP
    return order, grp * EXP_PER_GROUP + lo_tab[pair], grp * EXP_PER_GROUP + hi_tab[pair], n_used.reshape(1)


def kernel(x_prompt, x_sample, state_hgrn, state_ret, c_prompt, c_sample, w_in, w_out, lb_logits, hgrn_norm,
           ret_norm, norm_attn, norm_moe, w_ada, b_ada, w_router, b_router, w1, w3, w2, norm_f):
    bp, tp, _ = x_prompt.shape
    bs, ts, _ = x_sample.shape
    n_p, n_s = bp * tp, bs * ts
    n_tok = n_p + n_s
    assert n_tok % MOE_BLOCK == 0
    n_blocks = n_tok // MOE_BLOCK + N_COMBO
    assert n_blocks <= BLOCK_LANES

    lb_cum = jnp.cumsum(jax.nn.softmax(lb_logits.astype(F32), axis=0), axis=0)
    lb_all = lb_cum - lb_cum[0:1]

    mod = _ada(jnp.concatenate([c_prompt, c_sample], axis=0), w_ada, b_ada)

    intra, inter, kdec, sdec = _retention_tables()
    consts = {
        "wrt": w_router.T.astype(BF16), "br": b_router.astype(F32).reshape(N_EXPERTS, 1),
        "intra": intra, "inter": inter, "kdec": kdec, "sdec": sdec,
        "tri": jnp.tile(jnp.tril(jnp.ones((CHUNK, CHUNK), BF16)), (1, 3)),
        "triu": jnp.triu(jnp.ones((ROW_TILE, ROW_TILE), BF16)),
        "lstrict": jnp.tril(jnp.ones((COMBO_ROWS, COMBO_ROWS), BF16), k=-1),
    }

    xp, xs_ = x_prompt, x_sample
    sa_p = jnp.zeros((bp, N_HEADS, D_HEAD, D_HEAD), F32)
    sr_p = jnp.zeros((bp, N_HEADS, D_HEAD, D_HEAD), F32)
    new_states = []
    prev_p = prev_s = None
    for l in range(DEPTH):
        lw = {
            "na": norm_attn[l].reshape(1, D_MODEL), "nm": norm_moe[l].reshape(1, D_MODEL),
            "w_in": w_in[l].astype(BF16), "w_out": w_out[l].astype(BF16),
            "lb": lb_all[l].reshape(1, G_WIDTH), "ga": hgrn_norm[l].reshape(1, G_WIDTH),
            "gr": ret_norm[l].reshape(1, G_WIDTH),
        }
        parts = jnp.split(mod[l], 6, axis=-1)
        mods_p = [p[:bp, None, :] for p in parts]
        mods_s = [p[bp:, None, :] for p in parts]
        dsp0 = jnp.stack([jnp.zeros((COMBO_ROWS, 128), F32), jnp.full((COMBO_ROWS, 128), -1.0, F32),
                          jnp.zeros((COMBO_ROWS, 128), F32)])
        dispatch = (dsp0, jnp.zeros((ROUTE_ROWS, BLOCK_LANES), F32),
                    jnp.zeros((n_blocks * MOE_BLOCK + ROW_TILE, D_MODEL + EXT), F32))
        x1p, pos_p, sa_np, sr_np, dispatch = _mixer(xp, mods_p[:5], consts, lw, sa_p, sr_p, 0, dispatch, prev_p)
        x1s, pos_s, sa_ns, sr_ns, dispatch = _mixer(xs_, mods_s[:5], consts, lw, state_hgrn[l].astype(F32),
                                                    state_ret[l].astype(F32), PAST_LEN, dispatch, prev_s)
        new_states.append((sa_np, sr_np, sa_ns, sr_ns))

        dsp, bc, xs = dispatch
        order, ea, eb, n_used = _block_order(dsp, bc, n_blocks)
        ys = _moe(order, ea, eb, n_used, xs, w1[l].astype(BF16), w3[l].astype(BF16), w2[l].astype(BF16))

        xp, xs_ = x1p, x1s
        prev_p = (pos_p, mods_p[5], ys)
        prev_s = (pos_s, mods_s[5], ys)

    nf = norm_f.reshape(1, D_MODEL)
    xp = _combine(prev_p[0], xp, prev_p[1], nf, prev_p[2])
    xs_ = _combine(prev_s[0], xs_, prev_s[1], nf, prev_s[2])

    sa_prompt = jnp.stack([s[0] for s in new_states])
    sr_prompt = jnp.stack([s[1] for s in new_states])
    sa_sample = jnp.stack([s[2] for s in new_states])
    sr_sample = jnp.stack([s[3] for s in new_states])
    return (xp, xs_, sa_prompt, sr_prompt, sa_sample, sr_sample)
```

```python
import functools

import jax
import jax.numpy as jnp
from jax import lax
from jax.experimental import pallas as pl
from jax.experimental.pallas import tpu as pltpu

F32 = jnp.float32
BF16 = jnp.bfloat16

D_MODEL = 1024
DEPTH = 2
N_HEADS = 4
D_HEAD = 128
G_WIDTH = N_HEADS * D_HEAD
IN_COLS = 8 * G_WIDTH
CHUNK = 64
CHUNK_UNROLL = 2
STAGE_GAP = (None,) * 3
SUB = 8
LEVELS = (32, 16, 8)
N_EXPERTS = 16
EXP_PER_GROUP = 4
N_GROUPS = 4
PAIRS_PER_GROUP = 6
N_COMBO = N_GROUPS * PAIRS_PER_GROUP
COMBO_ROWS = 32
ROUTE_ROWS = 8
TOK_SUB = 8
PACK_SUB = 4
ROW_TILE = 512
ISSUE_UNROLL = 8
MOE_BLOCK = 256
BLOCK_LANES = 384
ADA_TILE = 1536
ROPE_BASE = 10000.0
PAST_LEN = 1024
EPS = 1e-6
LOG2E = 1.4426950408889634
VMEM_LIMIT = 56 * 1024 * 1024

_NT = (((1,), (1,)), ((), ()))
_TN = (((0,), (0,)), ((), ()))


def _dot(a, b):
    return jnp.dot(a, b, preferred_element_type=F32)


def _dot_nt(a, b):
    return lax.dot_general(a, b, _NT, preferred_element_type=F32)


def _dot_tn(a, b):
    return lax.dot_general(a, b, _TN, preferred_element_type=F32)


def _tiling(batch, seq):
    tq = min(seq, ROW_TILE)
    bb = ROW_TILE // tq
    assert tq % CHUNK == 0 and bb * tq == ROW_TILE and batch % bb == 0 and seq % tq == 0
    return bb, tq


def _ada_kernel(c_ref, w_ref, b_ref, o_ref):
    c = c_ref[...]
    cs = c * jax.nn.sigmoid(c)
    o_ref[0] = _dot(cs.astype(BF16), w_ref[0].astype(BF16)) + b_ref[0]


def _ada(c_all, w_ada, b_ada):
    m = c_all.shape[0]
    n = w_ada.shape[-1]
    return pl.pallas_call(
        _ada_kernel,
        grid=(DEPTH, n // ADA_TILE),
        in_specs=[
            pl.BlockSpec((m, D_MODEL), lambda l, j: (0, 0)),
            pl.BlockSpec((1, D_MODEL, ADA_TILE), lambda l, j: (l, 0, j)),
            pl.BlockSpec((1, 1, ADA_TILE), lambda l, j: (l, 0, j)),
        ],
        out_specs=pl.BlockSpec((1, m, ADA_TILE), lambda l, j: (l, 0, j)),
        out_shape=jax.ShapeDtypeStruct((DEPTH, m, n), F32),
        compiler_params=pltpu.CompilerParams(vmem_limit_bytes=VMEM_LIMIT),
        name="ada",
    )(c_all, w_ada, b_ada.reshape(DEPTH, 1, n))


def _route(lt, br):
    sc = jax.nn.sigmoid(lt)
    bi = sc + br
    s = [sc[e:e + 1] for e in range(N_EXPERTS)]
    v = [bi[e:e + 1] for e in range(N_EXPERTS)]
    gs = []
    for g in range(N_GROUPS):
        a = v[EXP_PER_GROUP * g:EXP_PER_GROUP * (g + 1)]
        m = a[0] + a[1]
        for i in range(EXP_PER_GROUP):
            for j in range(i + 1, EXP_PER_GROUP):
                if (i, j) != (0, 1):
                    m = jnp.maximum(m, a[i] + a[j])
        gs.append(m)
    best = gs[0]
    gsel = jnp.zeros_like(best)
    for g in range(1, N_GROUPS):
        upd = gs[g] > best
        best = jnp.where(upd, gs[g], best)
        gsel = jnp.where(upd, float(g), gsel)

    def pick(vals, key, n):
        out = vals[0]
        for g in range(1, n):
            out = jnp.where(key == float(g), vals[g], out)
        return out

    vv = [pick([v[EXP_PER_GROUP * g + j] for g in range(N_GROUPS)], gsel, N_GROUPS) for j in range(EXP_PER_GROUP)]
    ss = [pick([s[EXP_PER_GROUP * g + j] for g in range(N_GROUPS)], gsel, N_GROUPS) for j in range(EXP_PER_GROUP)]
    m1 = vv[0]
    i1 = jnp.zeros_like(m1)
    for j in range(1, EXP_PER_GROUP):
        upd = vv[j] > m1
        m1 = jnp.where(upd, vv[j], m1)
        i1 = jnp.where(upd, float(j), i1)
    m2 = jnp.full_like(m1, -jnp.inf)
    i2 = jnp.zeros_like(m1)
    for j in range(EXP_PER_GROUP):
        upd = (i1 != float(j)) & (vv[j] > m2)
        m2 = jnp.where(upd, vv[j], m2)
        i2 = jnp.where(upd, float(j), i2)
    s1 = pick(ss, i1, EXP_PER_GROUP)
    s2 = pick(ss, i2, EXP_PER_GROUP)
    den = s1 + s2
    w1 = s1 / den
    w2 = s2 / den
    first_lo = i1 < i2
    lo = jnp.where(first_lo, i1, i2)
    hi = jnp.where(first_lo, i2, i1)
    w_lo = jnp.where(first_lo, w1, w2)
    w_hi = jnp.where(first_lo, w2, w1)
    pair = jnp.where(lo == 0.0, hi - 1.0, jnp.where(lo == 1.0, hi + 1.0, 5.0))
    combo = gsel * float(PAIRS_PER_GROUP) + pair
    zero = jnp.zeros_like(combo)
    return jnp.concatenate([combo, w_lo, w_hi] + [zero] * (ROUTE_ROWS - 3), axis=0)


def _mixer_kernel(pos_ref, x_ref, g2p_ref, ys_ref, sh1_ref, sc1_ref, g1_ref, sh2_ref, sc2_ref, na_ref, nm_ref,
                  win_ref, wout_ref, lb_ref, ga_ref, gr_ref, wrt_ref, br_ref, cos_ref, sin_ref, intra_ref,
                  inter_ref, kdec_ref, sdec_ref, tri_ref, triu_ref, lstrict_ref, sa0_ref, sr0_ref,
                  dsp0_ref, bc0_ref, xs_in_ref,
                  x1_ref, pos3_ref, sa_ref, sr_ref, dsp_ref, bc_ref, xs_ref,
                  proj_ref, ob_ref, sat_ref, gbuf_ref, hbuf_ref, posv_ref, poss_ref, gsem, ssem, psem, scsem,
                  *, bb_n, tq, gather, dummy_row):
    del xs_in_ref
    tt = pl.program_id(1)
    rows = bb_n * tq
    n_chunks = tq // CHUNK
    n_iter = bb_n * n_chunks // CHUNK_UNROLL
    step = pl.program_id(0) * pl.num_programs(1) + tt
    last_step = pl.num_programs(0) * pl.num_programs(1) - 1

    x = x_ref[...]
    if gather:
        slot = step % 2

        def fetch(tile, dst_slot, r):
            _tok_copy(ys_ref, pos_ref[tile * ROW_TILE + r], gbuf_ref.at[dst_slot], r, gsem.at[dst_slot]).start()

        @pl.when(step == 0)
        def _():
            def issue(g, carry):
                r0 = pl.multiple_of(g * ISSUE_UNROLL, ISSUE_UNROLL)
                for j in range(ISSUE_UNROLL):
                    fetch(0, 0, r0 + j)
                return carry
            lax.fori_loop(0, ROW_TILE // ISSUE_UNROLL, issue, 0)

        pltpu.make_async_copy(ys_ref.at[pl.ds(0, ROW_TILE * TOK_SUB), :], gbuf_ref.at[slot], gsem.at[slot]).wait()
        y_prev = _tok_columns(gbuf_ref, ROW_TILE, TOK_SUB, (slot,))
        x = x + g2p_ref[...] * y_prev.reshape(bb_n, tq, D_MODEL)
    ms = jnp.mean(x * x, axis=-1, keepdims=True)
    h = x * lax.rsqrt(ms + EPS) * na_ref[...] * (1.0 + sc1_ref[...]) + sh1_ref[...]
    proj_ref[...] = _dot(h.reshape(rows, D_MODEL).astype(BF16), win_ref[...])

    @pl.when(tt == 0)
    def _():
        b0 = pl.program_id(0) * bb_n
        load_a = pltpu.make_async_copy(sa0_ref.at[pl.ds(b0, bb_n)], sa_ref, ssem.at[0])
        load_r = pltpu.make_async_copy(sr0_ref.at[pl.ds(b0, bb_n)], sr_ref, ssem.at[1])
        load_a.start()
        load_r.start()
        load_a.wait()
        load_r.wait()
        for b in range(bb_n):
            for hh in range(N_HEADS):
                sat_ref[b, hh] = sa_ref[b, hh].T

    @pl.when(step == 0)
    def _():
        dsp_ref[...] = dsp0_ref[...]
        bc_ref[...] = bc0_ref[...]
        hbuf_ref[...] = jnp.zeros_like(hbuf_ref)
        posv_ref[...] = jnp.zeros_like(posv_ref)
        init = pltpu.make_async_copy(posv_ref.at[0], poss_ref, psem)
        init.start()
        init.wait()

    def put(r):
        dst = jnp.where(step > 0, poss_ref[r], dummy_row + r)
        _tok_copy(hbuf_ref, r, xs_ref, dst, scsem).start()

    lb = lb_ref[...]
    ln_1mlb = jnp.log1p(-lb)
    ga = ga_ref[...]
    gr = gr_ref[...]
    tri3 = tri_ref[...]
    t_i = lax.broadcasted_iota(jnp.int32, (CHUNK, CHUNK), 0)
    s_i = lax.broadcasted_iota(jnp.int32, (CHUNK, CHUNK), 1)
    level_masks = []
    for m in LEVELS:
        sh = m.bit_length() - 1
        level_masks.append(((t_i >> (sh + 1)) == (s_i >> (sh + 1)))
                           & (((t_i >> sh) & 1) == 1) & (((s_i >> sh) & 1) == 0))
    diag_mask = ((t_i >> 3) == (s_i >> 3)) & (s_i <= t_i)
    col_in_sub = s_i & (SUB - 1)

    def col(group, hh):
        return slice(group * G_WIDTH + hh * D_HEAD, group * G_WIDTH + (hh + 1) * D_HEAD)

    def hgrn_head(b, rs, hh):
        sl = slice(hh * D_HEAD, (hh + 1) * D_HEAD)
        z = proj_ref[rs, col(1, hh)]
        lbh = lb[:, sl]
        u = jnp.exp(-jnp.abs(z))
        ln_1pu = jnp.log(1.0 + u)
        num = jnp.where(z >= 0.0, 1.0 + lbh * u, lbh + u)
        l2f = (jnp.where(num > 0.0, jnp.log(num), z) - ln_1pu) * LOG2E
        l2k = (ln_1mlb[:, sl] - jnp.maximum(z, 0.0) - ln_1pu) * LOG2E
        hi = l2f.astype(BF16)
        rem = l2f - hi.astype(F32)
        mid = rem.astype(BF16)
        lo = (rem - mid.astype(F32)).astype(BF16)
        bcum = _dot(tri3, jnp.concatenate([hi, mid, lo], axis=0))
        yield from STAGE_GAP
        q = proj_ref[rs, col(0, hh)]
        vb = proj_ref[rs, col(2, hh)].astype(BF16)
        b_last = bcum[CHUNK - 1:CHUNK, :]
        q_in = (q * jnp.exp2(bcum)).astype(BF16)
        k_out = jnp.exp2(l2k + (b_last - bcum)).astype(BF16)
        upd = _dot_tn(vb, k_out)
        lvl = []
        for m in LEVELS:
            ref_row = bcum.reshape(CHUNK // (2 * m), 2 * m, D_HEAD)[:, m - 1:m, :]
            ref_row = jnp.broadcast_to(ref_row, (CHUNK // (2 * m), 2 * m, D_HEAD)).reshape(CHUNK, D_HEAD)
            dist = jnp.abs(bcum - ref_row)
            lvl.append(_dot_nt((q * jnp.exp2(-dist)).astype(BF16), jnp.exp2(l2k - dist).astype(BF16)))
        src3 = (l2k - bcum).reshape(CHUNK // SUB, SUB, D_HEAD)
        diag = jnp.zeros((CHUNK, CHUNK), F32)
        for sg in range(SUB):
            src = jnp.broadcast_to(src3[:, sg:sg + 1, :], src3.shape).reshape(CHUNK, D_HEAD)
            row_sum = jnp.sum(q * jnp.exp2(jnp.minimum(bcum + src, 0.0)), axis=-1, keepdims=True)
            diag = jnp.where(col_in_sub == sg, row_sum, diag)
        yield from STAGE_GAP
        a = jnp.where(diag_mask, diag, 0.0)
        for li in range(len(LEVELS)):
            a = jnp.where(level_masks[li], lvl[li], a)
        st = sat_ref[b, hh]
        o = _dot(a.astype(BF16), vb) + _dot_nt(q_in, st.astype(BF16))
        sat_ref[b, hh] = st * jnp.exp2(b_last) + upd
        yield from STAGE_GAP
        o = o * lax.rsqrt(jnp.mean(o * o, axis=-1, keepdims=True) + EPS)
        o = o * ga[:, sl] * jax.nn.sigmoid(proj_ref[rs, col(3, hh)])
        ob_ref[rs, sl] = o.astype(BF16)

    def ret_head(b, rs, t0, hh):
        sl = slice(hh * D_HEAD, (hh + 1) * D_HEAD)
        cs = cos_ref[pl.ds(t0, CHUNK), :]
        sn = sin_ref[pl.ds(t0, CHUNK), :]
        qh = proj_ref[rs, col(4, hh)]
        kh = proj_ref[rs, col(5, hh)]
        qr = qh * cs + pltpu.roll(qh, D_HEAD // 2, 1) * sn
        kr = (kh * cs + pltpu.roll(kh, D_HEAD // 2, 1) * sn) * (D_HEAD ** -0.5)
        qb = qr.astype(BF16)
        vh = proj_ref[rs, col(6, hh)].astype(BF16)
        sc = _dot_nt(qb, kr.astype(BF16))
        upd = _dot_tn((kr * kdec_ref[hh]).astype(BF16), vh)
        yield from STAGE_GAP
        s_r = sr_ref[b, hh]
        o = _dot((sc * intra_ref[hh]).astype(BF16), vh) + _dot(qb, s_r.astype(BF16)) * inter_ref[hh]
        sr_ref[b, hh] = s_r * sdec_ref[hh] + upd
        yield from STAGE_GAP
        mu = jnp.mean(o, axis=-1, keepdims=True)
        oc = o - mu
        var = jnp.mean(oc * oc, axis=-1, keepdims=True)
        o = oc * lax.rsqrt(var + EPS) * gr[:, sl]
        g = proj_ref[rs, col(7, hh)]
        o = o * (g * jax.nn.sigmoid(g))
        ob_ref[rs, col(1, hh)] = o.astype(BF16)

    def chunk_body(i, carry):
        per_iter = ROW_TILE // n_iter
        for j in range(per_iter):
            put(i * per_iter + j)
        if gather:
            next_tile = jnp.minimum(step + 1, last_step)
            for j in range(per_iter):
                fetch(next_tile, 1 - slot, i * per_iter + j)
        heads = []
        for j in range(CHUNK_UNROLL):
            idx = i * CHUNK_UNROLL + j
            b = idx // n_chunks
            rs = pl.ds(pl.multiple_of(idx * CHUNK, CHUNK), CHUNK)
            t0 = pl.multiple_of((idx - b * n_chunks) * CHUNK, CHUNK)
            for hh in range(N_HEADS):
                heads.append(hgrn_head(b, rs, hh))
                heads.append(ret_head(b, rs, t0, hh))
        live = []
        while heads or live:
            if heads:
                live.append(heads.pop(0))
            nxt = []
            for h in live:
                try:
                    next(h)
                    nxt.append(h)
                except StopIteration:
                    pass
            live = nxt
        return carry

    lax.fori_loop(0, n_iter, chunk_body, 0)
    pltpu.make_async_copy(hbuf_ref, xs_ref.at[pl.ds(0, ROW_TILE * TOK_SUB), :], scsem).wait()

    if gather:
        @pl.when(step == last_step)
        def _():
            pltpu.make_async_copy(ys_ref.at[pl.ds(0, ROW_TILE * TOK_SUB), :], gbuf_ref.at[1 - slot],
                                  gsem.at[1 - slot]).wait()

    @pl.when(tt == pl.num_programs(1) - 1)
    def _():
        for b in range(bb_n):
            for hh in range(N_HEADS):
                sa_ref[b, hh] = sat_ref[b, hh].T

    out = _dot(ob_ref[...], wout_ref[...]).reshape(bb_n, tq, D_MODEL)
    x1 = x + g1_ref[...] * out
    x1_ref[...] = x1
    ms2 = jnp.mean(x1 * x1, axis=-1, keepdims=True)
    h2 = x1 * lax.rsqrt(ms2 + EPS) * nm_ref[...] * (1.0 + sc2_ref[...]) + sh2_ref[...]
    h2b = h2.reshape(rows, D_MODEL).astype(BF16)
    lt = _dot_nt(wrt_ref[...], h2b)
    route = _route(lt, br_ref[...])
    bits = lax.bitcast_convert_type(h2b.astype(F32), jnp.uint32)
    words = (bits[:, 0:D_MODEL // 2] >> 16) | (bits[:, D_MODEL // 2:D_MODEL] & jnp.uint32(0xFFFF0000))
    for s in range(PACK_SUB):
        hbuf_ref[pl.ds(s, ROW_TILE, stride=TOK_SUB), :] = words[:, s * 128:(s + 1) * 128]
    rec = jnp.concatenate([route, jnp.zeros((128 - ROUTE_ROWS, rows), F32)], axis=0).T
    hbuf_ref[pl.ds(PACK_SUB, ROW_TILE, stride=TOK_SUB), :] = lax.bitcast_convert_type(rec, jnp.uint32)

    combo = route[0:1, :]
    ids = lax.broadcasted_iota(jnp.int32, (COMBO_ROWS, ROW_TILE), 0).astype(F32)
    onehot = (ids == combo).astype(F32)
    cum = _dot(onehot.astype(BF16), triu_ref[...])
    total = dsp_ref[0][:, 0:1]
    cur = dsp_ref[1][:, 0:1]
    n_alloc = dsp_ref[2][:, 0:1]
    n_c = cum[:, ROW_TILE - 1:ROW_TILE]
    inv_blk = 1.0 / MOE_BLOCK
    before = jnp.floor((total + float(MOE_BLOCK - 1)) * inv_blk)
    after = jnp.floor((total + n_c + float(MOE_BLOCK - 1)) * inv_blk)
    new = after - before
    new_wide = jnp.broadcast_to(new, (COMBO_ROWS, 128))
    first_new = n_alloc + _dot(lstrict_ref[...], new_wide.astype(BF16))[:, 0:1]
    ordinal = total + cum - 1.0
    lblk = jnp.floor(ordinal * inv_blk)
    offs = ordinal - lblk * float(MOE_BLOCK)
    blk = jnp.where(lblk < before, cur, first_new + (lblk - before))
    pos = jnp.sum(onehot * (blk * float(MOE_BLOCK) + offs), axis=0, keepdims=True)
    pos_i = jnp.broadcast_to(pos, (ROUTE_ROWS, ROW_TILE)).astype(jnp.int32)
    pos3_ref[0] = pos_i
    posv_ref[...] = pos_i
    wide = (COMBO_ROWS, 128)
    dsp_ref[0] = jnp.broadcast_to(total + n_c, wide)
    dsp_ref[1] = jnp.broadcast_to(jnp.where(new > 0.0, first_new + new - 1.0, cur), wide)
    dsp_ref[2] = jnp.broadcast_to(n_alloc + jnp.sum(new, axis=0, keepdims=True), wide)
    blk_id = lax.broadcasted_iota(jnp.int32, (COMBO_ROWS, BLOCK_LANES), 1).astype(F32)
    pair_id = lax.broadcasted_iota(jnp.int32, (COMBO_ROWS, BLOCK_LANES), 0).astype(F32)
    fresh = (blk_id >= first_new) & (blk_id < first_new + new)
    owner = jnp.sum(jnp.where(fresh, pair_id, 0.0), axis=0, keepdims=True)
    bc_ref[...] += jnp.broadcast_to(owner, bc_ref.shape)
    to_smem = pltpu.make_async_copy(posv_ref.at[0], poss_ref, psem)
    to_smem.start()
    to_smem.wait()

    @pl.when(step == last_step)
    def _():
        def issue(g, carry):
            r0 = pl.multiple_of(g * ISSUE_UNROLL, ISSUE_UNROLL)
            for j in range(ISSUE_UNROLL):
                _tok_copy(hbuf_ref, r0 + j, xs_ref, poss_ref[r0 + j], scsem).start()
            return carry
        lax.fori_loop(0, ROW_TILE // ISSUE_UNROLL, issue, 0)
        pltpu.make_async_copy(hbuf_ref, xs_ref.at[pl.ds(0, ROW_TILE * TOK_SUB), :], scsem).wait()


def _mixer(x, mods, consts, lw, s_a0, s_r0, offset, dispatch, prev=None):
    batch, seq, _ = x.shape
    bb_n, tq = _tiling(batch, seq)
    nb, nt = batch // bb_n, seq // tq
    cos, sin = _rope_tables(seq, offset)
    sh1, sc1, g1, sh2, sc2 = mods
    dsp, bc, xs = dispatch
    gather = prev is not None
    if gather:
        pos, g2p, ys = prev
    else:
        pos = jnp.zeros((1,), jnp.int32)
        g2p = jnp.zeros((batch, 1, D_MODEL), F32)
        ys = jnp.zeros((ROUTE_ROWS * TOK_SUB, 128), F32)

    def per_b(shape):
        return pl.BlockSpec(shape, lambda b, t, pos: (b,) + (0,) * (len(shape) - 1))

    def const(shape):
        return pl.BlockSpec(shape, lambda b, t, pos: (0,) * len(shape))

    any_spec = pl.BlockSpec(memory_space=pl.ANY)
    mod_spec = per_b((bb_n, 1, D_MODEL))
    state_spec = per_b((bb_n, N_HEADS, D_HEAD, D_HEAD))
    rope_spec = pl.BlockSpec((tq, D_HEAD), lambda b, t, pos: (t, 0))
    operands = [
        (x, pl.BlockSpec((bb_n, tq, D_MODEL), lambda b, t, pos: (b, t, 0))),
        (g2p, mod_spec), (ys, any_spec),
        (sh1, mod_spec), (sc1, mod_spec), (g1, mod_spec), (sh2, mod_spec), (sc2, mod_spec),
        (lw["na"], const((1, D_MODEL))), (lw["nm"], const((1, D_MODEL))),
        (lw["w_in"], const((D_MODEL, IN_COLS))), (lw["w_out"], const((2 * G_WIDTH, D_MODEL))),
        (lw["lb"], const((1, G_WIDTH))), (lw["ga"], const((1, G_WIDTH))), (lw["gr"], const((1, G_WIDTH))),
        (consts["wrt"], const((N_EXPERTS, D_MODEL))), (consts["br"], const((N_EXPERTS, 1))),
        (cos, rope_spec), (sin, rope_spec),
        (consts["intra"], const((N_HEADS, CHUNK, CHUNK))), (consts["inter"], const((N_HEADS, CHUNK, D_HEAD))),
        (consts["kdec"], const((N_HEADS, CHUNK, D_HEAD))), (consts["sdec"], const((N_HEADS, 1, D_HEAD))),
        (consts["tri"], const((CHUNK, 3 * CHUNK))), (consts["triu"], const((ROW_TILE, ROW_TILE))),
        (consts["lstrict"], const((COMBO_ROWS, COMBO_ROWS))),
        (s_a0, any_spec), (s_r0, any_spec),
        (dsp, const(dsp.shape)), (bc, const(bc.shape)), (xs, any_spec),
    ]
    out_specs = [
        pl.BlockSpec((bb_n, tq, D_MODEL), lambda b, t, pos: (b, t, 0)),
        pl.BlockSpec((1, ROUTE_ROWS, ROW_TILE), lambda b, t, pos: (b * nt + t, 0, 0)),
        state_spec, state_spec,
        const(dsp.shape), const(bc.shape), any_spec,
    ]
    out_shape = [
        jax.ShapeDtypeStruct((batch, seq, D_MODEL), F32),
        jax.ShapeDtypeStruct((nb * nt, ROUTE_ROWS, ROW_TILE), jnp.int32),
        jax.ShapeDtypeStruct((batch, N_HEADS, D_HEAD, D_HEAD), F32),
        jax.ShapeDtypeStruct((batch, N_HEADS, D_HEAD, D_HEAD), F32),
        jax.ShapeDtypeStruct(dsp.shape, F32),
        jax.ShapeDtypeStruct(bc.shape, F32),
        jax.ShapeDtypeStruct(xs.shape, xs.dtype),
    ]
    gather_rows = ROW_TILE if gather else ROUTE_ROWS
    x1, pos3, s_a, s_r, dsp, bc, xs = pl.pallas_call(
        functools.partial(_mixer_kernel, bb_n=bb_n, tq=tq, gather=gather, dummy_row=xs.shape[0] // TOK_SUB - ROW_TILE),
        grid_spec=pltpu.PrefetchScalarGridSpec(
            num_scalar_prefetch=1,
            grid=(nb, nt),
            in_specs=[spec for _, spec in operands],
            out_specs=out_specs,
            scratch_shapes=[
                pltpu.VMEM((ROW_TILE, IN_COLS), F32),
                pltpu.VMEM((ROW_TILE, 2 * G_WIDTH), BF16),
                pltpu.VMEM((bb_n, N_HEADS, D_HEAD, D_HEAD), F32),
                pltpu.VMEM((2, gather_rows * TOK_SUB, 128), F32),
                pltpu.VMEM((ROW_TILE * TOK_SUB, 128), jnp.uint32),
                pltpu.VMEM((ROUTE_ROWS, ROW_TILE), jnp.int32),
                pltpu.SMEM((ROW_TILE,), jnp.int32),
                pltpu.SemaphoreType.DMA((2,)),
                pltpu.SemaphoreType.DMA((2,)),
                pltpu.SemaphoreType.DMA,
                pltpu.SemaphoreType.DMA,
            ],
        ),
        out_shape=out_shape,
        input_output_aliases={len(operands): 6},
        compiler_params=pltpu.CompilerParams(
            dimension_semantics=("arbitrary", "arbitrary"), vmem_limit_bytes=VMEM_LIMIT),
        name="mixer",
    )(pos, *[arr for arr, _ in operands])
    return x1, pos3[:, 0, :].reshape(batch * seq), s_a, s_r, (dsp, bc, xs)


def _rope_tables(seq, offset):
    half = D_HEAD // 2
    freq = ROPE_BASE ** (-jnp.arange(half, dtype=F32) / half)
    ang = (jnp.arange(seq, dtype=F32) + offset)[:, None] * freq[None, :]
    cos, sin = jnp.cos(ang), jnp.sin(ang)
    return jnp.concatenate([cos, cos], axis=-1), jnp.concatenate([-sin, sin], axis=-1)


def _retention_tables():
    log_g = jnp.log(1.0 - 2.0 ** (-5.0 - jnp.arange(N_HEADS, dtype=F32)))
    idx = jnp.arange(CHUNK, dtype=F32)
    rel = idx[:, None] - idx[None, :]
    intra = jnp.where(rel >= 0, jnp.exp(jnp.maximum(rel, 0.0)[None] * log_g[:, None, None]), 0.0)
    inter = jnp.exp((idx[None, :] + 1.0) * log_g[:, None])[..., None]
    kdec = jnp.exp((CHUNK - 1.0 - idx[None, :]) * log_g[:, None])[..., None]
    sdec = jnp.exp(CHUNK * log_g)[:, None, None]
    wide = (N_HEADS, CHUNK, D_HEAD)
    return (intra, jnp.broadcast_to(inter, wide), jnp.broadcast_to(kdec, wide),
            jnp.broadcast_to(sdec, (N_HEADS, 1, D_HEAD)))


def _tok_copy(src_ref, src_tok, dst_ref, dst_tok, sem):
    src = src_ref.at[pl.ds(pl.multiple_of(src_tok * TOK_SUB, TOK_SUB), TOK_SUB), :]
    dst = dst_ref.at[pl.ds(pl.multiple_of(dst_tok * TOK_SUB, TOK_SUB), TOK_SUB), :]
    return pltpu.make_async_copy(src, dst, sem)


def _tok_columns(ref, n_tok, n_sub, lead=()):
    return jnp.concatenate([ref[lead + (pl.ds(s, n_tok, stride=TOK_SUB), slice(None))] for s in range(n_sub)], axis=1)


def _moe_kernel(order_ref, ea_ref, eb_ref, nu_ref, xs_ref, w1a, w3a, w2a, w1b, w3b, w2b, ys_ref):
    del order_ref, ea_ref, eb_ref
    j = pl.program_id(0)

    @pl.when(j < nu_ref[0])
    def _():
        words = _tok_columns(xs_ref, MOE_BLOCK, PACK_SUB)
        lo = lax.bitcast_convert_type(words << 16, F32)
        hi = lax.bitcast_convert_type(words & jnp.uint32(0xFFFF0000), F32)
        x = jnp.concatenate([lo, hi], axis=1).astype(BF16)
        rec = lax.bitcast_convert_type(xs_ref[pl.ds(PACK_SUB, MOE_BLOCK, stride=TOK_SUB), :], F32)
        w_lo = rec[:, 1:2]
        w_hi = rec[:, 2:3]

        def ffn(w1, w3, w2):
            h1 = _dot(x, w1[0])
            hid = (h1 * jax.nn.sigmoid(h1)) * _dot(x, w3[0])
            return _dot(hid.astype(BF16), w2[0])

        y = ffn(w1a, w3a, w2a) * w_lo + ffn(w1b, w3b, w2b) * w_hi
        for s in range(TOK_SUB):
            ys_ref[pl.ds(s, MOE_BLOCK, stride=TOK_SUB), :] = y[:, s * 128:(s + 1) * 128]

    @pl.when(j >= nu_ref[0])
    def _():
        ys_ref[...] = jnp.zeros_like(ys_ref)


def _moe(order, ea, eb, n_used, xs, w1, w3, w2):
    n_blocks = order.shape[0]
    blk_rows = MOE_BLOCK * TOK_SUB

    def w_spec(which):
        if which == 0:
            return pl.BlockSpec((1, D_MODEL, D_MODEL), lambda j, order, ea, eb, nu: (ea[j], 0, 0))
        return pl.BlockSpec((1, D_MODEL, D_MODEL), lambda j, order, ea, eb, nu: (eb[j], 0, 0))

    return pl.pallas_call(
        _moe_kernel,
        grid_spec=pltpu.PrefetchScalarGridSpec(
            num_scalar_prefetch=4,
            grid=(n_blocks,),
            in_specs=[
                pl.BlockSpec((blk_rows, 128), lambda j, order, ea, eb, nu: (order[j], 0)),
                w_spec(0), w_spec(0), w_spec(0), w_spec(1), w_spec(1), w_spec(1),
            ],
            out_specs=pl.BlockSpec((blk_rows, 128), lambda j, order, ea, eb, nu: (order[j], 0)),
        ),
        out_shape=jax.ShapeDtypeStruct((n_blocks * blk_rows, 128), F32),
        compiler_params=pltpu.CompilerParams(dimension_semantics=("arbitrary",), vmem_limit_bytes=VMEM_LIMIT),
        name="moe",
    )(order, ea, eb, n_used, xs, w1, w3, w2, w1, w3, w2)


def _combine_kernel(pos_ref, x1_ref, g2_ref, nf_ref, ys_ref, out_ref, buf_ref, sem):
    step = pl.program_id(0) * pl.num_programs(1) + pl.program_id(1)
    base = step * ROW_TILE

    def issue(g, carry):
        r0 = pl.multiple_of(g * ISSUE_UNROLL, ISSUE_UNROLL)
        for j in range(ISSUE_UNROLL):
            _tok_copy(ys_ref, pos_ref[base + r0 + j], buf_ref, r0 + j, sem).start()
        return carry

    lax.fori_loop(0, ROW_TILE // ISSUE_UNROLL, issue, 0)
    pltpu.make_async_copy(ys_ref.at[pl.ds(0, ROW_TILE * TOK_SUB), :], buf_ref, sem).wait()
    x2 = x1_ref[...] + g2_ref[...] * _tok_columns(buf_ref, ROW_TILE, TOK_SUB).reshape(x1_ref.shape)
    out_ref[...] = x2 * lax.rsqrt(jnp.mean(x2 * x2, axis=-1, keepdims=True) + EPS) * nf_ref[...]


def _combine(pos, x1, g2, nf, ys):
    batch, seq, _ = x1.shape
    bb_n, tq = _tiling(batch, seq)
    tok_spec = pl.BlockSpec((bb_n, tq, D_MODEL), lambda b, t, pos: (b, t, 0))
    return pl.pallas_call(
        _combine_kernel,
        grid_spec=pltpu.PrefetchScalarGridSpec(
            num_scalar_prefetch=1,
            grid=(batch // bb_n, seq // tq),
            in_specs=[
                tok_spec,
                pl.BlockSpec((bb_n, 1, D_MODEL), lambda b, t, pos: (b, 0, 0)),
                pl.BlockSpec((1, D_MODEL), lambda b, t, pos: (0, 0)),
                pl.BlockSpec(memory_space=pl.ANY),
            ],
            out_specs=tok_spec,
            scratch_shapes=[pltpu.VMEM((ROW_TILE * TOK_SUB, 128), F32), pltpu.SemaphoreType.DMA],
        ),
        out_shape=jax.ShapeDtypeStruct(x1.shape, F32),
        compiler_params=pltpu.CompilerParams(dimension_semantics=("arbitrary", "arbitrary")),
        name="combine",
    )(pos, x1, g2, nf, ys)


def _block_order(dsp, bc, n_blocks):
    n_used = dsp[2, 0, 0].astype(jnp.int32)
    blk = jnp.arange(n_blocks, dtype=jnp.int32)
    owner = jnp.where(blk < n_used, bc[0, :n_blocks].astype(jnp.int32), N_COMBO)
    order = jnp.argsort(owner, stable=True).astype(jnp.int32)
    combo = jnp.minimum(owner[order], N_COMBO - 1)
    last = combo[jnp.maximum(n_used - 1, 0)]
    combo = jnp.where(blk < n_used, combo, last)
    lo_tab = jnp.array([0, 0, 0, 1, 1, 2], jnp.int32)
    hi_tab = jnp.array([1, 2, 3, 2, 3, 3], jnp.int32)
    grp = combo // PAIRS_PER_GROUP
    pair = combo % PAIRS_PER_GROUP
    return order, grp * EXP_PER_GROUP + lo_tab[pair], grp * EXP_PER_GROUP + hi_tab[pair], n_used.reshape(1)


def kernel(x_prompt, x_sample, state_hgrn, state_ret, c_prompt, c_sample, w_in, w_out, lb_logits, hgrn_norm,
           ret_norm, norm_attn, norm_moe, w_ada, b_ada, w_router, b_router, w1, w3, w2, norm_f):
    bp, tp, _ = x_prompt.shape
    bs, ts, _ = x_sample.shape
    n_p, n_s = bp * tp, bs * ts
    n_tok = n_p + n_s
    assert n_tok % MOE_BLOCK == 0
    n_blocks = n_tok // MOE_BLOCK + N_COMBO
    assert n_blocks <= BLOCK_LANES

    lb_cum = jnp.cumsum(jax.nn.softmax(lb_logits.astype(F32), axis=0), axis=0)
    lb_all = lb_cum - lb_cum[0:1]

    mod = _ada(jnp.concatenate([c_prompt, c_sample], axis=0), w_ada, b_ada)

    intra, inter, kdec, sdec = _retention_tables()
    consts = {
        "wrt": w_router.T.astype(BF16), "br": b_router.astype(F32).reshape(N_EXPERTS, 1),
        "intra": intra, "inter": inter, "kdec": kdec, "sdec": sdec,
        "tri": jnp.tile(jnp.tril(jnp.ones((CHUNK, CHUNK), BF16)), (1, 3)),
        "triu": jnp.triu(jnp.ones((ROW_TILE, ROW_TILE), BF16)),
        "lstrict": jnp.tril(jnp.ones((COMBO_ROWS, COMBO_ROWS), BF16), k=-1),
    }

    xp, xs_ = x_prompt, x_sample
    sa_p = jnp.zeros((bp, N_HEADS, D_HEAD, D_HEAD), F32)
    sr_p = jnp.zeros((bp, N_HEADS, D_HEAD, D_HEAD), F32)
    new_states = []
    prev_p = prev_s = None
    for l in range(DEPTH):
        lw = {
            "na": norm_attn[l].reshape(1, D_MODEL), "nm": norm_moe[l].reshape(1, D_MODEL),
            "w_in": w_in[l].astype(BF16), "w_out": w_out[l].astype(BF16),
            "lb": lb_all[l].reshape(1, G_WIDTH), "ga": hgrn_norm[l].reshape(1, G_WIDTH),
            "gr": ret_norm[l].reshape(1, G_WIDTH),
        }
        parts = jnp.split(mod[l], 6, axis=-1)
        mods_p = [p[:bp, None, :] for p in parts]
        mods_s = [p[bp:, None, :] for p in parts]
        dsp0 = jnp.stack([jnp.zeros((COMBO_ROWS, 128), F32), jnp.full((COMBO_ROWS, 128), -1.0, F32),
                          jnp.zeros((COMBO_ROWS, 128), F32)])
        dispatch = (dsp0, jnp.zeros((ROUTE_ROWS, BLOCK_LANES), F32),
                    jnp.zeros(((n_blocks * MOE_BLOCK + ROW_TILE) * TOK_SUB, 128), jnp.uint32))
        x1p, pos_p, sa_np, sr_np, dispatch = _mixer(xp, mods_p[:5], consts, lw, sa_p, sr_p, 0, dispatch, prev_p)
        x1s, pos_s, sa_ns, sr_ns, dispatch = _mixer(xs_, mods_s[:5], consts, lw, state_hgrn[l].astype(F32),
                                                    state_ret[l].astype(F32), PAST_LEN, dispatch, prev_s)
        new_states.append((sa_np, sr_np, sa_ns, sr_ns))

        dsp, bc, xs = dispatch
        order, ea, eb, n_used = _block_order(dsp, bc, n_blocks)
        ys = _moe(order, ea, eb, n_used, xs, w1[l].astype(BF16), w3[l].astype(BF16), w2[l].astype(BF16))

        xp, xs_ = x1p, x1s
        prev_p = (pos_p, mods_p[5], ys)
        prev_s = (pos_s, mods_s[5], ys)

    nf = norm_f.reshape(1, D_MODEL)
    xp = _combine(prev_p[0], xp, prev_p[1], nf, prev_p[2])
    xs_ = _combine(prev_s[0], xs_, prev_s[1], nf, prev_s[2])

    sa_prompt = jnp.stack([s[0] for s in new_states])
    sr_prompt = jnp.stack([s[1] for s in new_states])
    sa_sample = jnp.stack([s[2] for s in new_states])
    sr_sample = jnp.stack([s[3] for s in new_states])
    return (xp, xs_, sa_prompt, sr_prompt, sa_sample, sr_sample)
```

```python
import functools

import jax
import jax.numpy as jnp
from jax import lax
from jax.experimental import pallas as pl
from jax.experimental.pallas import tpu as pltpu

F32 = jnp.float32
BF16 = jnp.bfloat16

D_MODEL = 1024
DEPTH = 2
N_HEADS = 4
D_HEAD = 128
G_WIDTH = N_HEADS * D_HEAD
IN_COLS = 8 * G_WIDTH
CHUNK = 64
CHUNK_UNROLL = 2
STAGE_GAP = (None,) * 3
SUB = 8
LEVELS = (32, 16, 8)
N_EXPERTS = 16
EXP_PER_GROUP = 4
N_GROUPS = 4
PAIRS_PER_GROUP = 6
N_COMBO = N_GROUPS * PAIRS_PER_GROUP
COMBO_ROWS = 32
ROUTE_ROWS = 8
TOK_SUB = 8
PACK_SUB = 4
ROW_TILE = 512
ISSUE_UNROLL = 8
MOE_BLOCK = 256
BLOCK_LANES = 384
ADA_TILE = 1536
ROPE_BASE = 10000.0
PAST_LEN = 1024
EPS = 1e-6
LOG2E = 1.4426950408889634
VMEM_LIMIT = 56 * 1024 * 1024

_NT = (((1,), (1,)), ((), ()))
_TN = (((0,), (0,)), ((), ()))


def _dot(a, b):
    return jnp.dot(a, b, preferred_element_type=F32)


def _dot_nt(a, b):
    return lax.dot_general(a, b, _NT, preferred_element_type=F32)


def _dot_tn(a, b):
    return lax.dot_general(a, b, _TN, preferred_element_type=F32)


def _tiling(batch, seq):
    tq = min(seq, ROW_TILE)
    bb = ROW_TILE // tq
    assert tq % CHUNK == 0 and bb * tq == ROW_TILE and batch % bb == 0 and seq % tq == 0
    return bb, tq


def _ada_kernel(c_ref, w_ref, b_ref, o_ref):
    c = c_ref[...]
    cs = c * jax.nn.sigmoid(c)
    o_ref[0] = _dot(cs.astype(BF16), w_ref[0].astype(BF16)) + b_ref[0]


def _ada(c_all, w_ada, b_ada):
    m = c_all.shape[0]
    n = w_ada.shape[-1]
    return pl.pallas_call(
        _ada_kernel,
        grid=(DEPTH, n // ADA_TILE),
        in_specs=[
            pl.BlockSpec((m, D_MODEL), lambda l, j: (0, 0)),
            pl.BlockSpec((1, D_MODEL, ADA_TILE), lambda l, j: (l, 0, j)),
            pl.BlockSpec((1, 1, ADA_TILE), lambda l, j: (l, 0, j)),
        ],
        out_specs=pl.BlockSpec((1, m, ADA_TILE), lambda l, j: (l, 0, j)),
        out_shape=jax.ShapeDtypeStruct((DEPTH, m, n), F32),
        compiler_params=pltpu.CompilerParams(vmem_limit_bytes=VMEM_LIMIT),
        name="ada",
    )(c_all, w_ada, b_ada.reshape(DEPTH, 1, n))


def _route(lt, br):
    sc = jax.nn.sigmoid(lt)
    bi = sc + br
    s = [sc[e:e + 1] for e in range(N_EXPERTS)]
    v = [bi[e:e + 1] for e in range(N_EXPERTS)]
    gs = []
    for g in range(N_GROUPS):
        a = v[EXP_PER_GROUP * g:EXP_PER_GROUP * (g + 1)]
        m = a[0] + a[1]
        for i in range(EXP_PER_GROUP):
            for j in range(i + 1, EXP_PER_GROUP):
                if (i, j) != (0, 1):
                    m = jnp.maximum(m, a[i] + a[j])
        gs.append(m)
    best = gs[0]
    gsel = jnp.zeros_like(best)
    for g in range(1, N_GROUPS):
        upd = gs[g] > best
        best = jnp.where(upd, gs[g], best)
        gsel = jnp.where(upd, float(g), gsel)

    def pick(vals, key, n):
        out = vals[0]
        for g in range(1, n):
            out = jnp.where(key == float(g), vals[g], out)
        return out

    vv = [pick([v[EXP_PER_GROUP * g + j] for g in range(N_GROUPS)], gsel, N_GROUPS) for j in range(EXP_PER_GROUP)]
    ss = [pick([s[EXP_PER_GROUP * g + j] for g in range(N_GROUPS)], gsel, N_GROUPS) for j in range(EXP_PER_GROUP)]
    m1 = vv[0]
    i1 = jnp.zeros_like(m1)
    for j in range(1, EXP_PER_GROUP):
        upd = vv[j] > m1
        m1 = jnp.where(upd, vv[j], m1)
        i1 = jnp.where(upd, float(j), i1)
    m2 = jnp.full_like(m1, -jnp.inf)
    i2 = jnp.zeros_like(m1)
    for j in range(EXP_PER_GROUP):
        upd = (i1 != float(j)) & (vv[j] > m2)
        m2 = jnp.where(upd, vv[j], m2)
        i2 = jnp.where(upd, float(j), i2)
    s1 = pick(ss, i1, EXP_PER_GROUP)
    s2 = pick(ss, i2, EXP_PER_GROUP)
    den = s1 + s2
    w1 = s1 / den
    w2 = s2 / den
    first_lo = i1 < i2
    lo = jnp.where(first_lo, i1, i2)
    hi = jnp.where(first_lo, i2, i1)
    w_lo = jnp.where(first_lo, w1, w2)
    w_hi = jnp.where(first_lo, w2, w1)
    pair = jnp.where(lo == 0.0, hi - 1.0, jnp.where(lo == 1.0, hi + 1.0, 5.0))
    combo = gsel * float(PAIRS_PER_GROUP) + pair
    zero = jnp.zeros_like(combo)
    return jnp.concatenate([combo, w_lo, w_hi] + [zero] * (ROUTE_ROWS - 3), axis=0)


def _mixer_kernel(pos_ref, x_ref, g2p_ref, ys_ref, sh1_ref, sc1_ref, g1_ref, sh2_ref, sc2_ref, na_ref, nm_ref,
                  win_ref, wout_ref, lb_ref, ga_ref, gr_ref, wrt_ref, br_ref, cos_ref, sin_ref, intra_ref,
                  inter_ref, kdec_ref, sdec_ref, tri_ref, triu_ref, lstrict_ref, sa0_ref, sr0_ref,
                  dsp0_ref, bc0_ref, xs_in_ref,
                  x1_ref, pos3_ref, sa_ref, sr_ref, dsp_ref, bc_ref, xs_ref,
                  proj_ref, ob_ref, sat_ref, gbuf_ref, hbuf_ref, posv_ref, poss_ref, gsem, ssem, psem, scsem,
                  *, bb_n, tq, gather, dummy_row):
    del xs_in_ref
    tt = pl.program_id(1)
    rows = bb_n * tq
    n_chunks = tq // CHUNK
    n_iter = bb_n * n_chunks // CHUNK_UNROLL
    step = pl.program_id(0) * pl.num_programs(1) + tt
    last_step = pl.num_programs(0) * pl.num_programs(1) - 1

    x = x_ref[...]
    if gather:
        slot = step % 2

        def fetch(tile, dst_slot, r):
            _tok_copy(ys_ref, pos_ref[tile * ROW_TILE + r], gbuf_ref.at[dst_slot], r, gsem.at[dst_slot]).start()

        @pl.when(step == 0)
        def _():
            def issue(g, carry):
                r0 = pl.multiple_of(g * ISSUE_UNROLL, ISSUE_UNROLL)
                for j in range(ISSUE_UNROLL):
                    fetch(0, 0, r0 + j)
                return carry
            lax.fori_loop(0, ROW_TILE // ISSUE_UNROLL, issue, 0)

        pltpu.make_async_copy(ys_ref.at[pl.ds(0, ROW_TILE * TOK_SUB), :], gbuf_ref.at[slot], gsem.at[slot]).wait()
        y_prev = _tok_columns(gbuf_ref, ROW_TILE, TOK_SUB, (slot,))
        x = x + g2p_ref[...] * y_prev.reshape(bb_n, tq, D_MODEL)
    ms = jnp.mean(x * x, axis=-1, keepdims=True)
    h = x * lax.rsqrt(ms + EPS) * na_ref[...] * (1.0 + sc1_ref[...]) + sh1_ref[...]
    proj_ref[...] = _dot(h.reshape(rows, D_MODEL).astype(BF16), win_ref[...])

    @pl.when(tt == 0)
    def _():
        b0 = pl.program_id(0) * bb_n
        load_a = pltpu.make_async_copy(sa0_ref.at[pl.ds(b0, bb_n)], sa_ref, ssem.at[0])
        load_r = pltpu.make_async_copy(sr0_ref.at[pl.ds(b0, bb_n)], sr_ref, ssem.at[1])
        load_a.start()
        load_r.start()
        load_a.wait()
        load_r.wait()
        for b in range(bb_n):
            for hh in range(N_HEADS):
                sat_ref[b, hh] = sa_ref[b, hh].T

    @pl.when(step == 0)
    def _():
        dsp_ref[...] = dsp0_ref[...]
        bc_ref[...] = bc0_ref[...]
        hbuf_ref[...] = jnp.zeros_like(hbuf_ref)
        posv_ref[...] = jnp.zeros_like(posv_ref)
        init = pltpu.make_async_copy(posv_ref.at[0], poss_ref, psem)
        init.start()
        init.wait()

    def put(r):
        dst = jnp.where(step > 0, poss_ref[r], dummy_row + r)
        _tok_copy(hbuf_ref, r, xs_ref, dst, scsem).start()

    lb = lb_ref[...]
    ln_1mlb = jnp.log1p(-lb)
    ga = ga_ref[...]
    gr = gr_ref[...]
    tri3 = tri_ref[...]
    t_i = lax.broadcasted_iota(jnp.int32, (CHUNK, CHUNK), 0)
    s_i = lax.broadcasted_iota(jnp.int32, (CHUNK, CHUNK), 1)
    level_masks = []
    for m in LEVELS:
        sh = m.bit_length() - 1
        level_masks.append(((t_i >> (sh + 1)) == (s_i >> (sh + 1)))
                           & (((t_i >> sh) & 1) == 1) & (((s_i >> sh) & 1) == 0))
    diag_mask = ((t_i >> 3) == (s_i >> 3)) & (s_i <= t_i)
    col_in_sub = s_i & (SUB - 1)

    def col(group, hh):
        return slice(group * G_WIDTH + hh * D_HEAD, group * G_WIDTH + (hh + 1) * D_HEAD)

    def hgrn_head(b, rs, hh):
        sl = slice(hh * D_HEAD, (hh + 1) * D_HEAD)
        z = proj_ref[rs, col(1, hh)]
        lbh = lb[:, sl]
        u = jnp.exp(-jnp.abs(z))
        ln_1pu = jnp.log(1.0 + u)
        num = jnp.where(z >= 0.0, 1.0 + lbh * u, lbh + u)
        l2f = (jnp.where(num > 0.0, jnp.log(num), z) - ln_1pu) * LOG2E
        l2k = (ln_1mlb[:, sl] - jnp.maximum(z, 0.0) - ln_1pu) * LOG2E
        hi = l2f.astype(BF16)
        rem = l2f - hi.astype(F32)
        mid = rem.astype(BF16)
        lo = (rem - mid.astype(F32)).astype(BF16)
        bcum = _dot(tri3, jnp.concatenate([hi, mid, lo], axis=0))
        yield from STAGE_GAP
        q = proj_ref[rs, col(0, hh)]
        vb = proj_ref[rs, col(2, hh)].astype(BF16)
        b_last = bcum[CHUNK - 1:CHUNK, :]
        q_in = (q * jnp.exp2(bcum)).astype(BF16)
        k_out = jnp.exp2(l2k + (b_last - bcum)).astype(BF16)
        upd = _dot_tn(vb, k_out)
        lvl = []
        for m in LEVELS:
            ref_row = bcum.reshape(CHUNK // (2 * m), 2 * m, D_HEAD)[:, m - 1:m, :]
            ref_row = jnp.broadcast_to(ref_row, (CHUNK // (2 * m), 2 * m, D_HEAD)).reshape(CHUNK, D_HEAD)
            dist = jnp.abs(bcum - ref_row)
            lvl.append(_dot_nt((q * jnp.exp2(-dist)).astype(BF16), jnp.exp2(l2k - dist).astype(BF16)))
        src3 = (l2k - bcum).reshape(CHUNK // SUB, SUB, D_HEAD)
        diag = jnp.zeros((CHUNK, CHUNK), F32)
        for sg in range(SUB):
            src = jnp.broadcast_to(src3[:, sg:sg + 1, :], src3.shape).reshape(CHUNK, D_HEAD)
            row_sum = jnp.sum(q * jnp.exp2(jnp.minimum(bcum + src, 0.0)), axis=-1, keepdims=True)
            diag = jnp.where(col_in_sub == sg, row_sum, diag)
        yield from STAGE_GAP
        a = jnp.where(diag_mask, diag, 0.0)
        for li in range(len(LEVELS)):
            a = jnp.where(level_masks[li], lvl[li], a)
        st = sat_ref[b, hh]
        o = _dot(a.astype(BF16), vb) + _dot_nt(q_in, st.astype(BF16))
        sat_ref[b, hh] = st * jnp.exp2(b_last) + upd
        yield from STAGE_GAP
        o = o * lax.rsqrt(jnp.mean(o * o, axis=-1, keepdims=True) + EPS)
        o = o * ga[:, sl] * jax.nn.sigmoid(proj_ref[rs, col(3, hh)])
        ob_ref[rs, sl] = o.astype(BF16)

    def ret_head(b, rs, t0, hh):
        sl = slice(hh * D_HEAD, (hh + 1) * D_HEAD)
        cs = cos_ref[pl.ds(t0, CHUNK), :]
        sn = sin_ref[pl.ds(t0, CHUNK), :]
        qh = proj_ref[rs, col(4, hh)]
        kh = proj_ref[rs, col(5, hh)]
        qr = qh * cs + pltpu.roll(qh, D_HEAD // 2, 1) * sn
        kr = (kh * cs + pltpu.roll(kh, D_HEAD // 2, 1) * sn) * (D_HEAD ** -0.5)
        qb = qr.astype(BF16)
        vh = proj_ref[rs, col(6, hh)].astype(BF16)
        sc = _dot_nt(qb, kr.astype(BF16))
        upd = _dot_tn((kr * kdec_ref[hh]).astype(BF16), vh)
        yield from STAGE_GAP
        s_r = sr_ref[b, hh]
        o = _dot((sc * intra_ref[hh]).astype(BF16), vh) + _dot(qb, s_r.astype(BF16)) * inter_ref[hh]
        sr_ref[b, hh] = s_r * sdec_ref[hh] + upd
        yield from STAGE_GAP
        mu = jnp.mean(o, axis=-1, keepdims=True)
        oc = o - mu
        var = jnp.mean(oc * oc, axis=-1, keepdims=True)
        o = oc * lax.rsqrt(var + EPS) * gr[:, sl]
        g = proj_ref[rs, col(7, hh)]
        o = o * (g * jax.nn.sigmoid(g))
        ob_ref[rs, col(1, hh)] = o.astype(BF16)

    def chunk_body(i, carry):
        per_iter = ROW_TILE // n_iter
        for j in range(per_iter):
            put(i * per_iter + j)
        if gather:
            next_tile = jnp.minimum(step + 1, last_step)
            for j in range(per_iter):
                fetch(next_tile, 1 - slot, i * per_iter + j)
        heads = []
        for j in range(CHUNK_UNROLL):
            idx = i * CHUNK_UNROLL + j
            b = idx // n_chunks
            rs = pl.ds(pl.multiple_of(idx * CHUNK, CHUNK), CHUNK)
            t0 = pl.multiple_of((idx - b * n_chunks) * CHUNK, CHUNK)
            for hh in range(N_HEADS):
                heads.append(hgrn_head(b, rs, hh))
                heads.append(ret_head(b, rs, t0, hh))
        live = []
        while heads or live:
            if heads:
                live.append(heads.pop(0))
            nxt = []
            for h in live:
                try:
                    next(h)
                    nxt.append(h)
                except StopIteration:
                    pass
            live = nxt
        return carry

    lax.fori_loop(0, n_iter, chunk_body, 0)
    pltpu.make_async_copy(hbuf_ref, xs_ref.at[pl.ds(0, ROW_TILE * TOK_SUB), :], scsem).wait()

    if gather:
        @pl.when(step == last_step)
        def _():
            pltpu.make_async_copy(ys_ref.at[pl.ds(0, ROW_TILE * TOK_SUB), :], gbuf_ref.at[1 - slot],
                                  gsem.at[1 - slot]).wait()

    @pl.when(tt == pl.num_programs(1) - 1)
    def _():
        for b in range(bb_n):
            for hh in range(N_HEADS):
                sa_ref[b, hh] = sat_ref[b, hh].T

    out = _dot(ob_ref[...], wout_ref[...]).reshape(bb_n, tq, D_MODEL)
    x1 = x + g1_ref[...] * out
    x1_ref[...] = x1
    ms2 = jnp.mean(x1 * x1, axis=-1, keepdims=True)
    h2 = x1 * lax.rsqrt(ms2 + EPS) * nm_ref[...] * (1.0 + sc2_ref[...]) + sh2_ref[...]
    h2b = h2.reshape(rows, D_MODEL).astype(BF16)
    lt = _dot_nt(wrt_ref[...], h2b)
    route = _route(lt, br_ref[...])
    bits = lax.bitcast_convert_type(h2b.astype(F32), jnp.uint32)
    words = (bits[:, 0:D_MODEL // 2] >> 16) | (bits[:, D_MODEL // 2:D_MODEL] & jnp.uint32(0xFFFF0000))
    for s in range(PACK_SUB):
        hbuf_ref[pl.ds(s, ROW_TILE, stride=TOK_SUB), :] = words[:, s * 128:(s + 1) * 128]
    rec = jnp.concatenate([route, jnp.zeros((128 - ROUTE_ROWS, rows), F32)], axis=0).T
    hbuf_ref[pl.ds(PACK_SUB, ROW_TILE, stride=TOK_SUB), :] = lax.bitcast_convert_type(rec, jnp.uint32)

    combo = route[0:1, :]
    ids = lax.broadcasted_iota(jnp.int32, (COMBO_ROWS, ROW_TILE), 0).astype(F32)
    onehot = (ids == combo).astype(F32)
    cum = _dot(onehot.astype(BF16), triu_ref[...])
    total = dsp_ref[0][:, 0:1]
    cur = dsp_ref[1][:, 0:1]
    n_alloc = dsp_ref[2][:, 0:1]
    n_c = cum[:, ROW_TILE - 1:ROW_TILE]
    inv_blk = 1.0 / MOE_BLOCK
    before = jnp.floor((total + float(MOE_BLOCK - 1)) * inv_blk)
    after = jnp.floor((total + n_c + float(MOE_BLOCK - 1)) * inv_blk)
    new = after - before
    new_wide = jnp.broadcast_to(new, (COMBO_ROWS, 128))
    first_new = n_alloc + _dot(lstrict_ref[...], new_wide.astype(BF16))[:, 0:1]
    ordinal = total + cum - 1.0
    lblk = jnp.floor(ordinal * inv_blk)
    offs = ordinal - lblk * float(MOE_BLOCK)
    blk = jnp.where(lblk < before, cur, first_new + (lblk - before))
    pos = jnp.sum(onehot * (blk * float(MOE_BLOCK) + offs), axis=0, keepdims=True)
    pos_i = jnp.broadcast_to(pos, (ROUTE_ROWS, ROW_TILE)).astype(jnp.int32)
    pos3_ref[0] = pos_i
    posv_ref[...] = pos_i
    wide = (COMBO_ROWS, 128)
    dsp_ref[0] = jnp.broadcast_to(total + n_c, wide)
    dsp_ref[1] = jnp.broadcast_to(jnp.where(new > 0.0, first_new + new - 1.0, cur), wide)
    dsp_ref[2] = jnp.broadcast_to(n_alloc + jnp.sum(new, axis=0, keepdims=True), wide)
    blk_id = lax.broadcasted_iota(jnp.int32, (COMBO_ROWS, BLOCK_LANES), 1).astype(F32)
    pair_id = lax.broadcasted_iota(jnp.int32, (COMBO_ROWS, BLOCK_LANES), 0).astype(F32)
    fresh = (blk_id >= first_new) & (blk_id < first_new + new)
    owner = jnp.sum(jnp.where(fresh, pair_id, 0.0), axis=0, keepdims=True)
    bc_ref[...] += jnp.broadcast_to(owner, bc_ref.shape)
    to_smem = pltpu.make_async_copy(posv_ref.at[0], poss_ref, psem)
    to_smem.start()
    to_smem.wait()

    @pl.when(step == last_step)
    def _():
        def issue(g, carry):
            r0 = pl.multiple_of(g * ISSUE_UNROLL, ISSUE_UNROLL)
            for j in range(ISSUE_UNROLL):
                _tok_copy(hbuf_ref, r0 + j, xs_ref, poss_ref[r0 + j], scsem).start()
            return carry
        lax.fori_loop(0, ROW_TILE // ISSUE_UNROLL, issue, 0)
        pltpu.make_async_copy(hbuf_ref, xs_ref.at[pl.ds(0, ROW_TILE * TOK_SUB), :], scsem).wait()


def _mixer(x, mods, consts, lw, s_a0, s_r0, offset, dispatch, prev=None):
    batch, seq, _ = x.shape
    bb_n, tq = _tiling(batch, seq)
    nb, nt = batch // bb_n, seq // tq
    cos, sin = _rope_tables(seq, offset)
    sh1, sc1, g1, sh2, sc2 = mods
    dsp, bc, xs = dispatch
    gather = prev is not None
    if gather:
        pos, g2p, ys = prev
    else:
        pos = jnp.zeros((1,), jnp.int32)
        g2p = jnp.zeros((batch, 1, D_MODEL), F32)
        ys = jnp.zeros((ROUTE_ROWS * TOK_SUB, 128), F32)

    def per_b(shape):
        return pl.BlockSpec(shape, lambda b, t, pos: (b,) + (0,) * (len(shape) - 1))

    def const(shape):
        return pl.BlockSpec(shape, lambda b, t, pos: (0,) * len(shape))

    any_spec = pl.BlockSpec(memory_space=pl.ANY)
    mod_spec = per_b((bb_n, 1, D_MODEL))
    state_spec = per_b((bb_n, N_HEADS, D_HEAD, D_HEAD))
    rope_spec = pl.BlockSpec((tq, D_HEAD), lambda b, t, pos: (t, 0))
    operands = [
        (x, pl.BlockSpec((bb_n, tq, D_MODEL), lambda b, t, pos: (b, t, 0))),
        (g2p, mod_spec), (ys, any_spec),
        (sh1, mod_spec), (sc1, mod_spec), (g1, mod_spec), (sh2, mod_spec), (sc2, mod_spec),
        (lw["na"], const((1, D_MODEL))), (lw["nm"], const((1, D_MODEL))),
        (lw["w_in"], const((D_MODEL, IN_COLS))), (lw["w_out"], const((2 * G_WIDTH, D_MODEL))),
        (lw["lb"], const((1, G_WIDTH))), (lw["ga"], const((1, G_WIDTH))), (lw["gr"], const((1, G_WIDTH))),
        (consts["wrt"], const((N_EXPERTS, D_MODEL))), (consts["br"], const((N_EXPERTS, 1))),
        (cos, rope_spec), (sin, rope_spec),
        (consts["intra"], const((N_HEADS, CHUNK, CHUNK))), (consts["inter"], const((N_HEADS, CHUNK, D_HEAD))),
        (consts["kdec"], const((N_HEADS, CHUNK, D_HEAD))), (consts["sdec"], const((N_HEADS, 1, D_HEAD))),
        (consts["tri"], const((CHUNK, 3 * CHUNK))), (consts["triu"], const((ROW_TILE, ROW_TILE))),
        (consts["lstrict"], const((COMBO_ROWS, COMBO_ROWS))),
        (s_a0, any_spec), (s_r0, any_spec),
        (dsp, const(dsp.shape)), (bc, const(bc.shape)), (xs, any_spec),
    ]
    out_specs = [
        pl.BlockSpec((bb_n, tq, D_MODEL), lambda b, t, pos: (b, t, 0)),
        pl.BlockSpec((1, ROUTE_ROWS, ROW_TILE), lambda b, t, pos: (b * nt + t, 0, 0)),
        state_spec, state_spec,
        const(dsp.shape), const(bc.shape), any_spec,
    ]
    out_shape = [
        jax.ShapeDtypeStruct((batch, seq, D_MODEL), F32),
        jax.ShapeDtypeStruct((nb * nt, ROUTE_ROWS, ROW_TILE), jnp.int32),
        jax.ShapeDtypeStruct((batch, N_HEADS, D_HEAD, D_HEAD), F32),
        jax.ShapeDtypeStruct((batch, N_HEADS, D_HEAD, D_HEAD), F32),
        jax.ShapeDtypeStruct(dsp.shape, F32),
        jax.ShapeDtypeStruct(bc.shape, F32),
        jax.ShapeDtypeStruct(xs.shape, xs.dtype),
    ]
    gather_rows = ROW_TILE if gather else ROUTE_ROWS
    x1, pos3, s_a, s_r, dsp, bc, xs = pl.pallas_call(
        functools.partial(_mixer_kernel, bb_n=bb_n, tq=tq, gather=gather, dummy_row=xs.shape[0] // TOK_SUB - ROW_TILE),
        grid_spec=pltpu.PrefetchScalarGridSpec(
            num_scalar_prefetch=1,
            grid=(nb, nt),
            in_specs=[spec for _, spec in operands],
            out_specs=out_specs,
            scratch_shapes=[
                pltpu.VMEM((ROW_TILE, IN_COLS), F32),
                pltpu.VMEM((ROW_TILE, 2 * G_WIDTH), BF16),
                pltpu.VMEM((bb_n, N_HEADS, D_HEAD, D_HEAD), F32),
                pltpu.VMEM((2, gather_rows * TOK_SUB, 128), F32),
                pltpu.VMEM((ROW_TILE * TOK_SUB, 128), jnp.uint32),
                pltpu.VMEM((ROUTE_ROWS, ROW_TILE), jnp.int32),
                pltpu.SMEM((ROW_TILE,), jnp.int32),
                pltpu.SemaphoreType.DMA((2,)),
                pltpu.SemaphoreType.DMA((2,)),
                pltpu.SemaphoreType.DMA,
                pltpu.SemaphoreType.DMA,
            ],
        ),
        out_shape=out_shape,
        input_output_aliases={len(operands): 6},
        compiler_params=pltpu.CompilerParams(
            dimension_semantics=("arbitrary", "arbitrary"), vmem_limit_bytes=VMEM_LIMIT),
        name="mixer",
    )(pos, *[arr for arr, _ in operands])
    return x1, pos3[:, 0, :].reshape(batch * seq), s_a, s_r, (dsp, bc, xs)


def _rope_tables(seq, offset):
    half = D_HEAD // 2
    freq = ROPE_BASE ** (-jnp.arange(half, dtype=F32) / half)
    ang = (jnp.arange(seq, dtype=F32) + offset)[:, None] * freq[None, :]
    cos, sin = jnp.cos(ang), jnp.sin(ang)
    return jnp.concatenate([cos, cos], axis=-1), jnp.concatenate([-sin, sin], axis=-1)


def _retention_tables():
    log_g = jnp.log(1.0 - 2.0 ** (-5.0 - jnp.arange(N_HEADS, dtype=F32)))
    idx = jnp.arange(CHUNK, dtype=F32)
    rel = idx[:, None] - idx[None, :]
    intra = jnp.where(rel >= 0, jnp.exp(jnp.maximum(rel, 0.0)[None] * log_g[:, None, None]), 0.0)
    inter = jnp.exp((idx[None, :] + 1.0) * log_g[:, None])[..., None]
    kdec = jnp.exp((CHUNK - 1.0 - idx[None, :]) * log_g[:, None])[..., None]
    sdec = jnp.exp(CHUNK * log_g)[:, None, None]
    wide = (N_HEADS, CHUNK, D_HEAD)
    return (intra, jnp.broadcast_to(inter, wide), jnp.broadcast_to(kdec, wide),
            jnp.broadcast_to(sdec, (N_HEADS, 1, D_HEAD)))


def _tok_copy(src_ref, src_tok, dst_ref, dst_tok, sem):
    src = src_ref.at[pl.ds(pl.multiple_of(src_tok * TOK_SUB, TOK_SUB), TOK_SUB), :]
    dst = dst_ref.at[pl.ds(pl.multiple_of(dst_tok * TOK_SUB, TOK_SUB), TOK_SUB), :]
    return pltpu.make_async_copy(src, dst, sem)


def _tok_columns(ref, n_tok, n_sub, lead=()):
    return jnp.concatenate([ref[lead + (pl.ds(s, n_tok, stride=TOK_SUB), slice(None))] for s in range(n_sub)], axis=1)


def _moe_kernel(order_ref, ea_ref, eb_ref, nu_ref, xs_ref, w1a, w3a, w2a, w1b, w3b, w2b, ys_ref):
    del order_ref, ea_ref, eb_ref
    j = pl.program_id(0)

    @pl.when(j < nu_ref[0])
    def _():
        words = _tok_columns(xs_ref, MOE_BLOCK, PACK_SUB)
        lo = lax.bitcast_convert_type(words << 16, F32)
        hi = lax.bitcast_convert_type(words & jnp.uint32(0xFFFF0000), F32)
        x = jnp.concatenate([lo, hi], axis=1).astype(BF16)
        rec = lax.bitcast_convert_type(xs_ref[pl.ds(PACK_SUB, MOE_BLOCK, stride=TOK_SUB), :], F32)
        w_lo = rec[:, 1:2]
        w_hi = rec[:, 2:3]

        def ffn(w1, w3, w2):
            h1 = _dot(x, w1[0])
            hid = (h1 * jax.nn.sigmoid(h1)) * _dot(x, w3[0])
            return _dot(hid.astype(BF16), w2[0])

        y = ffn(w1a, w3a, w2a) * w_lo + ffn(w1b, w3b, w2b) * w_hi
        for s in range(TOK_SUB):
            ys_ref[pl.ds(s, MOE_BLOCK, stride=TOK_SUB), :] = y[:, s * 128:(s + 1) * 128]

    @pl.when(j >= nu_ref[0])
    def _():
        ys_ref[...] = jnp.zeros_like(ys_ref)


def _moe(order, ea, eb, n_used, xs, w1, w3, w2, layer):
    n_blocks = order.shape[0]
    blk_rows = MOE_BLOCK * TOK_SUB

    def w_spec(which):
        if which == 0:
            return pl.BlockSpec((None, 1, D_MODEL, D_MODEL), lambda j, order, ea, eb, nu: (layer, ea[j], 0, 0))
        return pl.BlockSpec((None, 1, D_MODEL, D_MODEL), lambda j, order, ea, eb, nu: (layer, eb[j], 0, 0))

    return pl.pallas_call(
        _moe_kernel,
        grid_spec=pltpu.PrefetchScalarGridSpec(
            num_scalar_prefetch=4,
            grid=(n_blocks,),
            in_specs=[
                pl.BlockSpec((blk_rows, 128), lambda j, order, ea, eb, nu: (order[j], 0)),
                w_spec(0), w_spec(0), w_spec(0), w_spec(1), w_spec(1), w_spec(1),
            ],
            out_specs=pl.BlockSpec((blk_rows, 128), lambda j, order, ea, eb, nu: (order[j], 0)),
        ),
        out_shape=jax.ShapeDtypeStruct((n_blocks * blk_rows, 128), F32),
        compiler_params=pltpu.CompilerParams(dimension_semantics=("arbitrary",), vmem_limit_bytes=VMEM_LIMIT),
        name="moe",
    )(order, ea, eb, n_used, xs, w1, w3, w2, w1, w3, w2)


def _combine_kernel(pos_ref, x1_ref, g2_ref, nf_ref, ys_ref, out_ref, buf_ref, sem):
    step = pl.program_id(0) * pl.num_programs(1) + pl.program_id(1)
    last_step = pl.num_programs(0) * pl.num_programs(1) - 1
    slot = step % 2

    def fetch_tile(tile, dst_slot):
        def issue(g, carry):
            r0 = pl.multiple_of(g * ISSUE_UNROLL, ISSUE_UNROLL)
            for j in range(ISSUE_UNROLL):
                _tok_copy(ys_ref, pos_ref[tile * ROW_TILE + r0 + j], buf_ref.at[dst_slot], r0 + j,
                          sem.at[dst_slot]).start(priority=j % 2)
            return carry
        lax.fori_loop(0, ROW_TILE // ISSUE_UNROLL, issue, 0)

    def wait_tile(dst_slot):
        pltpu.make_async_copy(ys_ref.at[pl.ds(0, ROW_TILE * TOK_SUB), :], buf_ref.at[dst_slot],
                              sem.at[dst_slot]).wait()

    @pl.when(step == 0)
    def _():
        fetch_tile(0, 0)

    fetch_tile(jnp.minimum(step + 1, last_step), 1 - slot)
    wait_tile(slot)
    y = _tok_columns(buf_ref, ROW_TILE, TOK_SUB, (slot,))
    x2 = x1_ref[...] + g2_ref[...] * y.reshape(x1_ref.shape)
    out_ref[...] = x2 * lax.rsqrt(jnp.mean(x2 * x2, axis=-1, keepdims=True) + EPS) * nf_ref[...]

    @pl.when(step == last_step)
    def _():
        wait_tile(1 - slot)


def _combine(pos, x1, g2, nf, ys):
    batch, seq, _ = x1.shape
    bb_n, tq = _tiling(batch, seq)
    tok_spec = pl.BlockSpec((bb_n, tq, D_MODEL), lambda b, t, pos: (b, t, 0))
    return pl.pallas_call(
        _combine_kernel,
        grid_spec=pltpu.PrefetchScalarGridSpec(
            num_scalar_prefetch=1,
            grid=(batch // bb_n, seq // tq),
            in_specs=[
                tok_spec,
                pl.BlockSpec((bb_n, 1, D_MODEL), lambda b, t, pos: (b, 0, 0)),
                pl.BlockSpec((1, D_MODEL), lambda b, t, pos: (0, 0)),
                pl.BlockSpec(memory_space=pl.ANY),
            ],
            out_specs=tok_spec,
            scratch_shapes=[pltpu.VMEM((2, ROW_TILE * TOK_SUB, 128), F32), pltpu.SemaphoreType.DMA((2,))],
        ),
        out_shape=jax.ShapeDtypeStruct(x1.shape, F32),
        compiler_params=pltpu.CompilerParams(dimension_semantics=("arbitrary", "arbitrary")),
        name="combine",
    )(pos, x1, g2, nf, ys)


def _block_order(dsp, bc, n_blocks):
    n_used = dsp[2, 0, 0].astype(jnp.int32)
    blk = jnp.arange(n_blocks, dtype=jnp.int32)
    owner = jnp.where(blk < n_used, bc[0, :n_blocks].astype(jnp.int32), N_COMBO)
    order = jnp.argsort(owner, stable=True).astype(jnp.int32)
    combo = jnp.minimum(owner[order], N_COMBO - 1)
    last = combo[jnp.maximum(n_used - 1, 0)]
    combo = jnp.where(blk < n_used, combo, last)
    lo_tab = jnp.array([0, 0, 0, 1, 1, 2], jnp.int32)
    hi_tab = jnp.array([1, 2, 3, 2, 3, 3], jnp.int32)
    grp = combo // PAIRS_PER_GROUP
    pair = combo % PAIRS_PER_GROUP
    return order, grp * EXP_PER_GROUP + lo_tab[pair], grp * EXP_PER_GROUP + hi_tab[pair], n_used.reshape(1)


def kernel(x_prompt, x_sample, state_hgrn, state_ret, c_prompt, c_sample, w_in, w_out, lb_logits, hgrn_norm,
           ret_norm, norm_attn, norm_moe, w_ada, b_ada, w_router, b_router, w1, w3, w2, norm_f):
    bp, tp, _ = x_prompt.shape
    bs, ts, _ = x_sample.shape
    n_p, n_s = bp * tp, bs * ts
    n_tok = n_p + n_s
    assert n_tok % MOE_BLOCK == 0
    n_blocks = n_tok // MOE_BLOCK + N_COMBO
    assert n_blocks <= BLOCK_LANES

    lb_cum = jnp.cumsum(jax.nn.softmax(lb_logits.astype(F32), axis=0), axis=0)
    lb_all = lb_cum - lb_cum[0:1]

    mod = _ada(jnp.concatenate([c_prompt, c_sample], axis=0), w_ada, b_ada)

    intra, inter, kdec, sdec = _retention_tables()
    consts = {
        "wrt": w_router.T.astype(BF16), "br": b_router.astype(F32).reshape(N_EXPERTS, 1),
        "intra": intra, "inter": inter, "kdec": kdec, "sdec": sdec,
        "tri": jnp.tile(jnp.tril(jnp.ones((CHUNK, CHUNK), BF16)), (1, 3)),
        "triu": jnp.triu(jnp.ones((ROW_TILE, ROW_TILE), BF16)),
        "lstrict": jnp.tril(jnp.ones((COMBO_ROWS, COMBO_ROWS), BF16), k=-1),
    }

    w1_b, w3_b, w2_b = w1.astype(BF16), w3.astype(BF16), w2.astype(BF16)

    xp, xs_ = x_prompt, x_sample
    sa_p = jnp.zeros((bp, N_HEADS, D_HEAD, D_HEAD), F32)
    sr_p = jnp.zeros((bp, N_HEADS, D_HEAD, D_HEAD), F32)
    new_states = []
    prev_p = prev_s = None
    for l in range(DEPTH):
        lw = {
            "na": norm_attn[l].reshape(1, D_MODEL), "nm": norm_moe[l].reshape(1, D_MODEL),
            "w_in": w_in[l].astype(BF16), "w_out": w_out[l].astype(BF16),
            "lb": lb_all[l].reshape(1, G_WIDTH), "ga": hgrn_norm[l].reshape(1, G_WIDTH),
            "gr": ret_norm[l].reshape(1, G_WIDTH),
        }
        parts = jnp.split(mod[l], 6, axis=-1)
        mods_p = [p[:bp, None, :] for p in parts]
        mods_s = [p[bp:, None, :] for p in parts]
        dsp0 = jnp.stack([jnp.zeros((COMBO_ROWS, 128), F32), jnp.full((COMBO_ROWS, 128), -1.0, F32),
                          jnp.zeros((COMBO_ROWS, 128), F32)])
        dispatch = (dsp0, jnp.zeros((ROUTE_ROWS, BLOCK_LANES), F32),
                    jnp.zeros(((n_blocks * MOE_BLOCK + ROW_TILE) * TOK_SUB, 128), jnp.uint32))
        x1p, pos_p, sa_np, sr_np, dispatch = _mixer(xp, mods_p[:5], consts, lw, sa_p, sr_p, 0, dispatch, prev_p)
        x1s, pos_s, sa_ns, sr_ns, dispatch = _mixer(xs_, mods_s[:5], consts, lw, state_hgrn[l].astype(F32),
                                                    state_ret[l].astype(F32), PAST_LEN, dispatch, prev_s)
        new_states.append((sa_np, sr_np, sa_ns, sr_ns))

        dsp, bc, xs = dispatch
        order, ea, eb, n_used = _block_order(dsp, bc, n_blocks)
        ys = _moe(order, ea, eb, n_used, xs, w1_b, w3_b, w2_b, l)

        xp, xs_ = x1p, x1s
        prev_p = (pos_p, mods_p[5], ys)
        prev_s = (pos_s, mods_s[5], ys)

    nf = norm_f.reshape(1, D_MODEL)
    xp = _combine(prev_p[0], xp, prev_p[1], nf, prev_p[2])
    xs_ = _combine(prev_s[0], xs_, prev_s[1], nf, prev_s[2])

    sa_prompt = jnp.stack([s[0] for s in new_states])
    sr_prompt = jnp.stack([s[1] for s in new_states])
    sa_sample = jnp.stack([s[2] for s in new_states])
    sr_sample = jnp.stack([s[3] for s in new_states])
    return (xp, xs_, sa_prompt, sr_prompt, sa_sample, sr_sample)
```

```python
import functools

import jax
import jax.numpy as jnp
from jax import lax
from jax.experimental import pallas as pl
from jax.experimental.pallas import tpu as pltpu

F32 = jnp.float32
BF16 = jnp.bfloat16

D_MODEL = 1024
DEPTH = 2
N_HEADS = 4
D_HEAD = 128
G_WIDTH = N_HEADS * D_HEAD
IN_COLS = 8 * G_WIDTH
CHUNK = 64
CHUNK_UNROLL = 2
STAGE_GAP = (None,) * 3
SUB = 8
LEVELS = (32, 16, 8)
N_EXPERTS = 16
EXP_PER_GROUP = 4
N_GROUPS = 4
PAIRS_PER_GROUP = 6
N_COMBO = N_GROUPS * PAIRS_PER_GROUP
COMBO_ROWS = 32
ROUTE_ROWS = 8
TOK_SUB = 8
PACK_SUB = 4
ROW_TILE = 512
ISSUE_UNROLL = 8
MOE_BLOCK = 256
BLOCK_LANES = 384
ADA_TILE = 1536
ROPE_BASE = 10000.0
PAST_LEN = 1024
EPS = 1e-6
LOG2E = 1.4426950408889634
VMEM_LIMIT = 56 * 1024 * 1024

_NT = (((1,), (1,)), ((), ()))
_TN = (((0,), (0,)), ((), ()))


def _dot(a, b):
    return jnp.dot(a, b, preferred_element_type=F32)


def _dot_nt(a, b):
    return lax.dot_general(a, b, _NT, preferred_element_type=F32)


def _dot_tn(a, b):
    return lax.dot_general(a, b, _TN, preferred_element_type=F32)


def _tiling(batch, seq):
    tq = min(seq, ROW_TILE)
    bb = ROW_TILE // tq
    assert tq % CHUNK == 0 and bb * tq == ROW_TILE and batch % bb == 0 and seq % tq == 0
    return bb, tq


def _ada_kernel(c_ref, w_ref, b_ref, o_ref):
    c = c_ref[...]
    cs = c * jax.nn.sigmoid(c)
    o_ref[0] = _dot(cs.astype(BF16), w_ref[0].astype(BF16)) + b_ref[0]


def _ada(c_all, w_ada, b_ada):
    m = c_all.shape[0]
    n = w_ada.shape[-1]
    return pl.pallas_call(
        _ada_kernel,
        grid=(DEPTH, n // ADA_TILE),
        in_specs=[
            pl.BlockSpec((m, D_MODEL), lambda l, j: (0, 0)),
            pl.BlockSpec((1, D_MODEL, ADA_TILE), lambda l, j: (l, 0, j)),
            pl.BlockSpec((1, 1, ADA_TILE), lambda l, j: (l, 0, j)),
        ],
        out_specs=pl.BlockSpec((1, m, ADA_TILE), lambda l, j: (l, 0, j)),
        out_shape=jax.ShapeDtypeStruct((DEPTH, m, n), F32),
        compiler_params=pltpu.CompilerParams(vmem_limit_bytes=VMEM_LIMIT),
        name="ada",
    )(c_all, w_ada, b_ada.reshape(DEPTH, 1, n))


def _route(lt, br):
    sc = jax.nn.sigmoid(lt)
    bi = sc + br
    s = [sc[e:e + 1] for e in range(N_EXPERTS)]
    v = [bi[e:e + 1] for e in range(N_EXPERTS)]
    gs = []
    for g in range(N_GROUPS):
        a = v[EXP_PER_GROUP * g:EXP_PER_GROUP * (g + 1)]
        m = a[0] + a[1]
        for i in range(EXP_PER_GROUP):
            for j in range(i + 1, EXP_PER_GROUP):
                if (i, j) != (0, 1):
                    m = jnp.maximum(m, a[i] + a[j])
        gs.append(m)
    best = gs[0]
    gsel = jnp.zeros_like(best)
    for g in range(1, N_GROUPS):
        upd = gs[g] > best
        best = jnp.where(upd, gs[g], best)
        gsel = jnp.where(upd, float(g), gsel)

    def pick(vals, key, n):
        out = vals[0]
        for g in range(1, n):
            out = jnp.where(key == float(g), vals[g], out)
        return out

    vv = [pick([v[EXP_PER_GROUP * g + j] for g in range(N_GROUPS)], gsel, N_GROUPS) for j in range(EXP_PER_GROUP)]
    ss = [pick([s[EXP_PER_GROUP * g + j] for g in range(N_GROUPS)], gsel, N_GROUPS) for j in range(EXP_PER_GROUP)]
    m1 = vv[0]
    i1 = jnp.zeros_like(m1)
    for j in range(1, EXP_PER_GROUP):
        upd = vv[j] > m1
        m1 = jnp.where(upd, vv[j], m1)
        i1 = jnp.where(upd, float(j), i1)
    m2 = jnp.full_like(m1, -jnp.inf)
    i2 = jnp.zeros_like(m1)
    for j in range(EXP_PER_GROUP):
        upd = (i1 != float(j)) & (vv[j] > m2)
        m2 = jnp.where(upd, vv[j], m2)
        i2 = jnp.where(upd, float(j), i2)
    s1 = pick(ss, i1, EXP_PER_GROUP)
    s2 = pick(ss, i2, EXP_PER_GROUP)
    den = s1 + s2
    w1 = s1 / den
    w2 = s2 / den
    first_lo = i1 < i2
    lo = jnp.where(first_lo, i1, i2)
    hi = jnp.where(first_lo, i2, i1)
    w_lo = jnp.where(first_lo, w1, w2)
    w_hi = jnp.where(first_lo, w2, w1)
    pair = jnp.where(lo == 0.0, hi - 1.0, jnp.where(lo == 1.0, hi + 1.0, 5.0))
    combo = gsel * float(PAIRS_PER_GROUP) + pair
    zero = jnp.zeros_like(combo)
    return jnp.concatenate([combo, w_lo, w_hi] + [zero] * (ROUTE_ROWS - 3), axis=0)


def _mixer_kernel(pos_ref, x_ref, g2p_ref, ys_ref, sh1_ref, sc1_ref, g1_ref, sh2_ref, sc2_ref, na_ref, nm_ref,
                  win_ref, wout_ref, lb_ref, ga_ref, gr_ref, wrt_ref, br_ref, cos_ref, sin_ref, intra_ref,
                  inter_ref, kdec_ref, sdec_ref, tri_ref, triu_ref, lstrict_ref, sa0_ref, sr0_ref,
                  dsp0_ref, bc0_ref, xs_in_ref,
                  x1_ref, pos3_ref, sa_ref, sr_ref, dsp_ref, bc_ref, xs_ref,
                  proj_ref, ob_ref, sat_ref, gbuf_ref, hbuf_ref, posv_ref, poss_ref, gsem, ssem, psem, scsem,
                  *, bb_n, tq, gather, dummy_row):
    del xs_in_ref
    tt = pl.program_id(1)
    rows = bb_n * tq
    n_chunks = tq // CHUNK
    n_iter = bb_n * n_chunks // CHUNK_UNROLL
    step = pl.program_id(0) * pl.num_programs(1) + tt
    last_step = pl.num_programs(0) * pl.num_programs(1) - 1

    x = x_ref[...]
    if gather:
        slot = step % 2

        def fetch(tile, dst_slot, r, priority):
            _tok_copy(ys_ref, pos_ref[tile * ROW_TILE + r], gbuf_ref.at[dst_slot], r,
                      gsem.at[dst_slot]).start(priority=priority)

        @pl.when(step == 0)
        def _():
            def issue(g, carry):
                r0 = pl.multiple_of(g * ISSUE_UNROLL, ISSUE_UNROLL)
                for j in range(ISSUE_UNROLL):
                    fetch(0, 0, r0 + j, j % 2)
                return carry
            lax.fori_loop(0, ROW_TILE // ISSUE_UNROLL, issue, 0)

        pltpu.make_async_copy(ys_ref.at[pl.ds(0, ROW_TILE * TOK_SUB), :], gbuf_ref.at[slot], gsem.at[slot]).wait()
        y_prev = _tok_columns(gbuf_ref, ROW_TILE, TOK_SUB, (slot,))
        x = x + g2p_ref[...] * y_prev.reshape(bb_n, tq, D_MODEL)
    ms = jnp.mean(x * x, axis=-1, keepdims=True)
    h = x * lax.rsqrt(ms + EPS) * na_ref[...] * (1.0 + sc1_ref[...]) + sh1_ref[...]
    proj_ref[...] = _dot(h.reshape(rows, D_MODEL).astype(BF16), win_ref[...])

    @pl.when(tt == 0)
    def _():
        b0 = pl.program_id(0) * bb_n
        load_a = pltpu.make_async_copy(sa0_ref.at[pl.ds(b0, bb_n)], sa_ref, ssem.at[0])
        load_r = pltpu.make_async_copy(sr0_ref.at[pl.ds(b0, bb_n)], sr_ref, ssem.at[1])
        load_a.start()
        load_r.start()
        load_a.wait()
        load_r.wait()
        for b in range(bb_n):
            for hh in range(N_HEADS):
                sat_ref[b, hh] = sa_ref[b, hh].T

    @pl.when(step == 0)
    def _():
        dsp_ref[...] = dsp0_ref[...]
        bc_ref[...] = bc0_ref[...]
        hbuf_ref[...] = jnp.zeros_like(hbuf_ref)
        posv_ref[...] = jnp.zeros_like(posv_ref)
        init = pltpu.make_async_copy(posv_ref.at[0], poss_ref, psem)
        init.start()
        init.wait()

    def put(r, priority):
        dst = jnp.where(step > 0, poss_ref[r], dummy_row + r)
        _tok_copy(hbuf_ref, r, xs_ref, dst, scsem).start(priority=priority)

    lb = lb_ref[...]
    ln_1mlb = jnp.log1p(-lb)
    ga = ga_ref[...]
    gr = gr_ref[...]
    tri3 = tri_ref[...]
    t_i = lax.broadcasted_iota(jnp.int32, (CHUNK, CHUNK), 0)
    s_i = lax.broadcasted_iota(jnp.int32, (CHUNK, CHUNK), 1)
    level_masks = []
    for m in LEVELS:
        sh = m.bit_length() - 1
        level_masks.append(((t_i >> (sh + 1)) == (s_i >> (sh + 1)))
                           & (((t_i >> sh) & 1) == 1) & (((s_i >> sh) & 1) == 0))
    diag_mask = ((t_i >> 3) == (s_i >> 3)) & (s_i <= t_i)
    col_in_sub = s_i & (SUB - 1)

    def col(group, hh):
        return slice(group * G_WIDTH + hh * D_HEAD, group * G_WIDTH + (hh + 1) * D_HEAD)

    def hgrn_head(b, rs, hh):
        sl = slice(hh * D_HEAD, (hh + 1) * D_HEAD)
        z = proj_ref[rs, col(1, hh)]
        lbh = lb[:, sl]
        u = jnp.exp(-jnp.abs(z))
        ln_1pu = jnp.log(1.0 + u)
        num = jnp.where(z >= 0.0, 1.0 + lbh * u, lbh + u)
        l2f = (jnp.where(num > 0.0, jnp.log(num), z) - ln_1pu) * LOG2E
        l2k = (ln_1mlb[:, sl] - jnp.maximum(z, 0.0) - ln_1pu) * LOG2E
        hi = l2f.astype(BF16)
        rem = l2f - hi.astype(F32)
        mid = rem.astype(BF16)
        lo = (rem - mid.astype(F32)).astype(BF16)
        bcum = _dot(tri3, jnp.concatenate([hi, mid, lo], axis=0))
        yield from STAGE_GAP
        q = proj_ref[rs, col(0, hh)]
        vb = proj_ref[rs, col(2, hh)].astype(BF16)
        b_last = bcum[CHUNK - 1:CHUNK, :]
        q_in = (q * jnp.exp2(bcum)).astype(BF16)
        k_out = jnp.exp2(l2k + (b_last - bcum)).astype(BF16)
        upd = _dot_tn(vb, k_out)
        lvl = []
        for m in LEVELS:
            ref_row = bcum.reshape(CHUNK // (2 * m), 2 * m, D_HEAD)[:, m - 1:m, :]
            ref_row = jnp.broadcast_to(ref_row, (CHUNK // (2 * m), 2 * m, D_HEAD)).reshape(CHUNK, D_HEAD)
            dist = jnp.abs(bcum - ref_row)
            lvl.append(_dot_nt((q * jnp.exp2(-dist)).astype(BF16), jnp.exp2(l2k - dist).astype(BF16)))
        src3 = (l2k - bcum).reshape(CHUNK // SUB, SUB, D_HEAD)
        diag = jnp.zeros((CHUNK, CHUNK), F32)
        for sg in range(SUB):
            src = jnp.broadcast_to(src3[:, sg:sg + 1, :], src3.shape).reshape(CHUNK, D_HEAD)
            row_sum = jnp.sum(q * jnp.exp2(jnp.minimum(bcum + src, 0.0)), axis=-1, keepdims=True)
            diag = jnp.where(col_in_sub == sg, row_sum, diag)
        yield from STAGE_GAP
        a = jnp.where(diag_mask, diag, 0.0)
        for li in range(len(LEVELS)):
            a = jnp.where(level_masks[li], lvl[li], a)
        st = sat_ref[b, hh]
        o = _dot(a.astype(BF16), vb) + _dot_nt(q_in, st.astype(BF16))
        sat_ref[b, hh] = st * jnp.exp2(b_last) + upd
        yield from STAGE_GAP
        o = o * lax.rsqrt(jnp.mean(o * o, axis=-1, keepdims=True) + EPS)
        o = o * ga[:, sl] * jax.nn.sigmoid(proj_ref[rs, col(3, hh)])
        ob_ref[rs, sl] = o.astype(BF16)

    def ret_head(b, rs, t0, hh):
        sl = slice(hh * D_HEAD, (hh + 1) * D_HEAD)
        cs = cos_ref[pl.ds(t0, CHUNK), :]
        sn = sin_ref[pl.ds(t0, CHUNK), :]
        qh = proj_ref[rs, col(4, hh)]
        kh = proj_ref[rs, col(5, hh)]
        qr = qh * cs + pltpu.roll(qh, D_HEAD // 2, 1) * sn
        kr = (kh * cs + pltpu.roll(kh, D_HEAD // 2, 1) * sn) * (D_HEAD ** -0.5)
        qb = qr.astype(BF16)
        vh = proj_ref[rs, col(6, hh)].astype(BF16)
        sc = _dot_nt(qb, kr.astype(BF16))
        upd = _dot_tn((kr * kdec_ref[hh]).astype(BF16), vh)
        yield from STAGE_GAP
        s_r = sr_ref[b, hh]
        o = _dot((sc * intra_ref[hh]).astype(BF16), vh) + _dot(qb, s_r.astype(BF16)) * inter_ref[hh]
        sr_ref[b, hh] = s_r * sdec_ref[hh] + upd
        yield from STAGE_GAP
        mu = jnp.mean(o, axis=-1, keepdims=True)
        oc = o - mu
        var = jnp.mean(oc * oc, axis=-1, keepdims=True)
        o = oc * lax.rsqrt(var + EPS) * gr[:, sl]
        g = proj_ref[rs, col(7, hh)]
        o = o * (g * jax.nn.sigmoid(g))
        ob_ref[rs, col(1, hh)] = o.astype(BF16)

    def chunk_body(i, carry):
        per_iter = ROW_TILE // n_iter
        copies = []
        for j in range(per_iter):
            copies.append(functools.partial(put, i * per_iter + j, j % 2))
            if gather:
                next_tile = jnp.minimum(step + 1, last_step)
                copies.append(functools.partial(fetch, next_tile, 1 - slot, i * per_iter + j, (j + 1) % 2))
        per_turn = 2 if gather else 1
        heads = []
        for j in range(CHUNK_UNROLL):
            idx = i * CHUNK_UNROLL + j
            b = idx // n_chunks
            rs = pl.ds(pl.multiple_of(idx * CHUNK, CHUNK), CHUNK)
            t0 = pl.multiple_of((idx - b * n_chunks) * CHUNK, CHUNK)
            for hh in range(N_HEADS):
                heads.append(hgrn_head(b, rs, hh))
                heads.append(ret_head(b, rs, t0, hh))
        live = []
        while heads or live:
            if heads:
                live.append(heads.pop(0))
            nxt = []
            for h in live:
                try:
                    next(h)
                    nxt.append(h)
                except StopIteration:
                    pass
                for _ in range(min(per_turn, len(copies))):
                    copies.pop(0)()
            live = nxt
        while copies:
            copies.pop(0)()
        return carry

    lax.fori_loop(0, n_iter, chunk_body, 0)
    pltpu.make_async_copy(hbuf_ref, xs_ref.at[pl.ds(0, ROW_TILE * TOK_SUB), :], scsem).wait()

    if gather:
        @pl.when(step == last_step)
        def _():
            pltpu.make_async_copy(ys_ref.at[pl.ds(0, ROW_TILE * TOK_SUB), :], gbuf_ref.at[1 - slot],
                                  gsem.at[1 - slot]).wait()

    @pl.when(tt == pl.num_programs(1) - 1)
    def _():
        for b in range(bb_n):
            for hh in range(N_HEADS):
                sa_ref[b, hh] = sat_ref[b, hh].T

    out = _dot(ob_ref[...], wout_ref[...]).reshape(bb_n, tq, D_MODEL)
    x1 = x + g1_ref[...] * out
    x1_ref[...] = x1
    ms2 = jnp.mean(x1 * x1, axis=-1, keepdims=True)
    h2 = x1 * lax.rsqrt(ms2 + EPS) * nm_ref[...] * (1.0 + sc2_ref[...]) + sh2_ref[...]
    h2b = h2.reshape(rows, D_MODEL).astype(BF16)
    lt = _dot_nt(wrt_ref[...], h2b)
    route = _route(lt, br_ref[...])
    bits = lax.bitcast_convert_type(h2b.astype(F32), jnp.uint32)
    words = (bits[:, 0:D_MODEL // 2] >> 16) | (bits[:, D_MODEL // 2:D_MODEL] & jnp.uint32(0xFFFF0000))
    for s in range(PACK_SUB):
        hbuf_ref[pl.ds(s, ROW_TILE, stride=TOK_SUB), :] = words[:, s * 128:(s + 1) * 128]
    rec = jnp.concatenate([route, jnp.zeros((128 - ROUTE_ROWS, rows), F32)], axis=0).T
    hbuf_ref[pl.ds(PACK_SUB, ROW_TILE, stride=TOK_SUB), :] = lax.bitcast_convert_type(rec, jnp.uint32)

    combo = route[0:1, :]
    ids = lax.broadcasted_iota(jnp.int32, (COMBO_ROWS, ROW_TILE), 0).astype(F32)
    onehot = (ids == combo).astype(F32)
    cum = _dot(onehot.astype(BF16), triu_ref[...])
    total = dsp_ref[0][:, 0:1]
    cur = dsp_ref[1][:, 0:1]
    n_alloc = dsp_ref[2][:, 0:1]
    n_c = cum[:, ROW_TILE - 1:ROW_TILE]
    inv_blk = 1.0 / MOE_BLOCK
    before = jnp.floor((total + float(MOE_BLOCK - 1)) * inv_blk)
    after = jnp.floor((total + n_c + float(MOE_BLOCK - 1)) * inv_blk)
    new = after - before
    new_wide = jnp.broadcast_to(new, (COMBO_ROWS, 128))
    first_new = n_alloc + _dot(lstrict_ref[...], new_wide.astype(BF16))[:, 0:1]
    ordinal = total + cum - 1.0
    lblk = jnp.floor(ordinal * inv_blk)
    offs = ordinal - lblk * float(MOE_BLOCK)
    blk = jnp.where(lblk < before, cur, first_new + (lblk - before))
    pos = jnp.sum(onehot * (blk * float(MOE_BLOCK) + offs), axis=0, keepdims=True)
    pos_i = jnp.broadcast_to(pos, (ROUTE_ROWS, ROW_TILE)).astype(jnp.int32)
    pos3_ref[0] = pos_i
    posv_ref[...] = pos_i
    wide = (COMBO_ROWS, 128)
    dsp_ref[0] = jnp.broadcast_to(total + n_c, wide)
    dsp_ref[1] = jnp.broadcast_to(jnp.where(new > 0.0, first_new + new - 1.0, cur), wide)
    dsp_ref[2] = jnp.broadcast_to(n_alloc + jnp.sum(new, axis=0, keepdims=True), wide)
    blk_id = lax.broadcasted_iota(jnp.int32, (COMBO_ROWS, BLOCK_LANES), 1).astype(F32)
    pair_id = lax.broadcasted_iota(jnp.int32, (COMBO_ROWS, BLOCK_LANES), 0).astype(F32)
    fresh = (blk_id >= first_new) & (blk_id < first_new + new)
    owner = jnp.sum(jnp.where(fresh, pair_id, 0.0), axis=0, keepdims=True)
    bc_ref[...] += jnp.broadcast_to(owner, bc_ref.shape)
    to_smem = pltpu.make_async_copy(posv_ref.at[0], poss_ref, psem)
    to_smem.start()
    to_smem.wait()

    @pl.when(step == last_step)
    def _():
        def issue(g, carry):
            r0 = pl.multiple_of(g * ISSUE_UNROLL, ISSUE_UNROLL)
            for j in range(ISSUE_UNROLL):
                _tok_copy(hbuf_ref, r0 + j, xs_ref, poss_ref[r0 + j], scsem).start(priority=j % 2)
            return carry
        lax.fori_loop(0, ROW_TILE // ISSUE_UNROLL, issue, 0)
        pltpu.make_async_copy(hbuf_ref, xs_ref.at[pl.ds(0, ROW_TILE * TOK_SUB), :], scsem).wait()


def _mixer(x, mods, consts, lw, s_a0, s_r0, offset, dispatch, prev=None):
    batch, seq, _ = x.shape
    bb_n, tq = _tiling(batch, seq)
    nb, nt = batch // bb_n, seq // tq
    cos, sin = _rope_tables(seq, offset)
    sh1, sc1, g1, sh2, sc2 = mods
    dsp, bc, xs = dispatch
    gather = prev is not None
    if gather:
        pos, g2p, ys = prev
    else:
        pos = jnp.zeros((1,), jnp.int32)
        g2p = jnp.zeros((batch, 1, D_MODEL), F32)
        ys = jnp.zeros((ROUTE_ROWS * TOK_SUB, 128), F32)

    def per_b(shape):
        return pl.BlockSpec(shape, lambda b, t, pos: (b,) + (0,) * (len(shape) - 1))

    def const(shape):
        return pl.BlockSpec(shape, lambda b, t, pos: (0,) * len(shape))

    any_spec = pl.BlockSpec(memory_space=pl.ANY)
    mod_spec = per_b((bb_n, 1, D_MODEL))
    state_spec = per_b((bb_n, N_HEADS, D_HEAD, D_HEAD))
    rope_spec = pl.BlockSpec((tq, D_HEAD), lambda b, t, pos: (t, 0))
    operands = [
        (x, pl.BlockSpec((bb_n, tq, D_MODEL), lambda b, t, pos: (b, t, 0))),
        (g2p, mod_spec), (ys, any_spec),
        (sh1, mod_spec), (sc1, mod_spec), (g1, mod_spec), (sh2, mod_spec), (sc2, mod_spec),
        (lw["na"], const((1, D_MODEL))), (lw["nm"], const((1, D_MODEL))),
        (lw["w_in"], const((D_MODEL, IN_COLS))), (lw["w_out"], const((2 * G_WIDTH, D_MODEL))),
        (lw["lb"], const((1, G_WIDTH))), (lw["ga"], const((1, G_WIDTH))), (lw["gr"], const((1, G_WIDTH))),
        (consts["wrt"], const((N_EXPERTS, D_MODEL))), (consts["br"], const((N_EXPERTS, 1))),
        (cos, rope_spec), (sin, rope_spec),
        (consts["intra"], const((N_HEADS, CHUNK, CHUNK))), (consts["inter"], const((N_HEADS, CHUNK, D_HEAD))),
        (consts["kdec"], const((N_HEADS, CHUNK, D_HEAD))), (consts["sdec"], const((N_HEADS, 1, D_HEAD))),
        (consts["tri"], const((CHUNK, 3 * CHUNK))), (consts["triu"], const((ROW_TILE, ROW_TILE))),
        (consts["lstrict"], const((COMBO_ROWS, COMBO_ROWS))),
        (s_a0, any_spec), (s_r0, any_spec),
        (dsp, const(dsp.shape)), (bc, const(bc.shape)), (xs, any_spec),
    ]
    out_specs = [
        pl.BlockSpec((bb_n, tq, D_MODEL), lambda b, t, pos: (b, t, 0)),
        pl.BlockSpec((1, ROUTE_ROWS, ROW_TILE), lambda b, t, pos: (b * nt + t, 0, 0)),
        state_spec, state_spec,
        const(dsp.shape), const(bc.shape), any_spec,
    ]
    out_shape = [
        jax.ShapeDtypeStruct((batch, seq, D_MODEL), F32),
        jax.ShapeDtypeStruct((nb * nt, ROUTE_ROWS, ROW_TILE), jnp.int32),
        jax.ShapeDtypeStruct((batch, N_HEADS, D_HEAD, D_HEAD), F32),
        jax.ShapeDtypeStruct((batch, N_HEADS, D_HEAD, D_HEAD), F32),
        jax.ShapeDtypeStruct(dsp.shape, F32),
        jax.ShapeDtypeStruct(bc.shape, F32),
        jax.ShapeDtypeStruct(xs.shape, xs.dtype),
    ]
    gather_rows = ROW_TILE if gather else ROUTE_ROWS
    x1, pos3, s_a, s_r, dsp, bc, xs = pl.pallas_call(
        functools.partial(_mixer_kernel, bb_n=bb_n, tq=tq, gather=gather, dummy_row=xs.shape[0] // TOK_SUB - ROW_TILE),
        grid_spec=pltpu.PrefetchScalarGridSpec(
            num_scalar_prefetch=1,
            grid=(nb, nt),
            in_specs=[spec for _, spec in operands],
            out_specs=out_specs,
            scratch_shapes=[
                pltpu.VMEM((ROW_TILE, IN_COLS), F32),
                pltpu.VMEM((ROW_TILE, 2 * G_WIDTH), BF16),
                pltpu.VMEM((bb_n, N_HEADS, D_HEAD, D_HEAD), F32),
                pltpu.VMEM((2, gather_rows * TOK_SUB, 128), F32),
                pltpu.VMEM((ROW_TILE * TOK_SUB, 128), jnp.uint32),
                pltpu.VMEM((ROUTE_ROWS, ROW_TILE), jnp.int32),
                pltpu.SMEM((ROW_TILE,), jnp.int32),
                pltpu.SemaphoreType.DMA((2,)),
                pltpu.SemaphoreType.DMA((2,)),
                pltpu.SemaphoreType.DMA,
                pltpu.SemaphoreType.DMA,
            ],
        ),
        out_shape=out_shape,
        input_output_aliases={len(operands): 6},
        compiler_params=pltpu.CompilerParams(
            dimension_semantics=("arbitrary", "arbitrary"), vmem_limit_bytes=VMEM_LIMIT),
        name="mixer",
    )(pos, *[arr for arr, _ in operands])
    return x1, pos3[:, 0, :].reshape(batch * seq), s_a, s_r, (dsp, bc, xs)


def _rope_tables(seq, offset):
    half = D_HEAD // 2
    freq = ROPE_BASE ** (-jnp.arange(half, dtype=F32) / half)
    ang = (jnp.arange(seq, dtype=F32) + offset)[:, None] * freq[None, :]
    cos, sin = jnp.cos(ang), jnp.sin(ang)
    return jnp.concatenate([cos, cos], axis=-1), jnp.concatenate([-sin, sin], axis=-1)


def _retention_tables():
    log_g = jnp.log(1.0 - 2.0 ** (-5.0 - jnp.arange(N_HEADS, dtype=F32)))
    idx = jnp.arange(CHUNK, dtype=F32)
    rel = idx[:, None] - idx[None, :]
    intra = jnp.where(rel >= 0, jnp.exp(jnp.maximum(rel, 0.0)[None] * log_g[:, None, None]), 0.0)
    inter = jnp.exp((idx[None, :] + 1.0) * log_g[:, None])[..., None]
    kdec = jnp.exp((CHUNK - 1.0 - idx[None, :]) * log_g[:, None])[..., None]
    sdec = jnp.exp(CHUNK * log_g)[:, None, None]
    wide = (N_HEADS, CHUNK, D_HEAD)
    return (intra, jnp.broadcast_to(inter, wide), jnp.broadcast_to(kdec, wide),
            jnp.broadcast_to(sdec, (N_HEADS, 1, D_HEAD)))


def _tok_copy(src_ref, src_tok, dst_ref, dst_tok, sem):
    src = src_ref.at[pl.ds(pl.multiple_of(src_tok * TOK_SUB, TOK_SUB), TOK_SUB), :]
    dst = dst_ref.at[pl.ds(pl.multiple_of(dst_tok * TOK_SUB, TOK_SUB), TOK_SUB), :]
    return pltpu.make_async_copy(src, dst, sem)


def _tok_columns(ref, n_tok, n_sub, lead=()):
    return jnp.concatenate([ref[lead + (pl.ds(s, n_tok, stride=TOK_SUB), slice(None))] for s in range(n_sub)], axis=1)


def _moe_kernel(order_ref, ea_ref, eb_ref, nu_ref, xs_ref, w1a, w3a, w2a, w1b, w3b, w2b, ys_ref):
    del order_ref, ea_ref, eb_ref
    j = pl.program_id(0)

    @pl.when(j < nu_ref[0])
    def _():
        words = _tok_columns(xs_ref, MOE_BLOCK, PACK_SUB)
        lo = lax.bitcast_convert_type(words << 16, F32)
        hi = lax.bitcast_convert_type(words & jnp.uint32(0xFFFF0000), F32)
        x = jnp.concatenate([lo, hi], axis=1).astype(BF16)
        rec = lax.bitcast_convert_type(xs_ref[pl.ds(PACK_SUB, MOE_BLOCK, stride=TOK_SUB), :], F32)
        w_lo = rec[:, 1:2]
        w_hi = rec[:, 2:3]

        def ffn(w1, w3, w2):
            h1 = _dot(x, w1[0])
            hid = (h1 * jax.nn.sigmoid(h1)) * _dot(x, w3[0])
            return _dot(hid.astype(BF16), w2[0])

        y = ffn(w1a, w3a, w2a) * w_lo + ffn(w1b, w3b, w2b) * w_hi
        for s in range(TOK_SUB):
            ys_ref[pl.ds(s, MOE_BLOCK, stride=TOK_SUB), :] = y[:, s * 128:(s + 1) * 128]

    @pl.when(j >= nu_ref[0])
    def _():
        ys_ref[...] = jnp.zeros_like(ys_ref)


def _moe(order, ea, eb, n_used, xs, w1, w3, w2, layer):
    n_blocks = order.shape[0]
    blk_rows = MOE_BLOCK * TOK_SUB

    def w_spec(which):
        if which == 0:
            return pl.BlockSpec((None, 1, D_MODEL, D_MODEL), lambda j, order, ea, eb, nu: (layer, ea[j], 0, 0))
        return pl.BlockSpec((None, 1, D_MODEL, D_MODEL), lambda j, order, ea, eb, nu: (layer, eb[j], 0, 0))

    return pl.pallas_call(
        _moe_kernel,
        grid_spec=pltpu.PrefetchScalarGridSpec(
            num_scalar_prefetch=4,
            grid=(n_blocks,),
            in_specs=[
                pl.BlockSpec((blk_rows, 128), lambda j, order, ea, eb, nu: (order[j], 0)),
                w_spec(0), w_spec(0), w_spec(0), w_spec(1), w_spec(1), w_spec(1),
            ],
            out_specs=pl.BlockSpec((blk_rows, 128), lambda j, order, ea, eb, nu: (order[j], 0)),
        ),
        out_shape=jax.ShapeDtypeStruct((n_blocks * blk_rows, 128), F32),
        compiler_params=pltpu.CompilerParams(dimension_semantics=("arbitrary",), vmem_limit_bytes=VMEM_LIMIT),
        name="moe",
    )(order, ea, eb, n_used, xs, w1, w3, w2, w1, w3, w2)


def _combine_kernel(pos_ref, x1_ref, g2_ref, nf_ref, ys_ref, out_ref, buf_ref, sem):
    step = pl.program_id(0) * pl.num_programs(1) + pl.program_id(1)
    last_step = pl.num_programs(0) * pl.num_programs(1) - 1
    slot = step % 2

    def fetch_tile(tile, dst_slot):
        def issue(g, carry):
            r0 = pl.multiple_of(g * ISSUE_UNROLL, ISSUE_UNROLL)
            for j in range(ISSUE_UNROLL):
                _tok_copy(ys_ref, pos_ref[tile * ROW_TILE + r0 + j], buf_ref.at[dst_slot], r0 + j,
                          sem.at[dst_slot]).start(priority=j % 2)
            return carry
        lax.fori_loop(0, ROW_TILE // ISSUE_UNROLL, issue, 0)

    def wait_tile(dst_slot):
        pltpu.make_async_copy(ys_ref.at[pl.ds(0, ROW_TILE * TOK_SUB), :], buf_ref.at[dst_slot],
                              sem.at[dst_slot]).wait()

    @pl.when(step == 0)
    def _():
        fetch_tile(0, 0)

    fetch_tile(jnp.minimum(step + 1, last_step), 1 - slot)
    wait_tile(slot)
    y = _tok_columns(buf_ref, ROW_TILE, TOK_SUB, (slot,))
    x2 = x1_ref[...] + g2_ref[...] * y.reshape(x1_ref.shape)
    out_ref[...] = x2 * lax.rsqrt(jnp.mean(x2 * x2, axis=-1, keepdims=True) + EPS) * nf_ref[...]

    @pl.when(step == last_step)
    def _():
        wait_tile(1 - slot)


def _combine(pos, x1, g2, nf, ys):
    batch, seq, _ = x1.shape
    bb_n, tq = _tiling(batch, seq)
    tok_spec = pl.BlockSpec((bb_n, tq, D_MODEL), lambda b, t, pos: (b, t, 0))
    return pl.pallas_call(
        _combine_kernel,
        grid_spec=pltpu.PrefetchScalarGridSpec(
            num_scalar_prefetch=1,
            grid=(batch // bb_n, seq // tq),
            in_specs=[
                tok_spec,
                pl.BlockSpec((bb_n, 1, D_MODEL), lambda b, t, pos: (b, 0, 0)),
                pl.BlockSpec((1, D_MODEL), lambda b, t, pos: (0, 0)),
                pl.BlockSpec(memory_space=pl.ANY),
            ],
            out_specs=tok_spec,
            scratch_shapes=[pltpu.VMEM((2, ROW_TILE * TOK_SUB, 128), F32), pltpu.SemaphoreType.DMA((2,))],
        ),
        out_shape=jax.ShapeDtypeStruct(x1.shape, F32),
        compiler_params=pltpu.CompilerParams(dimension_semantics=("arbitrary", "arbitrary")),
        name="combine",
    )(pos, x1, g2, nf, ys)


def _block_order(dsp, bc, n_blocks):
    n_used = dsp[2, 0, 0].astype(jnp.int32)
    blk = jnp.arange(n_blocks, dtype=jnp.int32)
    owner = jnp.where(blk < n_used, bc[0, :n_blocks].astype(jnp.int32), N_COMBO)
    order = jnp.argsort(owner, stable=True).astype(jnp.int32)
    combo = jnp.minimum(owner[order], N_COMBO - 1)
    last = combo[jnp.maximum(n_used - 1, 0)]
    combo = jnp.where(blk < n_used, combo, last)
    lo_tab = jnp.array([0, 0, 0, 1, 1, 2], jnp.int32)
    hi_tab = jnp.array([1, 2, 3, 2, 3, 3], jnp.int32)
    grp = combo // PAIRS_PER_GROUP
    pair = combo % PAIRS_PER_GROUP
    return order, grp * EXP_PER_GROUP + lo_tab[pair], grp * EXP_PER_GROUP + hi_tab[pair], n_used.reshape(1)


def kernel(x_prompt, x_sample, state_hgrn, state_ret, c_prompt, c_sample, w_in, w_out, lb_logits, hgrn_norm,
           ret_norm, norm_attn, norm_moe, w_ada, b_ada, w_router, b_router, w1, w3, w2, norm_f):
    bp, tp, _ = x_prompt.shape
    bs, ts, _ = x_sample.shape
    n_p, n_s = bp * tp, bs * ts
    n_tok = n_p + n_s
    assert n_tok % MOE_BLOCK == 0
    n_blocks = n_tok // MOE_BLOCK + N_COMBO
    assert n_blocks <= BLOCK_LANES

    lb_cum = jnp.cumsum(jax.nn.softmax(lb_logits.astype(F32), axis=0), axis=0)
    lb_all = lb_cum - lb_cum[0:1]

    mod = _ada(jnp.concatenate([c_prompt, c_sample], axis=0), w_ada, b_ada)

    intra, inter, kdec, sdec = _retention_tables()
    consts = {
        "wrt": w_router.T.astype(BF16), "br": b_router.astype(F32).reshape(N_EXPERTS, 1),
        "intra": intra, "inter": inter, "kdec": kdec, "sdec": sdec,
        "tri": jnp.tile(jnp.tril(jnp.ones((CHUNK, CHUNK), BF16)), (1, 3)),
        "triu": jnp.triu(jnp.ones((ROW_TILE, ROW_TILE), BF16)),
        "lstrict": jnp.tril(jnp.ones((COMBO_ROWS, COMBO_ROWS), BF16), k=-1),
    }

    w1_b, w3_b, w2_b = w1.astype(BF16), w3.astype(BF16), w2.astype(BF16)

    xp, xs_ = x_prompt, x_sample
    sa_p = jnp.zeros((bp, N_HEADS, D_HEAD, D_HEAD), F32)
    sr_p = jnp.zeros((bp, N_HEADS, D_HEAD, D_HEAD), F32)
    new_states = []
    prev_p = prev_s = None
    for l in range(DEPTH):
        lw = {
            "na": norm_attn[l].reshape(1, D_MODEL), "nm": norm_moe[l].reshape(1, D_MODEL),
            "w_in": w_in[l].astype(BF16), "w_out": w_out[l].astype(BF16),
            "lb": lb_all[l].reshape(1, G_WIDTH), "ga": hgrn_norm[l].reshape(1, G_WIDTH),
            "gr": ret_norm[l].reshape(1, G_WIDTH),
        }
        parts = jnp.split(mod[l], 6, axis=-1)
        mods_p = [p[:bp, None, :] for p in parts]
        mods_s = [p[bp:, None, :] for p in parts]
        dsp0 = jnp.stack([jnp.zeros((COMBO_ROWS, 128), F32), jnp.full((COMBO_ROWS, 128), -1.0, F32),
                          jnp.zeros((COMBO_ROWS, 128), F32)])
        dispatch = (dsp0, jnp.zeros((ROUTE_ROWS, BLOCK_LANES), F32),
                    jnp.zeros(((n_blocks * MOE_BLOCK + ROW_TILE) * TOK_SUB, 128), jnp.uint32))
        x1p, pos_p, sa_np, sr_np, dispatch = _mixer(xp, mods_p[:5], consts, lw, sa_p, sr_p, 0, dispatch, prev_p)
        x1s, pos_s, sa_ns, sr_ns, dispatch = _mixer(xs_, mods_s[:5], consts, lw, state_hgrn[l].astype(F32),
                                                    state_ret[l].astype(F32), PAST_LEN, dispatch, prev_s)
        new_states.append((sa_np, sr_np, sa_ns, sr_ns))

        dsp, bc, xs = dispatch
        order, ea, eb, n_used = _block_order(dsp, bc, n_blocks)
        ys = _moe(order, ea, eb, n_used, xs, w1_b, w3_b, w2_b, l)

        xp, xs_ = x1p, x1s
        prev_p = (pos_p, mods_p[5], ys)
        prev_s = (pos_s, mods_s[5], ys)

    nf = norm_f.reshape(1, D_MODEL)
    xp = _combine(prev_p[0], xp, prev_p[1], nf, prev_p[2])
    xs_ = _combine(prev_s[0], xs_, prev_s[1], nf, prev_s[2])

    sa_prompt = jnp.stack([s[0] for s in new_states])
    sr_prompt = jnp.stack([s[1] for s in new_states])
    sa_sample = jnp.stack([s[2] for s in new_states])
    sr_sample = jnp.stack([s[3] for s in new_states])
    return (xp, xs_, sa_prompt, sr_prompt, sa_sample, sr_sample)
```

```python
import functools

import jax
import jax.numpy as jnp
from jax import lax
from jax.experimental import pallas as pl
from jax.experimental.pallas import tpu as pltpu

F32 = jnp.float32
BF16 = jnp.bfloat16

D_MODEL = 1024
DEPTH = 2
N_HEADS = 4
D_HEAD = 128
G_WIDTH = N_HEADS * D_HEAD
IN_COLS = 8 * G_WIDTH
CHUNK = 64
CHUNK_UNROLL = 2
STAGE_GAP = (None,) * 3
SUB = 8
LEVELS = (32, 16, 8)
N_EXPERTS = 16
EXP_PER_GROUP = 4
N_GROUPS = 4
PAIRS_PER_GROUP = 6
N_COMBO = N_GROUPS * PAIRS_PER_GROUP
COMBO_ROWS = 32
ROUTE_ROWS = 8
TOK_SUB = 8
PACK_SUB = 4
ROW_TILE = 512
ISSUE_UNROLL = 8
MOE_BLOCK = 256
BLOCK_LANES = 384
ADA_TILE = 1536
ROPE_BASE = 10000.0
PAST_LEN = 1024
EPS = 1e-6
LOG2E = 1.4426950408889634
VMEM_LIMIT = 56 * 1024 * 1024

_NT = (((1,), (1,)), ((), ()))
_TN = (((0,), (0,)), ((), ()))


def _dot(a, b):
    return jnp.dot(a, b, preferred_element_type=F32)


def _dot_nt(a, b):
    return lax.dot_general(a, b, _NT, preferred_element_type=F32)


def _dot_tn(a, b):
    return lax.dot_general(a, b, _TN, preferred_element_type=F32)


def _tiling(batch, seq):
    tq = min(seq, ROW_TILE)
    bb = ROW_TILE // tq
    assert tq % CHUNK == 0 and bb * tq == ROW_TILE and batch % bb == 0 and seq % tq == 0
    return bb, tq


def _ada_kernel(c_ref, w_ref, b_ref, o_ref):
    c = c_ref[...]
    cs = c * jax.nn.sigmoid(c)
    o_ref[0] = _dot(cs.astype(BF16), w_ref[0].astype(BF16)) + b_ref[0]


def _ada(c_all, w_ada, b_ada):
    m = c_all.shape[0]
    n = w_ada.shape[-1]
    return pl.pallas_call(
        _ada_kernel,
        grid=(DEPTH, n // ADA_TILE),
        in_specs=[
            pl.BlockSpec((m, D_MODEL), lambda l, j: (0, 0)),
            pl.BlockSpec((1, D_MODEL, ADA_TILE), lambda l, j: (l, 0, j)),
            pl.BlockSpec((1, 1, ADA_TILE), lambda l, j: (l, 0, j)),
        ],
        out_specs=pl.BlockSpec((1, m, ADA_TILE), lambda l, j: (l, 0, j)),
        out_shape=jax.ShapeDtypeStruct((DEPTH, m, n), F32),
        compiler_params=pltpu.CompilerParams(vmem_limit_bytes=VMEM_LIMIT),
        name="ada",
    )(c_all, w_ada, b_ada.reshape(DEPTH, 1, n))


def _route(lt, br):
    sc = jax.nn.sigmoid(lt)
    bi = sc + br
    s = [sc[e:e + 1] for e in range(N_EXPERTS)]
    v = [bi[e:e + 1] for e in range(N_EXPERTS)]
    gs = []
    for g in range(N_GROUPS):
        a = v[EXP_PER_GROUP * g:EXP_PER_GROUP * (g + 1)]
        m = a[0] + a[1]
        for i in range(EXP_PER_GROUP):
            for j in range(i + 1, EXP_PER_GROUP):
                if (i, j) != (0, 1):
                    m = jnp.maximum(m, a[i] + a[j])
        gs.append(m)
    best = gs[0]
    gsel = jnp.zeros_like(best)
    for g in range(1, N_GROUPS):
        upd = gs[g] > best
        best = jnp.where(upd, gs[g], best)
        gsel = jnp.where(upd, float(g), gsel)

    def pick(vals, key, n):
        out = vals[0]
        for g in range(1, n):
            out = jnp.where(key == float(g), vals[g], out)
        return out

    vv = [pick([v[EXP_PER_GROUP * g + j] for g in range(N_GROUPS)], gsel, N_GROUPS) for j in range(EXP_PER_GROUP)]
    ss = [pick([s[EXP_PER_GROUP * g + j] for g in range(N_GROUPS)], gsel, N_GROUPS) for j in range(EXP_PER_GROUP)]
    m1 = vv[0]
    i1 = jnp.zeros_like(m1)
    for j in range(1, EXP_PER_GROUP):
        upd = vv[j] > m1
        m1 = jnp.where(upd, vv[j], m1)
        i1 = jnp.where(upd, float(j), i1)
    m2 = jnp.full_like(m1, -jnp.inf)
    i2 = jnp.zeros_like(m1)
    for j in range(EXP_PER_GROUP):
        upd = (i1 != float(j)) & (vv[j] > m2)
        m2 = jnp.where(upd, vv[j], m2)
        i2 = jnp.where(upd, float(j), i2)
    s1 = pick(ss, i1, EXP_PER_GROUP)
    s2 = pick(ss, i2, EXP_PER_GROUP)
    den = s1 + s2
    w1 = s1 / den
    w2 = s2 / den
    first_lo = i1 < i2
    lo = jnp.where(first_lo, i1, i2)
    hi = jnp.where(first_lo, i2, i1)
    w_lo = jnp.where(first_lo, w1, w2)
    w_hi = jnp.where(first_lo, w2, w1)
    pair = jnp.where(lo == 0.0, hi - 1.0, jnp.where(lo == 1.0, hi + 1.0, 5.0))
    combo = gsel * float(PAIRS_PER_GROUP) + pair
    zero = jnp.zeros_like(combo)
    return jnp.concatenate([combo, w_lo, w_hi] + [zero] * (ROUTE_ROWS - 3), axis=0)


def _mixer_kernel(pos_ref, x_ref, g2p_ref, ys_ref, sh1_ref, sc1_ref, g1_ref, sh2_ref, sc2_ref, na_ref, nm_ref,
                  win_ref, wout_ref, lb_ref, ga_ref, gr_ref, wrt_ref, br_ref, cos_ref, sin_ref, intra_ref,
                  inter_ref, kdec_ref, sdec_ref, tri_ref, triu_ref, lstrict_ref, sa0_ref, sr0_ref,
                  dsp0_ref, bc0_ref, xs_in_ref,
                  x1_ref, pos3_ref, sa_ref, sr_ref, dsp_ref, bc_ref, xs_ref,
                  proj_ref, ob_ref, sat_ref, gbuf_ref, hbuf_ref, posv_ref, poss_ref, gsem, ssem, psem, scsem,
                  *, bb_n, tq, gather, dummy_row):
    del xs_in_ref
    tt = pl.program_id(1)
    rows = bb_n * tq
    n_chunks = tq // CHUNK
    n_iter = bb_n * n_chunks // CHUNK_UNROLL
    step = pl.program_id(0) * pl.num_programs(1) + tt
    last_step = pl.num_programs(0) * pl.num_programs(1) - 1

    x = x_ref[...]
    if gather:
        slot = step % 2

        def fetch(tile, dst_slot, r, priority):
            _tok_copy(ys_ref, pos_ref[tile * ROW_TILE + r], gbuf_ref.at[dst_slot], r,
                      gsem.at[dst_slot]).start(priority=priority)

        @pl.when(step == 0)
        def _():
            def issue(g, carry):
                r0 = pl.multiple_of(g * ISSUE_UNROLL, ISSUE_UNROLL)
                for j in range(ISSUE_UNROLL):
                    fetch(0, 0, r0 + j, j % 2)
                return carry
            lax.fori_loop(0, ROW_TILE // ISSUE_UNROLL, issue, 0)

        pltpu.make_async_copy(ys_ref.at[pl.ds(0, ROW_TILE * TOK_SUB), :], gbuf_ref.at[slot], gsem.at[slot]).wait()
        y_prev = _tok_columns(gbuf_ref, ROW_TILE, TOK_SUB, (slot,))
        x = x + g2p_ref[...] * y_prev.reshape(bb_n, tq, D_MODEL)

    @pl.when(tt == 0)
    def _():
        b0 = pl.program_id(0) * bb_n
        load_a = pltpu.make_async_copy(sa0_ref.at[pl.ds(b0, bb_n)], sa_ref, ssem.at[0])
        load_r = pltpu.make_async_copy(sr0_ref.at[pl.ds(b0, bb_n)], sr_ref, ssem.at[1])
        load_a.start()
        load_r.start()
        load_a.wait()
        load_r.wait()
        for b in range(bb_n):
            for hh in range(N_HEADS):
                sat_ref[b, hh] = sa_ref[b, hh].T

    @pl.when(step == 0)
    def _():
        dsp_ref[...] = dsp0_ref[...]
        bc_ref[...] = bc0_ref[...]
        hbuf_ref[...] = jnp.zeros_like(hbuf_ref)
        posv_ref[...] = jnp.zeros_like(posv_ref)
        init = pltpu.make_async_copy(posv_ref.at[0], poss_ref, psem)
        init.start()
        init.wait()

    def put(r, priority):
        dst = jnp.where(step > 0, poss_ref[r], dummy_row + r)
        _tok_copy(hbuf_ref, r, xs_ref, dst, scsem).start(priority=priority)

    for r in range(ROW_TILE):
        put(r, r % 2)
    if gather:
        next_tile = jnp.minimum(step + 1, last_step)
        for r in range(ROW_TILE):
            fetch(next_tile, 1 - slot, r, (r + 1) % 2)
    ms = jnp.mean(x * x, axis=-1, keepdims=True)
    h = x * lax.rsqrt(ms + EPS) * na_ref[...] * (1.0 + sc1_ref[...]) + sh1_ref[...]
    proj_ref[...] = _dot(h.reshape(rows, D_MODEL).astype(BF16), win_ref[...])

    lb = lb_ref[...]
    ln_1mlb = jnp.log1p(-lb)
    ga = ga_ref[...]
    gr = gr_ref[...]
    tri3 = tri_ref[...]
    t_i = lax.broadcasted_iota(jnp.int32, (CHUNK, CHUNK), 0)
    s_i = lax.broadcasted_iota(jnp.int32, (CHUNK, CHUNK), 1)
    level_masks = []
    for m in LEVELS:
        sh = m.bit_length() - 1
        level_masks.append(((t_i >> (sh + 1)) == (s_i >> (sh + 1)))
                           & (((t_i >> sh) & 1) == 1) & (((s_i >> sh) & 1) == 0))
    diag_mask = ((t_i >> 3) == (s_i >> 3)) & (s_i <= t_i)
    col_in_sub = s_i & (SUB - 1)

    def col(group, hh):
        return slice(group * G_WIDTH + hh * D_HEAD, group * G_WIDTH + (hh + 1) * D_HEAD)

    def hgrn_head(b, rs, hh):
        sl = slice(hh * D_HEAD, (hh + 1) * D_HEAD)
        z = proj_ref[rs, col(1, hh)]
        lbh = lb[:, sl]
        u = jnp.exp(-jnp.abs(z))
        ln_1pu = jnp.log(1.0 + u)
        num = jnp.where(z >= 0.0, 1.0 + lbh * u, lbh + u)
        l2f = (jnp.where(num > 0.0, jnp.log(num), z) - ln_1pu) * LOG2E
        l2k = (ln_1mlb[:, sl] - jnp.maximum(z, 0.0) - ln_1pu) * LOG2E
        hi = l2f.astype(BF16)
        rem = l2f - hi.astype(F32)
        mid = rem.astype(BF16)
        lo = (rem - mid.astype(F32)).astype(BF16)
        bcum = _dot(tri3, jnp.concatenate([hi, mid, lo], axis=0))
        yield from STAGE_GAP
        q = proj_ref[rs, col(0, hh)]
        vb = proj_ref[rs, col(2, hh)].astype(BF16)
        b_last = bcum[CHUNK - 1:CHUNK, :]
        q_in = (q * jnp.exp2(bcum)).astype(BF16)
        k_out = jnp.exp2(l2k + (b_last - bcum)).astype(BF16)
        upd = _dot_tn(vb, k_out)
        lvl = []
        for m in LEVELS:
            ref_row = bcum.reshape(CHUNK // (2 * m), 2 * m, D_HEAD)[:, m - 1:m, :]
            ref_row = jnp.broadcast_to(ref_row, (CHUNK // (2 * m), 2 * m, D_HEAD)).reshape(CHUNK, D_HEAD)
            dist = jnp.abs(bcum - ref_row)
            lvl.append(_dot_nt((q * jnp.exp2(-dist)).astype(BF16), jnp.exp2(l2k - dist).astype(BF16)))
        src3 = (l2k - bcum).reshape(CHUNK // SUB, SUB, D_HEAD)
        diag = jnp.zeros((CHUNK, CHUNK), F32)
        for sg in range(SUB):
            src = jnp.broadcast_to(src3[:, sg:sg + 1, :], src3.shape).reshape(CHUNK, D_HEAD)
            row_sum = jnp.sum(q * jnp.exp2(jnp.minimum(bcum + src, 0.0)), axis=-1, keepdims=True)
            diag = jnp.where(col_in_sub == sg, row_sum, diag)
        yield from STAGE_GAP
        a = jnp.where(diag_mask, diag, 0.0)
        for li in range(len(LEVELS)):
            a = jnp.where(level_masks[li], lvl[li], a)
        st = sat_ref[b, hh]
        o = _dot(a.astype(BF16), vb) + _dot_nt(q_in, st.astype(BF16))
        sat_ref[b, hh] = st * jnp.exp2(b_last) + upd
        yield from STAGE_GAP
        o = o * lax.rsqrt(jnp.mean(o * o, axis=-1, keepdims=True) + EPS)
        o = o * ga[:, sl] * jax.nn.sigmoid(proj_ref[rs, col(3, hh)])
        ob_ref[rs, sl] = o.astype(BF16)

    def ret_head(b, rs, t0, hh):
        sl = slice(hh * D_HEAD, (hh + 1) * D_HEAD)
        cs = cos_ref[pl.ds(t0, CHUNK), :]
        sn = sin_ref[pl.ds(t0, CHUNK), :]
        qh = proj_ref[rs, col(4, hh)]
        kh = proj_ref[rs, col(5, hh)]
        qr = qh * cs + pltpu.roll(qh, D_HEAD // 2, 1) * sn
        kr = (kh * cs + pltpu.roll(kh, D_HEAD // 2, 1) * sn) * (D_HEAD ** -0.5)
        qb = qr.astype(BF16)
        vh = proj_ref[rs, col(6, hh)].astype(BF16)
        sc = _dot_nt(qb, kr.astype(BF16))
        upd = _dot_tn((kr * kdec_ref[hh]).astype(BF16), vh)
        yield from STAGE_GAP
        s_r = sr_ref[b, hh]
        o = _dot((sc * intra_ref[hh]).astype(BF16), vh) + _dot(qb, s_r.astype(BF16)) * inter_ref[hh]
        sr_ref[b, hh] = s_r * sdec_ref[hh] + upd
        yield from STAGE_GAP
        mu = jnp.mean(o, axis=-1, keepdims=True)
        oc = o - mu
        var = jnp.mean(oc * oc, axis=-1, keepdims=True)
        o = oc * lax.rsqrt(var + EPS) * gr[:, sl]
        g = proj_ref[rs, col(7, hh)]
        o = o * (g * jax.nn.sigmoid(g))
        ob_ref[rs, col(1, hh)] = o.astype(BF16)

    def chunk_body(i, carry):
        heads = []
        for j in range(CHUNK_UNROLL):
            idx = i * CHUNK_UNROLL + j
            b = idx // n_chunks
            rs = pl.ds(pl.multiple_of(idx * CHUNK, CHUNK), CHUNK)
            t0 = pl.multiple_of((idx - b * n_chunks) * CHUNK, CHUNK)
            for hh in range(N_HEADS):
                heads.append(hgrn_head(b, rs, hh))
                heads.append(ret_head(b, rs, t0, hh))
        live = []
        while heads or live:
            if heads:
                live.append(heads.pop(0))
            nxt = []
            for h in live:
                try:
                    next(h)
                    nxt.append(h)
                except StopIteration:
                    pass
            live = nxt
        return carry

    lax.fori_loop(0, n_iter, chunk_body, 0)
    pltpu.make_async_copy(hbuf_ref, xs_ref.at[pl.ds(0, ROW_TILE * TOK_SUB), :], scsem).wait()

    @pl.when(tt == pl.num_programs(1) - 1)
    def _():
        for b in range(bb_n):
            for hh in range(N_HEADS):
                sa_ref[b, hh] = sat_ref[b, hh].T

    out = _dot(ob_ref[...], wout_ref[...]).reshape(bb_n, tq, D_MODEL)
    x1 = x + g1_ref[...] * out
    x1_ref[...] = x1
    ms2 = jnp.mean(x1 * x1, axis=-1, keepdims=True)
    h2 = x1 * lax.rsqrt(ms2 + EPS) * nm_ref[...] * (1.0 + sc2_ref[...]) + sh2_ref[...]
    h2b = h2.reshape(rows, D_MODEL).astype(BF16)
    lt = _dot_nt(wrt_ref[...], h2b)
    route = _route(lt, br_ref[...])
    bits = lax.bitcast_convert_type(h2b.astype(F32), jnp.uint32)
    words = (bits[:, 0:D_MODEL // 2] >> 16) | (bits[:, D_MODEL // 2:D_MODEL] & jnp.uint32(0xFFFF0000))
    for s in range(PACK_SUB):
        hbuf_ref[pl.ds(s, ROW_TILE, stride=TOK_SUB), :] = words[:, s * 128:(s + 1) * 128]
    rec = jnp.concatenate([route, jnp.zeros((128 - ROUTE_ROWS, rows), F32)], axis=0).T
    hbuf_ref[pl.ds(PACK_SUB, ROW_TILE, stride=TOK_SUB), :] = lax.bitcast_convert_type(rec, jnp.uint32)

    combo = route[0:1, :]
    ids = lax.broadcasted_iota(jnp.int32, (COMBO_ROWS, ROW_TILE), 0).astype(F32)
    onehot = (ids == combo).astype(F32)
    cum = _dot(onehot.astype(BF16), triu_ref[...])
    total = dsp_ref[0][:, 0:1]
    cur = dsp_ref[1][:, 0:1]
    n_alloc = dsp_ref[2][:, 0:1]
    n_c = cum[:, ROW_TILE - 1:ROW_TILE]
    inv_blk = 1.0 / MOE_BLOCK
    before = jnp.floor((total + float(MOE_BLOCK - 1)) * inv_blk)
    after = jnp.floor((total + n_c + float(MOE_BLOCK - 1)) * inv_blk)
    new = after - before
    new_wide = jnp.broadcast_to(new, (COMBO_ROWS, 128))
    first_new = n_alloc + _dot(lstrict_ref[...], new_wide.astype(BF16))[:, 0:1]
    ordinal = total + cum - 1.0
    lblk = jnp.floor(ordinal * inv_blk)
    offs = ordinal - lblk * float(MOE_BLOCK)
    blk = jnp.where(lblk < before, cur, first_new + (lblk - before))
    pos = jnp.sum(onehot * (blk * float(MOE_BLOCK) + offs), axis=0, keepdims=True)
    pos_i = jnp.broadcast_to(pos, (ROUTE_ROWS, ROW_TILE)).astype(jnp.int32)
    pos3_ref[0] = pos_i
    posv_ref[...] = pos_i
    wide = (COMBO_ROWS, 128)
    dsp_ref[0] = jnp.broadcast_to(total + n_c, wide)
    dsp_ref[1] = jnp.broadcast_to(jnp.where(new > 0.0, first_new + new - 1.0, cur), wide)
    dsp_ref[2] = jnp.broadcast_to(n_alloc + jnp.sum(new, axis=0, keepdims=True), wide)
    blk_id = lax.broadcasted_iota(jnp.int32, (COMBO_ROWS, BLOCK_LANES), 1).astype(F32)
    pair_id = lax.broadcasted_iota(jnp.int32, (COMBO_ROWS, BLOCK_LANES), 0).astype(F32)
    fresh = (blk_id >= first_new) & (blk_id < first_new + new)
    owner = jnp.sum(jnp.where(fresh, pair_id, 0.0), axis=0, keepdims=True)
    bc_ref[...] += jnp.broadcast_to(owner, bc_ref.shape)
    to_smem = pltpu.make_async_copy(posv_ref.at[0], poss_ref, psem)
    to_smem.start()
    to_smem.wait()

    @pl.when(step == last_step)
    def _():
        def issue(g, carry):
            r0 = pl.multiple_of(g * ISSUE_UNROLL, ISSUE_UNROLL)
            for j in range(ISSUE_UNROLL):
                _tok_copy(hbuf_ref, r0 + j, xs_ref, poss_ref[r0 + j], scsem).start(priority=j % 2)
            return carry
        lax.fori_loop(0, ROW_TILE // ISSUE_UNROLL, issue, 0)
        pltpu.make_async_copy(hbuf_ref, xs_ref.at[pl.ds(0, ROW_TILE * TOK_SUB), :], scsem).wait()
        if gather:
            pltpu.make_async_copy(ys_ref.at[pl.ds(0, ROW_TILE * TOK_SUB), :], gbuf_ref.at[1 - slot],
                                  gsem.at[1 - slot]).wait()


def _mixer(x, mods, consts, lw, s_a0, s_r0, offset, dispatch, prev=None):
    batch, seq, _ = x.shape
    bb_n, tq = _tiling(batch, seq)
    nb, nt = batch // bb_n, seq // tq
    cos, sin = _rope_tables(seq, offset)
    sh1, sc1, g1, sh2, sc2 = mods
    dsp, bc, xs = dispatch
    gather = prev is not None
    if gather:
        pos, g2p, ys = prev
    else:
        pos = jnp.zeros((1,), jnp.int32)
        g2p = jnp.zeros((batch, 1, D_MODEL), F32)
        ys = jnp.zeros((ROUTE_ROWS * TOK_SUB, 128), F32)

    def per_b(shape):
        return pl.BlockSpec(shape, lambda b, t, pos: (b,) + (0,) * (len(shape) - 1))

    def const(shape):
        return pl.BlockSpec(shape, lambda b, t, pos: (0,) * len(shape))

    any_spec = pl.BlockSpec(memory_space=pl.ANY)
    mod_spec = per_b((bb_n, 1, D_MODEL))
    state_spec = per_b((bb_n, N_HEADS, D_HEAD, D_HEAD))
    rope_spec = pl.BlockSpec((tq, D_HEAD), lambda b, t, pos: (t, 0))
    operands = [
        (x, pl.BlockSpec((bb_n, tq, D_MODEL), lambda b, t, pos: (b, t, 0))),
        (g2p, mod_spec), (ys, any_spec),
        (sh1, mod_spec), (sc1, mod_spec), (g1, mod_spec), (sh2, mod_spec), (sc2, mod_spec),
        (lw["na"], const((1, D_MODEL))), (lw["nm"], const((1, D_MODEL))),
        (lw["w_in"], const((D_MODEL, IN_COLS))), (lw["w_out"], const((2 * G_WIDTH, D_MODEL))),
        (lw["lb"], const((1, G_WIDTH))), (lw["ga"], const((1, G_WIDTH))), (lw["gr"], const((1, G_WIDTH))),
        (consts["wrt"], const((N_EXPERTS, D_MODEL))), (consts["br"], const((N_EXPERTS, 1))),
        (cos, rope_spec), (sin, rope_spec),
        (consts["intra"], const((N_HEADS, CHUNK, CHUNK))), (consts["inter"], const((N_HEADS, CHUNK, D_HEAD))),
        (consts["kdec"], const((N_HEADS, CHUNK, D_HEAD))), (consts["sdec"], const((N_HEADS, 1, D_HEAD))),
        (consts["tri"], const((CHUNK, 3 * CHUNK))), (consts["triu"], const((ROW_TILE, ROW_TILE))),
        (consts["lstrict"], const((COMBO_ROWS, COMBO_ROWS))),
        (s_a0, any_spec), (s_r0, any_spec),
        (dsp, const(dsp.shape)), (bc, const(bc.shape)), (xs, any_spec),
    ]
    out_specs = [
        pl.BlockSpec((bb_n, tq, D_MODEL), lambda b, t, pos: (b, t, 0)),
        pl.BlockSpec((1, ROUTE_ROWS, ROW_TILE), lambda b, t, pos: (b * nt + t, 0, 0)),
        state_spec, state_spec,
        const(dsp.shape), const(bc.shape), any_spec,
    ]
    out_shape = [
        jax.ShapeDtypeStruct((batch, seq, D_MODEL), F32),
        jax.ShapeDtypeStruct((nb * nt, ROUTE_ROWS, ROW_TILE), jnp.int32),
        jax.ShapeDtypeStruct((batch, N_HEADS, D_HEAD, D_HEAD), F32),
        jax.ShapeDtypeStruct((batch, N_HEADS, D_HEAD, D_HEAD), F32),
        jax.ShapeDtypeStruct(dsp.shape, F32),
        jax.ShapeDtypeStruct(bc.shape, F32),
        jax.ShapeDtypeStruct(xs.shape, xs.dtype),
    ]
    gather_rows = ROW_TILE if gather else ROUTE_ROWS
    x1, pos3, s_a, s_r, dsp, bc, xs = pl.pallas_call(
        functools.partial(_mixer_kernel, bb_n=bb_n, tq=tq, gather=gather, dummy_row=xs.shape[0] // TOK_SUB - ROW_TILE),
        grid_spec=pltpu.PrefetchScalarGridSpec(
            num_scalar_prefetch=1,
            grid=(nb, nt),
            in_specs=[spec for _, spec in operands],
            out_specs=out_specs,
            scratch_shapes=[
                pltpu.VMEM((ROW_TILE, IN_COLS), F32),
                pltpu.VMEM((ROW_TILE, 2 * G_WIDTH), BF16),
                pltpu.VMEM((bb_n, N_HEADS, D_HEAD, D_HEAD), F32),
                pltpu.VMEM((2, gather_rows * TOK_SUB, 128), F32),
                pltpu.VMEM((ROW_TILE * TOK_SUB, 128), jnp.uint32),
                pltpu.VMEM((ROUTE_ROWS, ROW_TILE), jnp.int32),
                pltpu.SMEM((ROW_TILE,), jnp.int32),
                pltpu.SemaphoreType.DMA((2,)),
                pltpu.SemaphoreType.DMA((2,)),
                pltpu.SemaphoreType.DMA,
                pltpu.SemaphoreType.DMA,
            ],
        ),
        out_shape=out_shape,
        input_output_aliases={len(operands): 6},
        compiler_params=pltpu.CompilerParams(
            dimension_semantics=("arbitrary", "arbitrary"), vmem_limit_bytes=VMEM_LIMIT),
        name="mixer",
    )(pos, *[arr for arr, _ in operands])
    return x1, pos3[:, 0, :].reshape(batch * seq), s_a, s_r, (dsp, bc, xs)


def _rope_tables(seq, offset):
    half = D_HEAD // 2
    freq = ROPE_BASE ** (-jnp.arange(half, dtype=F32) / half)
    ang = (jnp.arange(seq, dtype=F32) + offset)[:, None] * freq[None, :]
    cos, sin = jnp.cos(ang), jnp.sin(ang)
    return jnp.concatenate([cos, cos], axis=-1), jnp.concatenate([-sin, sin], axis=-1)


def _retention_tables():
    log_g = jnp.log(1.0 - 2.0 ** (-5.0 - jnp.arange(N_HEADS, dtype=F32)))
    idx = jnp.arange(CHUNK, dtype=F32)
    rel = idx[:, None] - idx[None, :]
    intra = jnp.where(rel >= 0, jnp.exp(jnp.maximum(rel, 0.0)[None] * log_g[:, None, None]), 0.0)
    inter = jnp.exp((idx[None, :] + 1.0) * log_g[:, None])[..., None]
    kdec = jnp.exp((CHUNK - 1.0 - idx[None, :]) * log_g[:, None])[..., None]
    sdec = jnp.exp(CHUNK * log_g)[:, None, None]
    wide = (N_HEADS, CHUNK, D_HEAD)
    return (intra, jnp.broadcast_to(inter, wide), jnp.broadcast_to(kdec, wide),
            jnp.broadcast_to(sdec, (N_HEADS, 1, D_HEAD)))


def _tok_copy(src_ref, src_tok, dst_ref, dst_tok, sem):
    src = src_ref.at[pl.ds(pl.multiple_of(src_tok * TOK_SUB, TOK_SUB), TOK_SUB), :]
    dst = dst_ref.at[pl.ds(pl.multiple_of(dst_tok * TOK_SUB, TOK_SUB), TOK_SUB), :]
    return pltpu.make_async_copy(src, dst, sem)


def _tok_columns(ref, n_tok, n_sub, lead=()):
    return jnp.concatenate([ref[lead + (pl.ds(s, n_tok, stride=TOK_SUB), slice(None))] for s in range(n_sub)], axis=1)


def _moe_kernel(order_ref, ea_ref, eb_ref, nu_ref, xs_ref, w1a, w3a, w2a, w1b, w3b, w2b, ys_ref):
    del order_ref, ea_ref, eb_ref
    j = pl.program_id(0)

    @pl.when(j < nu_ref[0])
    def _():
        words = _tok_columns(xs_ref, MOE_BLOCK, PACK_SUB)
        lo = lax.bitcast_convert_type(words << 16, F32)
        hi = lax.bitcast_convert_type(words & jnp.uint32(0xFFFF0000), F32)
        x = jnp.concatenate([lo, hi], axis=1).astype(BF16)
        rec = lax.bitcast_convert_type(xs_ref[pl.ds(PACK_SUB, MOE_BLOCK, stride=TOK_SUB), :], F32)
        w_lo = rec[:, 1:2]
        w_hi = rec[:, 2:3]

        def ffn(w1, w3, w2):
            h1 = _dot(x, w1[0])
            hid = (h1 * jax.nn.sigmoid(h1)) * _dot(x, w3[0])
            return _dot(hid.astype(BF16), w2[0])

        y = ffn(w1a, w3a, w2a) * w_lo + ffn(w1b, w3b, w2b) * w_hi
        for s in range(TOK_SUB):
            ys_ref[pl.ds(s, MOE_BLOCK, stride=TOK_SUB), :] = y[:, s * 128:(s + 1) * 128]

    @pl.when(j >= nu_ref[0])
    def _():
        ys_ref[...] = jnp.zeros_like(ys_ref)


def _moe(order, ea, eb, n_used, xs, w1, w3, w2, layer):
    n_blocks = order.shape[0]
    blk_rows = MOE_BLOCK * TOK_SUB

    def w_spec(which):
        if which == 0:
            return pl.BlockSpec((None, 1, D_MODEL, D_MODEL), lambda j, order, ea, eb, nu: (layer, ea[j], 0, 0))
        return pl.BlockSpec((None, 1, D_MODEL, D_MODEL), lambda j, order, ea, eb, nu: (layer, eb[j], 0, 0))

    return pl.pallas_call(
        _moe_kernel,
        grid_spec=pltpu.PrefetchScalarGridSpec(
            num_scalar_prefetch=4,
            grid=(n_blocks,),
            in_specs=[
                pl.BlockSpec((blk_rows, 128), lambda j, order, ea, eb, nu: (order[j], 0)),
                w_spec(0), w_spec(0), w_spec(0), w_spec(1), w_spec(1), w_spec(1),
            ],
            out_specs=pl.BlockSpec((blk_rows, 128), lambda j, order, ea, eb, nu: (order[j], 0)),
        ),
        out_shape=jax.ShapeDtypeStruct((n_blocks * blk_rows, 128), F32),
        compiler_params=pltpu.CompilerParams(dimension_semantics=("arbitrary",), vmem_limit_bytes=VMEM_LIMIT),
        name="moe",
    )(order, ea, eb, n_used, xs, w1, w3, w2, w1, w3, w2)


def _combine_kernel(pos_ref, x1_ref, g2_ref, nf_ref, ys_ref, out_ref, buf_ref, sem):
    step = pl.program_id(0) * pl.num_programs(1) + pl.program_id(1)
    last_step = pl.num_programs(0) * pl.num_programs(1) - 1
    slot = step % 2

    def fetch_tile(tile, dst_slot):
        def issue(g, carry):
            r0 = pl.multiple_of(g * ISSUE_UNROLL, ISSUE_UNROLL)
            for j in range(ISSUE_UNROLL):
                _tok_copy(ys_ref, pos_ref[tile * ROW_TILE + r0 + j], buf_ref.at[dst_slot], r0 + j,
                          sem.at[dst_slot]).start(priority=j % 2)
            return carry
        lax.fori_loop(0, ROW_TILE // ISSUE_UNROLL, issue, 0)

    def wait_tile(dst_slot):
        pltpu.make_async_copy(ys_ref.at[pl.ds(0, ROW_TILE * TOK_SUB), :], buf_ref.at[dst_slot],
                              sem.at[dst_slot]).wait()

    @pl.when(step == 0)
    def _():
        fetch_tile(0, 0)

    fetch_tile(jnp.minimum(step + 1, last_step), 1 - slot)
    wait_tile(slot)
    y = _tok_columns(buf_ref, ROW_TILE, TOK_SUB, (slot,))
    x2 = x1_ref[...] + g2_ref[...] * y.reshape(x1_ref.shape)
    out_ref[...] = x2 * lax.rsqrt(jnp.mean(x2 * x2, axis=-1, keepdims=True) + EPS) * nf_ref[...]

    @pl.when(step == last_step)
    def _():
        wait_tile(1 - slot)


def _combine(pos, x1, g2, nf, ys):
    batch, seq, _ = x1.shape
    bb_n, tq = _tiling(batch, seq)
    tok_spec = pl.BlockSpec((bb_n, tq, D_MODEL), lambda b, t, pos: (b, t, 0))
    return pl.pallas_call(
        _combine_kernel,
        grid_spec=pltpu.PrefetchScalarGridSpec(
            num_scalar_prefetch=1,
            grid=(batch // bb_n, seq // tq),
            in_specs=[
                tok_spec,
                pl.BlockSpec((bb_n, 1, D_MODEL), lambda b, t, pos: (b, 0, 0)),
                pl.BlockSpec((1, D_MODEL), lambda b, t, pos: (0, 0)),
                pl.BlockSpec(memory_space=pl.ANY),
            ],
            out_specs=tok_spec,
            scratch_shapes=[pltpu.VMEM((2, ROW_TILE * TOK_SUB, 128), F32), pltpu.SemaphoreType.DMA((2,))],
        ),
        out_shape=jax.ShapeDtypeStruct(x1.shape, F32),
        compiler_params=pltpu.CompilerParams(dimension_semantics=("arbitrary", "arbitrary")),
        name="combine",
    )(pos, x1, g2, nf, ys)


def _block_order(dsp, bc, n_blocks):
    n_used = dsp[2, 0, 0].astype(jnp.int32)
    blk = jnp.arange(n_blocks, dtype=jnp.int32)
    owner = jnp.where(blk < n_used, bc[0, :n_blocks].astype(jnp.int32), N_COMBO)
    order = jnp.argsort(owner, stable=True).astype(jnp.int32)
    combo = jnp.minimum(owner[order], N_COMBO - 1)
    last = combo[jnp.maximum(n_used - 1, 0)]
    combo = jnp.where(blk < n_used, combo, last)
    lo_tab = jnp.array([0, 0, 0, 1, 1, 2], jnp.int32)
    hi_tab = jnp.array([1, 2, 3, 2, 3, 3], jnp.int32)
    grp = combo // PAIRS_PER_GROUP
    pair = combo % PAIRS_PER_GROUP
    return order, grp * EXP_PER_GROUP + lo_tab[pair], grp * EXP_PER_GROUP + hi_tab[pair], n_used.reshape(1)


def kernel(x_prompt, x_sample, state_hgrn, state_ret, c_prompt, c_sample, w_in, w_out, lb_logits, hgrn_norm,
           ret_norm, norm_attn, norm_moe, w_ada, b_ada, w_router, b_router, w1, w3, w2, norm_f):
    bp, tp, _ = x_prompt.shape
    bs, ts, _ = x_sample.shape
    n_p, n_s = bp * tp, bs * ts
    n_tok = n_p + n_s
    assert n_tok % MOE_BLOCK == 0
    n_blocks = n_tok // MOE_BLOCK + N_COMBO
    assert n_blocks <= BLOCK_LANES

    lb_cum = jnp.cumsum(jax.nn.softmax(lb_logits.astype(F32), axis=0), axis=0)
    lb_all = lb_cum - lb_cum[0:1]

    mod = _ada(jnp.concatenate([c_prompt, c_sample], axis=0), w_ada, b_ada)

    intra, inter, kdec, sdec = _retention_tables()
    consts = {
        "wrt": w_router.T.astype(BF16), "br": b_router.astype(F32).reshape(N_EXPERTS, 1),
        "intra": intra, "inter": inter, "kdec": kdec, "sdec": sdec,
        "tri": jnp.tile(jnp.tril(jnp.ones((CHUNK, CHUNK), BF16)), (1, 3)),
        "triu": jnp.triu(jnp.ones((ROW_TILE, ROW_TILE), BF16)),
        "lstrict": jnp.tril(jnp.ones((COMBO_ROWS, COMBO_ROWS), BF16), k=-1),
    }

    w1_b, w3_b, w2_b = w1.astype(BF16), w3.astype(BF16), w2.astype(BF16)

    xp, xs_ = x_prompt, x_sample
    sa_p = jnp.zeros((bp, N_HEADS, D_HEAD, D_HEAD), F32)
    sr_p = jnp.zeros((bp, N_HEADS, D_HEAD, D_HEAD), F32)
    new_states = []
    prev_p = prev_s = None
    for l in range(DEPTH):
        lw = {
            "na": norm_attn[l].reshape(1, D_MODEL), "nm": norm_moe[l].reshape(1, D_MODEL),
            "w_in": w_in[l].astype(BF16), "w_out": w_out[l].astype(BF16),
            "lb": lb_all[l].reshape(1, G_WIDTH), "ga": hgrn_norm[l].reshape(1, G_WIDTH),
            "gr": ret_norm[l].reshape(1, G_WIDTH),
        }
        parts = jnp.split(mod[l], 6, axis=-1)
        mods_p = [p[:bp, None, :] for p in parts]
        mods_s = [p[bp:, None, :] for p in parts]
        dsp0 = jnp.stack([jnp.zeros((COMBO_ROWS, 128), F32), jnp.full((COMBO_ROWS, 128), -1.0, F32),
                          jnp.zeros((COMBO_ROWS, 128), F32)])
        dispatch = (dsp0, jnp.zeros((ROUTE_ROWS, BLOCK_LANES), F32),
                    jnp.zeros(((n_blocks * MOE_BLOCK + ROW_TILE) * TOK_SUB, 128), jnp.uint32))
        x1p, pos_p, sa_np, sr_np, dispatch = _mixer(xp, mods_p[:5], consts, lw, sa_p, sr_p, 0, dispatch, prev_p)
        x1s, pos_s, sa_ns, sr_ns, dispatch = _mixer(xs_, mods_s[:5], consts, lw, state_hgrn[l].astype(F32),
                                                    state_ret[l].astype(F32), PAST_LEN, dispatch, prev_s)
        new_states.append((sa_np, sr_np, sa_ns, sr_ns))

        dsp, bc, xs = dispatch
        order, ea, eb, n_used = _block_order(dsp, bc, n_blocks)
        ys = _moe(order, ea, eb, n_used, xs, w1_b, w3_b, w2_b, l)

        xp, xs_ = x1p, x1s
        prev_p = (pos_p, mods_p[5], ys)
        prev_s = (pos_s, mods_s[5], ys)

    nf = norm_f.reshape(1, D_MODEL)
    xp = _combine(prev_p[0], xp, prev_p[1], nf, prev_p[2])
    xs_ = _combine(prev_s[0], xs_, prev_s[1], nf, prev_s[2])

    sa_prompt = jnp.stack([s[0] for s in new_states])
    sr_prompt = jnp.stack([s[1] for s in new_states])
    sa_sample = jnp.stack([s[2] for s in new_states])
    sr_sample = jnp.stack([s[3] for s in new_states])
    return (xp, xs_, sa_prompt, sr_prompt, sa_sample, sr_sample)
```

```python
import functools

import jax
import jax.numpy as jnp
from jax import lax
from jax.experimental import pallas as pl
from jax.experimental.pallas import tpu as pltpu

F32 = jnp.float32
BF16 = jnp.bfloat16

D_MODEL = 1024
DEPTH = 2
N_HEADS = 4
D_HEAD = 128
G_WIDTH = N_HEADS * D_HEAD
IN_COLS = 8 * G_WIDTH
CHUNK = 64
CHUNK_UNROLL = 2
STAGE_GAP = (None,) * 3
SUB = 8
LEVELS = (32, 16, 8)
N_EXPERTS = 16
EXP_PER_GROUP = 4
N_GROUPS = 4
PAIRS_PER_GROUP = 6
N_COMBO = N_GROUPS * PAIRS_PER_GROUP
COMBO_ROWS = 32
ROUTE_ROWS = 8
TOK_SUB = 8
PACK_SUB = 4
ROW_TILE = 512
ISSUE_UNROLL = 8
MOE_BLOCK = 256
BLOCK_LANES = 384
ADA_TILE = 1536
ROPE_BASE = 10000.0
PAST_LEN = 1024
EPS = 1e-6
LOG2E = 1.4426950408889634
VMEM_LIMIT = 56 * 1024 * 1024

_NT = (((1,), (1,)), ((), ()))
_TN = (((0,), (0,)), ((), ()))


def _dot(a, b):
    return jnp.dot(a, b, preferred_element_type=F32)


def _dot_nt(a, b):
    return lax.dot_general(a, b, _NT, preferred_element_type=F32)


def _dot_tn(a, b):
    return lax.dot_general(a, b, _TN, preferred_element_type=F32)


def _tiling(batch, seq):
    tq = min(seq, ROW_TILE)
    bb = ROW_TILE // tq
    assert tq % CHUNK == 0 and bb * tq == ROW_TILE and batch % bb == 0 and seq % tq == 0
    return bb, tq


def _ada_kernel(c_ref, w_ref, b_ref, o_ref):
    c = c_ref[...]
    cs = c * jax.nn.sigmoid(c)
    o_ref[0] = _dot(cs.astype(BF16), w_ref[0].astype(BF16)) + b_ref[0]


def _ada(c_all, w_ada, b_ada):
    m = c_all.shape[0]
    n = w_ada.shape[-1]
    return pl.pallas_call(
        _ada_kernel,
        grid=(DEPTH, n // ADA_TILE),
        in_specs=[
            pl.BlockSpec((m, D_MODEL), lambda l, j: (0, 0)),
            pl.BlockSpec((1, D_MODEL, ADA_TILE), lambda l, j: (l, 0, j)),
            pl.BlockSpec((1, 1, ADA_TILE), lambda l, j: (l, 0, j)),
        ],
        out_specs=pl.BlockSpec((1, m, ADA_TILE), lambda l, j: (l, 0, j)),
        out_shape=jax.ShapeDtypeStruct((DEPTH, m, n), F32),
        compiler_params=pltpu.CompilerParams(vmem_limit_bytes=VMEM_LIMIT),
        name="ada",
    )(c_all, w_ada, b_ada.reshape(DEPTH, 1, n))


def _route(lt, br):
    sc = jax.nn.sigmoid(lt)
    bi = sc + br
    s = [sc[e:e + 1] for e in range(N_EXPERTS)]
    v = [bi[e:e + 1] for e in range(N_EXPERTS)]
    gs = []
    for g in range(N_GROUPS):
        a = v[EXP_PER_GROUP * g:EXP_PER_GROUP * (g + 1)]
        m = a[0] + a[1]
        for i in range(EXP_PER_GROUP):
            for j in range(i + 1, EXP_PER_GROUP):
                if (i, j) != (0, 1):
                    m = jnp.maximum(m, a[i] + a[j])
        gs.append(m)
    best = gs[0]
    gsel = jnp.zeros_like(best)
    for g in range(1, N_GROUPS):
        upd = gs[g] > best
        best = jnp.where(upd, gs[g], best)
        gsel = jnp.where(upd, float(g), gsel)

    def pick(vals, key, n):
        out = vals[0]
        for g in range(1, n):
            out = jnp.where(key == float(g), vals[g], out)
        return out

    vv = [pick([v[EXP_PER_GROUP * g + j] for g in range(N_GROUPS)], gsel, N_GROUPS) for j in range(EXP_PER_GROUP)]
    ss = [pick([s[EXP_PER_GROUP * g + j] for g in range(N_GROUPS)], gsel, N_GROUPS) for j in range(EXP_PER_GROUP)]
    m1 = vv[0]
    i1 = jnp.zeros_like(m1)
    for j in range(1, EXP_PER_GROUP):
        upd = vv[j] > m1
        m1 = jnp.where(upd, vv[j], m1)
        i1 = jnp.where(upd, float(j), i1)
    m2 = jnp.full_like(m1, -jnp.inf)
    i2 = jnp.zeros_like(m1)
    for j in range(EXP_PER_GROUP):
        upd = (i1 != float(j)) & (vv[j] > m2)
        m2 = jnp.where(upd, vv[j], m2)
        i2 = jnp.where(upd, float(j), i2)
    s1 = pick(ss, i1, EXP_PER_GROUP)
    s2 = pick(ss, i2, EXP_PER_GROUP)
    den = s1 + s2
    w1 = s1 / den
    w2 = s2 / den
    first_lo = i1 < i2
    lo = jnp.where(first_lo, i1, i2)
    hi = jnp.where(first_lo, i2, i1)
    w_lo = jnp.where(first_lo, w1, w2)
    w_hi = jnp.where(first_lo, w2, w1)
    pair = jnp.where(lo == 0.0, hi - 1.0, jnp.where(lo == 1.0, hi + 1.0, 5.0))
    combo = gsel * float(PAIRS_PER_GROUP) + pair
    zero = jnp.zeros_like(combo)
    return jnp.concatenate([combo, w_lo, w_hi] + [zero] * (ROUTE_ROWS - 3), axis=0)


def _mixer_kernel(pos_ref, x_ref, g2p_ref, ys_ref, sh1_ref, sc1_ref, g1_ref, sh2_ref, sc2_ref, na_ref, nm_ref,
                  win_ref, wout_ref, lb_ref, ga_ref, gr_ref, wrt_ref, br_ref, cos_ref, sin_ref, intra_ref,
                  inter_ref, kdec_ref, sdec_ref, tri_ref, triu_ref, lstrict_ref, sa0_ref, sr0_ref,
                  dsp0_ref, bc0_ref, xs_in_ref,
                  x1_ref, pos3_ref, sa_ref, sr_ref, dsp_ref, bc_ref, xs_ref,
                  proj_ref, ob_ref, sat_ref, gbuf_ref, hbuf_ref, posv_ref, poss_ref, gsem, ssem, psem, scsem,
                  *, bb_n, tq, gather, dummy_row):
    del xs_in_ref
    tt = pl.program_id(1)
    rows = bb_n * tq
    n_chunks = tq // CHUNK
    n_iter = bb_n * n_chunks // CHUNK_UNROLL
    step = pl.program_id(0) * pl.num_programs(1) + tt
    last_step = pl.num_programs(0) * pl.num_programs(1) - 1

    x = x_ref[...]
    if gather:
        slot = step % 2

        def fetch(tile, dst_slot, r, priority):
            _tok_copy(ys_ref, pos_ref[tile * ROW_TILE + r], gbuf_ref.at[dst_slot], r,
                      gsem.at[dst_slot]).start(priority=priority)

        @pl.when(step == 0)
        def _():
            def issue(g, carry):
                r0 = pl.multiple_of(g * ISSUE_UNROLL, ISSUE_UNROLL)
                for j in range(ISSUE_UNROLL):
                    fetch(0, 0, r0 + j, j % 2)
                return carry
            lax.fori_loop(0, ROW_TILE // ISSUE_UNROLL, issue, 0)

        pltpu.make_async_copy(ys_ref.at[pl.ds(0, ROW_TILE * TOK_SUB), :], gbuf_ref.at[slot], gsem.at[slot]).wait()
        y_prev = _tok_columns(gbuf_ref, ROW_TILE, TOK_SUB, (slot,))
        x = x + g2p_ref[...] * y_prev.reshape(bb_n, tq, D_MODEL)

    @pl.when(tt == 0)
    def _():
        b0 = pl.program_id(0) * bb_n
        load_a = pltpu.make_async_copy(sa0_ref.at[pl.ds(b0, bb_n)], sa_ref, ssem.at[0])
        load_r = pltpu.make_async_copy(sr0_ref.at[pl.ds(b0, bb_n)], sr_ref, ssem.at[1])
        load_a.start()
        load_r.start()
        load_a.wait()
        load_r.wait()
        for b in range(bb_n):
            for hh in range(N_HEADS):
                sat_ref[b, hh] = sa_ref[b, hh].T

    @pl.when(step == 0)
    def _():
        dsp_ref[...] = dsp0_ref[...]
        bc_ref[...] = bc0_ref[...]
        hbuf_ref[...] = jnp.zeros_like(hbuf_ref)
        posv_ref[...] = jnp.zeros_like(posv_ref)
        init = pltpu.make_async_copy(posv_ref.at[0], poss_ref, psem)
        init.start()
        init.wait()

    def put(r, priority):
        dst = jnp.where(step > 0, poss_ref[r], dummy_row + r)
        _tok_copy(hbuf_ref, r, xs_ref, dst, scsem).start(priority=priority)

    for r in range(ROW_TILE):
        put(r, r % 2)
    if gather:
        next_tile = jnp.minimum(step + 1, last_step)
        for r in range(ROW_TILE):
            fetch(next_tile, 1 - slot, r, (r + 1) % 2)
    ms = jnp.mean(x * x, axis=-1, keepdims=True)
    h = x * lax.rsqrt(ms + EPS) * na_ref[...] * (1.0 + sc1_ref[...]) + sh1_ref[...]
    proj_ref[...] = _dot(h.reshape(rows, D_MODEL).astype(BF16), win_ref[...])

    lb = lb_ref[...]
    ln_1mlb = jnp.log1p(-lb)
    ga = ga_ref[...]
    gr = gr_ref[...]
    tri3 = tri_ref[...]
    t_i = lax.broadcasted_iota(jnp.int32, (CHUNK, CHUNK), 0)
    s_i = lax.broadcasted_iota(jnp.int32, (CHUNK, CHUNK), 1)
    level_masks = []
    for m in LEVELS:
        sh = m.bit_length() - 1
        level_masks.append(((t_i >> (sh + 1)) == (s_i >> (sh + 1)))
                           & (((t_i >> sh) & 1) == 1) & (((s_i >> sh) & 1) == 0))
    diag_mask = ((t_i >> 3) == (s_i >> 3)) & (s_i <= t_i)
    col_in_sub = s_i & (SUB - 1)

    def col(group, hh):
        return slice(group * G_WIDTH + hh * D_HEAD, group * G_WIDTH + (hh + 1) * D_HEAD)

    def hgrn_head(b, rs, hh):
        sl = slice(hh * D_HEAD, (hh + 1) * D_HEAD)
        z = proj_ref[rs, col(1, hh)]
        lbh = lb[:, sl]
        u = jnp.exp(-jnp.abs(z))
        ln_1pu = jnp.log(1.0 + u)
        num = jnp.where(z >= 0.0, 1.0 + lbh * u, lbh + u)
        l2f = (jnp.where(num > 0.0, jnp.log(num), z) - ln_1pu) * LOG2E
        l2k = (ln_1mlb[:, sl] - jnp.maximum(z, 0.0) - ln_1pu) * LOG2E
        hi = l2f.astype(BF16)
        rem = l2f - hi.astype(F32)
        mid = rem.astype(BF16)
        lo = (rem - mid.astype(F32)).astype(BF16)
        bcum = _dot(tri3, jnp.concatenate([hi, mid, lo], axis=0))
        yield from STAGE_GAP
        q = proj_ref[rs, col(0, hh)]
        vb = proj_ref[rs, col(2, hh)].astype(BF16)
        b_last = bcum[CHUNK - 1:CHUNK, :]
        q_in = (q * jnp.exp2(bcum)).astype(BF16)
        k_out = jnp.exp2(l2k + (b_last - bcum)).astype(BF16)
        upd = _dot_tn(vb, k_out)
        lvl = []
        for m in LEVELS:
            ref_row = bcum.reshape(CHUNK // (2 * m), 2 * m, D_HEAD)[:, m - 1:m, :]
            ref_row = jnp.broadcast_to(ref_row, (CHUNK // (2 * m), 2 * m, D_HEAD)).reshape(CHUNK, D_HEAD)
            dist = jnp.abs(bcum - ref_row)
            lvl.append(_dot_nt((q * jnp.exp2(-dist)).astype(BF16), jnp.exp2(l2k - dist).astype(BF16)))
        src3 = (l2k - bcum).reshape(CHUNK // SUB, SUB, D_HEAD)
        diag = jnp.zeros((CHUNK, CHUNK), F32)
        for sg in range(SUB):
            src = jnp.broadcast_to(src3[:, sg:sg + 1, :], src3.shape).reshape(CHUNK, D_HEAD)
            row_sum = jnp.sum(q * jnp.exp2(jnp.minimum(bcum + src, 0.0)), axis=-1, keepdims=True)
            diag = jnp.where(col_in_sub == sg, row_sum, diag)
        yield from STAGE_GAP
        a = jnp.where(diag_mask, diag, 0.0)
        for li in range(len(LEVELS)):
            a = jnp.where(level_masks[li], lvl[li], a)
        st = sat_ref[b, hh]
        o = _dot(a.astype(BF16), vb) + _dot_nt(q_in, st.astype(BF16))
        sat_ref[b, hh] = st * jnp.exp2(b_last) + upd
        yield from STAGE_GAP
        o = o * lax.rsqrt(jnp.mean(o * o, axis=-1, keepdims=True) + EPS)
        o = o * ga[:, sl] * jax.nn.sigmoid(proj_ref[rs, col(3, hh)])
        ob_ref[rs, sl] = o.astype(BF16)

    def ret_head(b, rs, t0, hh):
        sl = slice(hh * D_HEAD, (hh + 1) * D_HEAD)
        cs = cos_ref[pl.ds(t0, CHUNK), :]
        sn = sin_ref[pl.ds(t0, CHUNK), :]
        qh = proj_ref[rs, col(4, hh)]
        kh = proj_ref[rs, col(5, hh)]
        qr = qh * cs + pltpu.roll(qh, D_HEAD // 2, 1) * sn
        kr = (kh * cs + pltpu.roll(kh, D_HEAD // 2, 1) * sn) * (D_HEAD ** -0.5)
        qb = qr.astype(BF16)
        vh = proj_ref[rs, col(6, hh)].astype(BF16)
        sc = _dot_nt(qb, kr.astype(BF16))
        upd = _dot_tn((kr * kdec_ref[hh]).astype(BF16), vh)
        yield from STAGE_GAP
        s_r = sr_ref[b, hh]
        o = _dot((sc * intra_ref[hh]).astype(BF16), vh) + _dot(qb, s_r.astype(BF16)) * inter_ref[hh]
        sr_ref[b, hh] = s_r * sdec_ref[hh] + upd
        yield from STAGE_GAP
        mu = jnp.mean(o, axis=-1, keepdims=True)
        oc = o - mu
        var = jnp.mean(oc * oc, axis=-1, keepdims=True)
        o = oc * lax.rsqrt(var + EPS) * gr[:, sl]
        g = proj_ref[rs, col(7, hh)]
        o = o * (g * jax.nn.sigmoid(g))
        ob_ref[rs, col(1, hh)] = o.astype(BF16)

    def chunk_body(i, carry):
        heads = []
        for j in range(CHUNK_UNROLL):
            idx = i * CHUNK_UNROLL + j
            b = idx // n_chunks
            rs = pl.ds(pl.multiple_of(idx * CHUNK, CHUNK), CHUNK)
            t0 = pl.multiple_of((idx - b * n_chunks) * CHUNK, CHUNK)
            for hh in range(N_HEADS):
                heads.append(hgrn_head(b, rs, hh))
                heads.append(ret_head(b, rs, t0, hh))
        live = []
        while heads or live:
            if heads:
                live.append(heads.pop(0))
            nxt = []
            for h in live:
                try:
                    next(h)
                    nxt.append(h)
                except StopIteration:
                    pass
            live = nxt
        return carry

    lax.fori_loop(0, n_iter, chunk_body, 0)
    pltpu.make_async_copy(hbuf_ref, xs_ref.at[pl.ds(0, ROW_TILE * TOK_SUB), :], scsem).wait()

    @pl.when(tt == pl.num_programs(1) - 1)
    def _():
        for b in range(bb_n):
            for hh in range(N_HEADS):
                sa_ref[b, hh] = sat_ref[b, hh].T

    out = _dot(ob_ref[...], wout_ref[...]).reshape(bb_n, tq, D_MODEL)
    x1 = x + g1_ref[...] * out
    x1_ref[...] = x1
    ms2 = jnp.mean(x1 * x1, axis=-1, keepdims=True)
    h2 = x1 * lax.rsqrt(ms2 + EPS) * nm_ref[...] * (1.0 + sc2_ref[...]) + sh2_ref[...]
    h2b = h2.reshape(rows, D_MODEL).astype(BF16)
    lt = _dot_nt(wrt_ref[...], h2b)
    route = _route(lt, br_ref[...])
    bits = lax.bitcast_convert_type(h2b.astype(F32), jnp.uint32)
    words = (bits[:, 0:D_MODEL // 2] >> 16) | (bits[:, D_MODEL // 2:D_MODEL] & jnp.uint32(0xFFFF0000))
    for s in range(PACK_SUB):
        hbuf_ref[pl.ds(s, ROW_TILE, stride=TOK_SUB), :] = words[:, s * 128:(s + 1) * 128]
    rec = jnp.concatenate([route, jnp.zeros((128 - ROUTE_ROWS, rows), F32)], axis=0).T
    hbuf_ref[pl.ds(PACK_SUB, ROW_TILE, stride=TOK_SUB), :] = lax.bitcast_convert_type(rec, jnp.uint32)

    combo = route[0:1, :]
    ids = lax.broadcasted_iota(jnp.int32, (COMBO_ROWS, ROW_TILE), 0).astype(F32)
    onehot = (ids == combo).astype(F32)
    cum = _dot(onehot.astype(BF16), triu_ref[...])
    total = dsp_ref[0][:, 0:1]
    cur = dsp_ref[1][:, 0:1]
    n_alloc = dsp_ref[2][:, 0:1]
    n_c = cum[:, ROW_TILE - 1:ROW_TILE]
    inv_blk = 1.0 / MOE_BLOCK
    before = jnp.floor((total + float(MOE_BLOCK - 1)) * inv_blk)
    after = jnp.floor((total + n_c + float(MOE_BLOCK - 1)) * inv_blk)
    new = after - before
    new_wide = jnp.broadcast_to(new, (COMBO_ROWS, 128))
    first_new = n_alloc + _dot(lstrict_ref[...], new_wide.astype(BF16))[:, 0:1]
    ordinal = total + cum - 1.0
    lblk = jnp.floor(ordinal * inv_blk)
    offs = ordinal - lblk * float(MOE_BLOCK)
    blk = jnp.where(lblk < before, cur, first_new + (lblk - before))
    pos = jnp.sum(onehot * (blk * float(MOE_BLOCK) + offs), axis=0, keepdims=True)
    pos_i = jnp.broadcast_to(pos, (ROUTE_ROWS, ROW_TILE)).astype(jnp.int32)
    pos3_ref[0] = pos_i
    posv_ref[...] = pos_i
    wide = (COMBO_ROWS, 128)
    dsp_ref[0] = jnp.broadcast_to(total + n_c, wide)
    dsp_ref[1] = jnp.broadcast_to(jnp.where(new > 0.0, first_new + new - 1.0, cur), wide)
    dsp_ref[2] = jnp.broadcast_to(n_alloc + jnp.sum(new, axis=0, keepdims=True), wide)
    blk_id = lax.broadcasted_iota(jnp.int32, (COMBO_ROWS, BLOCK_LANES), 1).astype(F32)
    pair_id = lax.broadcasted_iota(jnp.int32, (COMBO_ROWS, BLOCK_LANES), 0).astype(F32)
    fresh = (blk_id >= first_new) & (blk_id < first_new + new)
    owner = jnp.sum(jnp.where(fresh, pair_id, 0.0), axis=0, keepdims=True)
    bc_ref[...] += jnp.broadcast_to(owner, bc_ref.shape)
    to_smem = pltpu.make_async_copy(posv_ref.at[0], poss_ref, psem)
    to_smem.start()
    to_smem.wait()

    @pl.when(step == last_step)
    def _():
        def issue(g, carry):
            r0 = pl.multiple_of(g * ISSUE_UNROLL, ISSUE_UNROLL)
            for j in range(ISSUE_UNROLL):
                _tok_copy(hbuf_ref, r0 + j, xs_ref, poss_ref[r0 + j], scsem).start(priority=j % 2)
            return carry
        lax.fori_loop(0, ROW_TILE // ISSUE_UNROLL, issue, 0)
        pltpu.make_async_copy(hbuf_ref, xs_ref.at[pl.ds(0, ROW_TILE * TOK_SUB), :], scsem).wait()
        if gather:
            pltpu.make_async_copy(ys_ref.at[pl.ds(0, ROW_TILE * TOK_SUB), :], gbuf_ref.at[1 - slot],
                                  gsem.at[1 - slot]).wait()


def _mixer(x, mods, consts, lw, s_a0, s_r0, offset, dispatch, prev=None):
    batch, seq, _ = x.shape
    bb_n, tq = _tiling(batch, seq)
    nb, nt = batch // bb_n, seq // tq
    cos, sin = _rope_tables(seq, offset)
    sh1, sc1, g1, sh2, sc2 = mods
    dsp, bc, xs = dispatch
    gather = prev is not None
    if gather:
        pos, g2p, ys = prev
    else:
        pos = jnp.zeros((1,), jnp.int32)
        g2p = jnp.zeros((batch, 1, D_MODEL), F32)
        ys = jnp.zeros((ROUTE_ROWS * TOK_SUB, 128), F32)

    def per_b(shape):
        return pl.BlockSpec(shape, lambda b, t, pos: (b,) + (0,) * (len(shape) - 1))

    def const(shape):
        return pl.BlockSpec(shape, lambda b, t, pos: (0,) * len(shape))

    any_spec = pl.BlockSpec(memory_space=pl.ANY)
    mod_spec = per_b((bb_n, 1, D_MODEL))
    state_spec = per_b((bb_n, N_HEADS, D_HEAD, D_HEAD))
    rope_spec = pl.BlockSpec((tq, D_HEAD), lambda b, t, pos: (t, 0))
    operands = [
        (x, pl.BlockSpec((bb_n, tq, D_MODEL), lambda b, t, pos: (b, t, 0))),
        (g2p, mod_spec), (ys, any_spec),
        (sh1, mod_spec), (sc1, mod_spec), (g1, mod_spec), (sh2, mod_spec), (sc2, mod_spec),
        (lw["na"], const((1, D_MODEL))), (lw["nm"], const((1, D_MODEL))),
        (lw["w_in"], const((D_MODEL, IN_COLS))), (lw["w_out"], const((2 * G_WIDTH, D_MODEL))),
        (lw["lb"], const((1, G_WIDTH))), (lw["ga"], const((1, G_WIDTH))), (lw["gr"], const((1, G_WIDTH))),
        (consts["wrt"], const((N_EXPERTS, D_MODEL))), (consts["br"], const((N_EXPERTS, 1))),
        (cos, rope_spec), (sin, rope_spec),
        (consts["intra"], const((N_HEADS, CHUNK, CHUNK))), (consts["inter"], const((N_HEADS, CHUNK, D_HEAD))),
        (consts["kdec"], const((N_HEADS, CHUNK, D_HEAD))), (consts["sdec"], const((N_HEADS, 1, D_HEAD))),
        (consts["tri"], const((CHUNK, 3 * CHUNK))), (consts["triu"], const((ROW_TILE, ROW_TILE))),
        (consts["lstrict"], const((COMBO_ROWS, COMBO_ROWS))),
        (s_a0, any_spec), (s_r0, any_spec),
        (dsp, const(dsp.shape)), (bc, const(bc.shape)), (xs, any_spec),
    ]
    out_specs = [
        pl.BlockSpec((bb_n, tq, D_MODEL), lambda b, t, pos: (b, t, 0)),
        pl.BlockSpec((1, ROUTE_ROWS, ROW_TILE), lambda b, t, pos: (b * nt + t, 0, 0)),
        state_spec, state_spec,
        const(dsp.shape), const(bc.shape), any_spec,
    ]
    out_shape = [
        jax.ShapeDtypeStruct((batch, seq, D_MODEL), F32),
        jax.ShapeDtypeStruct((nb * nt, ROUTE_ROWS, ROW_TILE), jnp.int32),
        jax.ShapeDtypeStruct((batch, N_HEADS, D_HEAD, D_HEAD), F32),
        jax.ShapeDtypeStruct((batch, N_HEADS, D_HEAD, D_HEAD), F32),
        jax.ShapeDtypeStruct(dsp.shape, F32),
        jax.ShapeDtypeStruct(bc.shape, F32),
        jax.ShapeDtypeStruct(xs.shape, xs.dtype),
    ]
    gather_rows = ROW_TILE if gather else ROUTE_ROWS
    x1, pos3, s_a, s_r, dsp, bc, xs = pl.pallas_call(
        functools.partial(_mixer_kernel, bb_n=bb_n, tq=tq, gather=gather, dummy_row=xs.shape[0] // TOK_SUB - ROW_TILE),
        grid_spec=pltpu.PrefetchScalarGridSpec(
            num_scalar_prefetch=1,
            grid=(nb, nt),
            in_specs=[spec for _, spec in operands],
            out_specs=out_specs,
            scratch_shapes=[
                pltpu.VMEM((ROW_TILE, IN_COLS), F32),
                pltpu.VMEM((ROW_TILE, 2 * G_WIDTH), BF16),
                pltpu.VMEM((bb_n, N_HEADS, D_HEAD, D_HEAD), F32),
                pltpu.VMEM((2, gather_rows * TOK_SUB, 128), F32),
                pltpu.VMEM((ROW_TILE * TOK_SUB, 128), jnp.uint32),
                pltpu.VMEM((ROUTE_ROWS, ROW_TILE), jnp.int32),
                pltpu.SMEM((ROW_TILE,), jnp.int32),
                pltpu.SemaphoreType.DMA((2,)),
                pltpu.SemaphoreType.DMA((2,)),
                pltpu.SemaphoreType.DMA,
                pltpu.SemaphoreType.DMA,
            ],
        ),
        out_shape=out_shape,
        input_output_aliases={len(operands): 6},
        compiler_params=pltpu.CompilerParams(
            dimension_semantics=("arbitrary", "arbitrary"), vmem_limit_bytes=VMEM_LIMIT),
        name="mixer",
    )(pos, *[arr for arr, _ in operands])
    return x1, pos3[:, 0, :].reshape(batch * seq), s_a, s_r, (dsp, bc, xs)


def _rope_tables(seq, offset):
    half = D_HEAD // 2
    freq = ROPE_BASE ** (-jnp.arange(half, dtype=F32) / half)
    ang = (jnp.arange(seq, dtype=F32) + offset)[:, None] * freq[None, :]
    cos, sin = jnp.cos(ang), jnp.sin(ang)
    return jnp.concatenate([cos, cos], axis=-1), jnp.concatenate([-sin, sin], axis=-1)


def _retention_tables():
    log_g = jnp.log(1.0 - 2.0 ** (-5.0 - jnp.arange(N_HEADS, dtype=F32)))
    idx = jnp.arange(CHUNK, dtype=F32)
    rel = idx[:, None] - idx[None, :]
    intra = jnp.where(rel >= 0, jnp.exp(jnp.maximum(rel, 0.0)[None] * log_g[:, None, None]), 0.0)
    inter = jnp.exp((idx[None, :] + 1.0) * log_g[:, None])[..., None]
    kdec = jnp.exp((CHUNK - 1.0 - idx[None, :]) * log_g[:, None])[..., None]
    sdec = jnp.exp(CHUNK * log_g)[:, None, None]
    wide = (N_HEADS, CHUNK, D_HEAD)
    return (intra, jnp.broadcast_to(inter, wide), jnp.broadcast_to(kdec, wide),
            jnp.broadcast_to(sdec, (N_HEADS, 1, D_HEAD)))


def _tok_copy(src_ref, src_tok, dst_ref, dst_tok, sem):
    src = src_ref.at[pl.ds(pl.multiple_of(src_tok * TOK_SUB, TOK_SUB), TOK_SUB), :]
    dst = dst_ref.at[pl.ds(pl.multiple_of(dst_tok * TOK_SUB, TOK_SUB), TOK_SUB), :]
    return pltpu.make_async_copy(src, dst, sem)


def _tok_columns(ref, n_tok, n_sub, lead=()):
    return jnp.concatenate([ref[lead + (pl.ds(s, n_tok, stride=TOK_SUB), slice(None))] for s in range(n_sub)], axis=1)


def _moe_kernel(order_ref, ea_ref, eb_ref, nu_ref, xs_ref, w1a, w3a, w2a, w1b, w3b, w2b, ys_ref):
    del order_ref, ea_ref, eb_ref
    j = pl.program_id(0)

    @pl.when(j < nu_ref[0])
    def _():
        words = _tok_columns(xs_ref, MOE_BLOCK, PACK_SUB)
        lo = lax.bitcast_convert_type(words << 16, F32)
        hi = lax.bitcast_convert_type(words & jnp.uint32(0xFFFF0000), F32)
        x = jnp.concatenate([lo, hi], axis=1).astype(BF16)
        rec = lax.bitcast_convert_type(xs_ref[pl.ds(PACK_SUB, MOE_BLOCK, stride=TOK_SUB), :], F32)
        w_lo = rec[:, 1:2]
        w_hi = rec[:, 2:3]

        def ffn(w1, w3, w2):
            h1 = _dot(x, w1[0])
            hid = (h1 * jax.nn.sigmoid(h1)) * _dot(x, w3[0])
            return _dot(hid.astype(BF16), w2[0])

        y = ffn(w1a, w3a, w2a) * w_lo + ffn(w1b, w3b, w2b) * w_hi
        for s in range(TOK_SUB):
            ys_ref[pl.ds(s, MOE_BLOCK, stride=TOK_SUB), :] = y[:, s * 128:(s + 1) * 128]

    @pl.when(j >= nu_ref[0])
    def _():
        ys_ref[...] = jnp.zeros_like(ys_ref)


def _moe(order, ea, eb, n_used, xs, w1, w3, w2, layer):
    n_blocks = order.shape[0]
    blk_rows = MOE_BLOCK * TOK_SUB

    def w_spec(which):
        if which == 0:
            return pl.BlockSpec((None, 1, D_MODEL, D_MODEL), lambda j, order, ea, eb, nu: (layer, ea[j], 0, 0))
        return pl.BlockSpec((None, 1, D_MODEL, D_MODEL), lambda j, order, ea, eb, nu: (layer, eb[j], 0, 0))

    return pl.pallas_call(
        _moe_kernel,
        grid_spec=pltpu.PrefetchScalarGridSpec(
            num_scalar_prefetch=4,
            grid=(n_blocks,),
            in_specs=[
                pl.BlockSpec((blk_rows, 128),
                             lambda j, order, ea, eb, nu: (order[jnp.maximum(jnp.minimum(j, nu[0] - 1), 0)], 0)),
                w_spec(0), w_spec(0), w_spec(0), w_spec(1), w_spec(1), w_spec(1),
            ],
            out_specs=pl.BlockSpec((blk_rows, 128), lambda j, order, ea, eb, nu: (order[j], 0)),
        ),
        out_shape=jax.ShapeDtypeStruct((n_blocks * blk_rows, 128), F32),
        compiler_params=pltpu.CompilerParams(dimension_semantics=("arbitrary",), vmem_limit_bytes=VMEM_LIMIT),
        name="moe",
    )(order, ea, eb, n_used, xs, w1, w3, w2, w1, w3, w2)


def _combine_kernel(pos_ref, x1_ref, g2_ref, nf_ref, ys_ref, out_ref, buf_ref, sem):
    step = pl.program_id(0) * pl.num_programs(1) + pl.program_id(1)
    last_step = pl.num_programs(0) * pl.num_programs(1) - 1
    slot = step % 2

    def fetch_tile(tile, dst_slot):
        def issue(g, carry):
            r0 = pl.multiple_of(g * ISSUE_UNROLL, ISSUE_UNROLL)
            for j in range(ISSUE_UNROLL):
                _tok_copy(ys_ref, pos_ref[tile * ROW_TILE + r0 + j], buf_ref.at[dst_slot], r0 + j,
                          sem.at[dst_slot]).start(priority=j % 2)
            return carry
        lax.fori_loop(0, ROW_TILE // ISSUE_UNROLL, issue, 0)

    def wait_tile(dst_slot):
        pltpu.make_async_copy(ys_ref.at[pl.ds(0, ROW_TILE * TOK_SUB), :], buf_ref.at[dst_slot],
                              sem.at[dst_slot]).wait()

    @pl.when(step == 0)
    def _():
        fetch_tile(0, 0)

    fetch_tile(jnp.minimum(step + 1, last_step), 1 - slot)
    wait_tile(slot)
    y = _tok_columns(buf_ref, ROW_TILE, TOK_SUB, (slot,))
    x2 = x1_ref[...] + g2_ref[...] * y.reshape(x1_ref.shape)
    out_ref[...] = x2 * lax.rsqrt(jnp.mean(x2 * x2, axis=-1, keepdims=True) + EPS) * nf_ref[...]

    @pl.when(step == last_step)
    def _():
        wait_tile(1 - slot)


def _combine(pos, x1, g2, nf, ys):
    batch, seq, _ = x1.shape
    bb_n, tq = _tiling(batch, seq)
    tok_spec = pl.BlockSpec((bb_n, tq, D_MODEL), lambda b, t, pos: (b, t, 0))
    return pl.pallas_call(
        _combine_kernel,
        grid_spec=pltpu.PrefetchScalarGridSpec(
            num_scalar_prefetch=1,
            grid=(batch // bb_n, seq // tq),
            in_specs=[
                tok_spec,
                pl.BlockSpec((bb_n, 1, D_MODEL), lambda b, t, pos: (b, 0, 0)),
                pl.BlockSpec((1, D_MODEL), lambda b, t, pos: (0, 0)),
                pl.BlockSpec(memory_space=pl.ANY),
            ],
            out_specs=tok_spec,
            scratch_shapes=[pltpu.VMEM((2, ROW_TILE * TOK_SUB, 128), F32), pltpu.SemaphoreType.DMA((2,))],
        ),
        out_shape=jax.ShapeDtypeStruct(x1.shape, F32),
        compiler_params=pltpu.CompilerParams(dimension_semantics=("arbitrary", "arbitrary")),
        name="combine",
    )(pos, x1, g2, nf, ys)


def _block_order(dsp, bc, n_blocks):
    n_used = dsp[2, 0, 0].astype(jnp.int32)
    blk = jnp.arange(n_blocks, dtype=jnp.int32)
    owner = jnp.where(blk < n_used, bc[0, :n_blocks].astype(jnp.int32), N_COMBO)
    order = jnp.argsort(owner, stable=True).astype(jnp.int32)
    combo = jnp.minimum(owner[order], N_COMBO - 1)
    last = combo[jnp.maximum(n_used - 1, 0)]
    combo = jnp.where(blk < n_used, combo, last)
    lo_tab = jnp.array([0, 0, 0, 1, 1, 2], jnp.int32)
    hi_tab = jnp.array([1, 2, 3, 2, 3, 3], jnp.int32)
    grp = combo // PAIRS_PER_GROUP
    pair = combo % PAIRS_PER_GROUP
    return order, grp * EXP_PER_GROUP + lo_tab[pair], grp * EXP_PER_GROUP + hi_tab[pair], n_used.reshape(1)


def kernel(x_prompt, x_sample, state_hgrn, state_ret, c_prompt, c_sample, w_in, w_out, lb_logits, hgrn_norm,
           ret_norm, norm_attn, norm_moe, w_ada, b_ada, w_router, b_router, w1, w3, w2, norm_f):
    bp, tp, _ = x_prompt.shape
    bs, ts, _ = x_sample.shape
    n_p, n_s = bp * tp, bs * ts
    n_tok = n_p + n_s
    assert n_tok % MOE_BLOCK == 0
    n_blocks = n_tok // MOE_BLOCK + N_COMBO
    assert n_blocks <= BLOCK_LANES

    lb_cum = jnp.cumsum(jax.nn.softmax(lb_logits.astype(F32), axis=0), axis=0)
    lb_all = lb_cum - lb_cum[0:1]

    mod = _ada(jnp.concatenate([c_prompt, c_sample], axis=0), w_ada, b_ada)

    intra, inter, kdec, sdec = _retention_tables()
    consts = {
        "wrt": w_router.T.astype(BF16), "br": b_router.astype(F32).reshape(N_EXPERTS, 1),
        "intra": intra, "inter": inter, "kdec": kdec, "sdec": sdec,
        "tri": jnp.tile(jnp.tril(jnp.ones((CHUNK, CHUNK), BF16)), (1, 3)),
        "triu": jnp.triu(jnp.ones((ROW_TILE, ROW_TILE), BF16)),
        "lstrict": jnp.tril(jnp.ones((COMBO_ROWS, COMBO_ROWS), BF16), k=-1),
    }

    w1_b, w3_b, w2_b = w1.astype(BF16), w3.astype(BF16), w2.astype(BF16)

    xp, xs_ = x_prompt, x_sample
    sa_p = jnp.zeros((bp, N_HEADS, D_HEAD, D_HEAD), F32)
    sr_p = jnp.zeros((bp, N_HEADS, D_HEAD, D_HEAD), F32)
    new_states = []
    prev_p = prev_s = None
    for l in range(DEPTH):
        lw = {
            "na": norm_attn[l].reshape(1, D_MODEL), "nm": norm_moe[l].reshape(1, D_MODEL),
            "w_in": w_in[l].astype(BF16), "w_out": w_out[l].astype(BF16),
            "lb": lb_all[l].reshape(1, G_WIDTH), "ga": hgrn_norm[l].reshape(1, G_WIDTH),
            "gr": ret_norm[l].reshape(1, G_WIDTH),
        }
        parts = jnp.split(mod[l], 6, axis=-1)
        mods_p = [p[:bp, None, :] for p in parts]
        mods_s = [p[bp:, None, :] for p in parts]
        dsp0 = jnp.stack([jnp.zeros((COMBO_ROWS, 128), F32), jnp.full((COMBO_ROWS, 128), -1.0, F32),
                          jnp.zeros((COMBO_ROWS, 128), F32)])
        dispatch = (dsp0, jnp.zeros((ROUTE_ROWS, BLOCK_LANES), F32),
                    jnp.zeros(((n_blocks * MOE_BLOCK + ROW_TILE) * TOK_SUB, 128), jnp.uint32))
        x1p, pos_p, sa_np, sr_np, dispatch = _mixer(xp, mods_p[:5], consts, lw, sa_p, sr_p, 0, dispatch, prev_p)
        x1s, pos_s, sa_ns, sr_ns, dispatch = _mixer(xs_, mods_s[:5], consts, lw, state_hgrn[l].astype(F32),
                                                    state_ret[l].astype(F32), PAST_LEN, dispatch, prev_s)
        new_states.append((sa_np, sr_np, sa_ns, sr_ns))

        dsp, bc, xs = dispatch
        order, ea, eb, n_used = _block_order(dsp, bc, n_blocks)
        ys = _moe(order, ea, eb, n_used, xs, w1_b, w3_b, w2_b, l)

        xp, xs_ = x1p, x1s
        prev_p = (pos_p, mods_p[5], ys)
        prev_s = (pos_s, mods_s[5], ys)

    nf = norm_f.reshape(1, D_MODEL)
    xp = _combine(prev_p[0], xp, prev_p[1], nf, prev_p[2])
    xs_ = _combine(prev_s[0], xs_, prev_s[1], nf, prev_s[2])

    sa_prompt = jnp.stack([s[0] for s in new_states])
    sr_prompt = jnp.stack([s[1] for s in new_states])
    sa_sample = jnp.stack([s[2] for s in new_states])
    sr_sample = jnp.stack([s[3] for s in new_states])
    return (xp, xs_, sa_prompt, sr_prompt, sa_sample, sr_sample)
```

```python
import functools

import jax
import jax.numpy as jnp
from jax import lax
from jax.experimental import pallas as pl
from jax.experimental.pallas import tpu as pltpu

F32 = jnp.float32
BF16 = jnp.bfloat16

D_MODEL = 1024
DEPTH = 2
N_HEADS = 4
D_HEAD = 128
G_WIDTH = N_HEADS * D_HEAD
IN_COLS = 8 * G_WIDTH
CHUNK = 64
CHUNK_UNROLL = 2
STAGE_GAP = (None,) * 3
SUB = 8
LEVELS = (32, 16, 8)
N_EXPERTS = 16
EXP_PER_GROUP = 4
N_GROUPS = 4
PAIRS_PER_GROUP = 6
N_COMBO = N_GROUPS * PAIRS_PER_GROUP
COMBO_ROWS = 32
ROUTE_ROWS = 8
TOK_SUB = 8
PACK_SUB = 4
ROW_TILE = 512
ISSUE_UNROLL = 8
MOE_BLOCK = 256
BLOCK_LANES = 384
ADA_TILE = 1536
ROPE_BASE = 10000.0
PAST_LEN = 1024
EPS = 1e-6
LOG2E = 1.4426950408889634
VMEM_LIMIT = 56 * 1024 * 1024

_NT = (((1,), (1,)), ((), ()))
_TN = (((0,), (0,)), ((), ()))


def _dot(a, b):
    return jnp.dot(a, b, preferred_element_type=F32)


def _dot_nt(a, b):
    return lax.dot_general(a, b, _NT, preferred_element_type=F32)


def _dot_tn(a, b):
    return lax.dot_general(a, b, _TN, preferred_element_type=F32)


def _tiling(batch, seq):
    tq = min(seq, ROW_TILE)
    bb = ROW_TILE // tq
    assert tq % CHUNK == 0 and bb * tq == ROW_TILE and batch % bb == 0 and seq % tq == 0
    return bb, tq


def _ada_kernel(c_ref, w_ref, b_ref, o_ref):
    c = c_ref[...]
    cs = c * jax.nn.sigmoid(c)
    o_ref[0] = _dot(cs.astype(BF16), w_ref[0].astype(BF16)) + b_ref[0]


def _ada(c_all, w_ada, b_ada):
    m = c_all.shape[0]
    n = w_ada.shape[-1]
    return pl.pallas_call(
        _ada_kernel,
        grid=(DEPTH, n // ADA_TILE),
        in_specs=[
            pl.BlockSpec((m, D_MODEL), lambda l, j: (0, 0)),
            pl.BlockSpec((1, D_MODEL, ADA_TILE), lambda l, j: (l, 0, j)),
            pl.BlockSpec((1, 1, ADA_TILE), lambda l, j: (l, 0, j)),
        ],
        out_specs=pl.BlockSpec((1, m, ADA_TILE), lambda l, j: (l, 0, j)),
        out_shape=jax.ShapeDtypeStruct((DEPTH, m, n), F32),
        compiler_params=pltpu.CompilerParams(vmem_limit_bytes=VMEM_LIMIT),
        name="ada",
    )(c_all, w_ada, b_ada.reshape(DEPTH, 1, n))


def _route(lt, br):
    sc = jax.nn.sigmoid(lt)
    bi = sc + br
    s = [sc[e:e + 1] for e in range(N_EXPERTS)]
    v = [bi[e:e + 1] for e in range(N_EXPERTS)]
    gs = []
    for g in range(N_GROUPS):
        a = v[EXP_PER_GROUP * g:EXP_PER_GROUP * (g + 1)]
        m = a[0] + a[1]
        for i in range(EXP_PER_GROUP):
            for j in range(i + 1, EXP_PER_GROUP):
                if (i, j) != (0, 1):
                    m = jnp.maximum(m, a[i] + a[j])
        gs.append(m)
    best = gs[0]
    gsel = jnp.zeros_like(best)
    for g in range(1, N_GROUPS):
        upd = gs[g] > best
        best = jnp.where(upd, gs[g], best)
        gsel = jnp.where(upd, float(g), gsel)

    def pick(vals, key, n):
        out = vals[0]
        for g in range(1, n):
            out = jnp.where(key == float(g), vals[g], out)
        return out

    vv = [pick([v[EXP_PER_GROUP * g + j] for g in range(N_GROUPS)], gsel, N_GROUPS) for j in range(EXP_PER_GROUP)]
    ss = [pick([s[EXP_PER_GROUP * g + j] for g in range(N_GROUPS)], gsel, N_GROUPS) for j in range(EXP_PER_GROUP)]
    m1 = vv[0]
    i1 = jnp.zeros_like(m1)
    for j in range(1, EXP_PER_GROUP):
        upd = vv[j] > m1
        m1 = jnp.where(upd, vv[j], m1)
        i1 = jnp.where(upd, float(j), i1)
    m2 = jnp.full_like(m1, -jnp.inf)
    i2 = jnp.zeros_like(m1)
    for j in range(EXP_PER_GROUP):
        upd = (i1 != float(j)) & (vv[j] > m2)
        m2 = jnp.where(upd, vv[j], m2)
        i2 = jnp.where(upd, float(j), i2)
    s1 = pick(ss, i1, EXP_PER_GROUP)
    s2 = pick(ss, i2, EXP_PER_GROUP)
    den = s1 + s2
    w1 = s1 / den
    w2 = s2 / den
    first_lo = i1 < i2
    lo = jnp.where(first_lo, i1, i2)
    hi = jnp.where(first_lo, i2, i1)
    w_lo = jnp.where(first_lo, w1, w2)
    w_hi = jnp.where(first_lo, w2, w1)
    pair = jnp.where(lo == 0.0, hi - 1.0, jnp.where(lo == 1.0, hi + 1.0, 5.0))
    combo = gsel * float(PAIRS_PER_GROUP) + pair
    zero = jnp.zeros_like(combo)
    return jnp.concatenate([combo, w_lo, w_hi] + [zero] * (ROUTE_ROWS - 3), axis=0)


def _mixer_kernel(pos_ref, x_ref, g2p_ref, ys_ref, sh1_ref, sc1_ref, g1_ref, sh2_ref, sc2_ref, na_ref, nm_ref,
                  win_ref, wout_ref, lb_ref, ga_ref, gr_ref, wrt_ref, br_ref, cos_ref, sin_ref, intra_ref,
                  inter_ref, kdec_ref, sdec_ref, tri_ref, triu_ref, lstrict_ref, sa0_ref, sr0_ref,
                  dsp0_ref, bc0_ref, xs_in_ref,
                  x1_ref, pos3_ref, sa_ref, sr_ref, dsp_ref, bc_ref, xs_ref,
                  proj_ref, ob_ref, sat_ref, gbuf_ref, hbuf_ref, posv_ref, poss_ref, gsem, ssem, psem, scsem,
                  *, bb_n, tq, gather, dummy_row):
    del xs_in_ref
    tt = pl.program_id(1)
    rows = bb_n * tq
    n_chunks = tq // CHUNK
    n_iter = bb_n * n_chunks // CHUNK_UNROLL
    step = pl.program_id(0) * pl.num_programs(1) + tt
    last_step = pl.num_programs(0) * pl.num_programs(1) - 1

    x = x_ref[...]
    if gather:
        slot = step % 2

        def fetch(tile, dst_slot, r, priority):
            _tok_copy(ys_ref, pos_ref[tile * ROW_TILE + r], gbuf_ref.at[dst_slot], r,
                      gsem.at[dst_slot]).start(priority=priority)

        @pl.when(step == 0)
        def _():
            def issue(g, carry):
                r0 = pl.multiple_of(g * ISSUE_UNROLL, ISSUE_UNROLL)
                for j in range(ISSUE_UNROLL):
                    fetch(0, 0, r0 + j, j % 2)
                return carry
            lax.fori_loop(0, ROW_TILE // ISSUE_UNROLL, issue, 0)

        pltpu.make_async_copy(ys_ref.at[pl.ds(0, ROW_TILE * TOK_SUB), :], gbuf_ref.at[slot], gsem.at[slot]).wait()
        y_prev = _tok_columns(gbuf_ref, ROW_TILE, TOK_SUB, (slot,))
        x = x + g2p_ref[...] * y_prev.reshape(bb_n, tq, D_MODEL)

    @pl.when(tt == 0)
    def _():
        b0 = pl.program_id(0) * bb_n
        load_a = pltpu.make_async_copy(sa0_ref.at[pl.ds(b0, bb_n)], sa_ref, ssem.at[0])
        load_r = pltpu.make_async_copy(sr0_ref.at[pl.ds(b0, bb_n)], sr_ref, ssem.at[1])
        load_a.start()
        load_r.start()
        load_a.wait()
        load_r.wait()
        for b in range(bb_n):
            for hh in range(N_HEADS):
                sat_ref[b, hh] = sa_ref[b, hh].T

    @pl.when(step == 0)
    def _():
        dsp_ref[...] = dsp0_ref[...]
        bc_ref[...] = bc0_ref[...]
        hbuf_ref[...] = jnp.zeros_like(hbuf_ref)
        posv_ref[...] = jnp.zeros_like(posv_ref)
        init = pltpu.make_async_copy(posv_ref.at[0], poss_ref, psem)
        init.start()
        init.wait()

    @pl.when(step > 0)
    def _():
        pltpu.make_async_copy(posv_ref.at[0], poss_ref, psem).wait()

    def put(r, priority):
        dst = jnp.where(step > 0, poss_ref[r], dummy_row + r)
        _tok_copy(hbuf_ref, r, xs_ref, dst, scsem).start(priority=priority)

    for r in range(ROW_TILE):
        put(r, r % 2)
    if gather:
        next_tile = jnp.minimum(step + 1, last_step)
        for r in range(ROW_TILE):
            fetch(next_tile, 1 - slot, r, (r + 1) % 2)
    ms = jnp.mean(x * x, axis=-1, keepdims=True)
    h = x * lax.rsqrt(ms + EPS) * na_ref[...] * (1.0 + sc1_ref[...]) + sh1_ref[...]
    proj_ref[...] = _dot(h.reshape(rows, D_MODEL).astype(BF16), win_ref[...])

    lb = lb_ref[...]
    ln_1mlb = jnp.log1p(-lb)
    ga = ga_ref[...]
    gr = gr_ref[...]
    tri3 = tri_ref[...]
    t_i = lax.broadcasted_iota(jnp.int32, (CHUNK, CHUNK), 0)
    s_i = lax.broadcasted_iota(jnp.int32, (CHUNK, CHUNK), 1)
    level_masks = []
    for m in LEVELS:
        sh = m.bit_length() - 1
        level_masks.append(((t_i >> (sh + 1)) == (s_i >> (sh + 1)))
                           & (((t_i >> sh) & 1) == 1) & (((s_i >> sh) & 1) == 0))
    diag_mask = ((t_i >> 3) == (s_i >> 3)) & (s_i <= t_i)
    col_in_sub = s_i & (SUB - 1)

    def col(group, hh):
        return slice(group * G_WIDTH + hh * D_HEAD, group * G_WIDTH + (hh + 1) * D_HEAD)

    def hgrn_head(b, rs, hh):
        sl = slice(hh * D_HEAD, (hh + 1) * D_HEAD)
        z = proj_ref[rs, col(1, hh)]
        lbh = lb[:, sl]
        u = jnp.exp(-jnp.abs(z))
        ln_1pu = jnp.log(1.0 + u)
        num = jnp.where(z >= 0.0, 1.0 + lbh * u, lbh + u)
        l2f = (jnp.where(num > 0.0, jnp.log(num), z) - ln_1pu) * LOG2E
        l2k = (ln_1mlb[:, sl] - jnp.maximum(z, 0.0) - ln_1pu) * LOG2E
        hi = l2f.astype(BF16)
        rem = l2f - hi.astype(F32)
        mid = rem.astype(BF16)
        lo = (rem - mid.astype(F32)).astype(BF16)
        bcum = _dot(tri3, jnp.concatenate([hi, mid, lo], axis=0))
        yield from STAGE_GAP
        q = proj_ref[rs, col(0, hh)]
        vb = proj_ref[rs, col(2, hh)].astype(BF16)
        b_last = bcum[CHUNK - 1:CHUNK, :]
        q_in = (q * jnp.exp2(bcum)).astype(BF16)
        k_out = jnp.exp2(l2k + (b_last - bcum)).astype(BF16)
        upd = _dot_tn(vb, k_out)
        lvl = []
        for m in LEVELS:
            ref_row = bcum.reshape(CHUNK // (2 * m), 2 * m, D_HEAD)[:, m - 1:m, :]
            ref_row = jnp.broadcast_to(ref_row, (CHUNK // (2 * m), 2 * m, D_HEAD)).reshape(CHUNK, D_HEAD)
            dist = jnp.abs(bcum - ref_row)
            lvl.append(_dot_nt((q * jnp.exp2(-dist)).astype(BF16), jnp.exp2(l2k - dist).astype(BF16)))
        src3 = (l2k - bcum).reshape(CHUNK // SUB, SUB, D_HEAD)
        diag = jnp.zeros((CHUNK, CHUNK), F32)
        for sg in range(SUB):
            src = jnp.broadcast_to(src3[:, sg:sg + 1, :], src3.shape).reshape(CHUNK, D_HEAD)
            row_sum = jnp.sum(q * jnp.exp2(jnp.minimum(bcum + src, 0.0)), axis=-1, keepdims=True)
            diag = jnp.where(col_in_sub == sg, row_sum, diag)
        yield from STAGE_GAP
        a = jnp.where(diag_mask, diag, 0.0)
        for li in range(len(LEVELS)):
            a = jnp.where(level_masks[li], lvl[li], a)
        st = sat_ref[b, hh]
        o = _dot(a.astype(BF16), vb) + _dot_nt(q_in, st.astype(BF16))
        sat_ref[b, hh] = st * jnp.exp2(b_last) + upd
        yield from STAGE_GAP
        o = o * lax.rsqrt(jnp.mean(o * o, axis=-1, keepdims=True) + EPS)
        o = o * ga[:, sl] * jax.nn.sigmoid(proj_ref[rs, col(3, hh)])
        ob_ref[rs, sl] = o.astype(BF16)

    def ret_head(b, rs, t0, hh):
        sl = slice(hh * D_HEAD, (hh + 1) * D_HEAD)
        cs = cos_ref[pl.ds(t0, CHUNK), :]
        sn = sin_ref[pl.ds(t0, CHUNK), :]
        qh = proj_ref[rs, col(4, hh)]
        kh = proj_ref[rs, col(5, hh)]
        qr = qh * cs + pltpu.roll(qh, D_HEAD // 2, 1) * sn
        kr = (kh * cs + pltpu.roll(kh, D_HEAD // 2, 1) * sn) * (D_HEAD ** -0.5)
        qb = qr.astype(BF16)
        vh = proj_ref[rs, col(6, hh)].astype(BF16)
        sc = _dot_nt(qb, kr.astype(BF16))
        upd = _dot_tn((kr * kdec_ref[hh]).astype(BF16), vh)
        yield from STAGE_GAP
        s_r = sr_ref[b, hh]
        o = _dot((sc * intra_ref[hh]).astype(BF16), vh) + _dot(qb, s_r.astype(BF16)) * inter_ref[hh]
        sr_ref[b, hh] = s_r * sdec_ref[hh] + upd
        yield from STAGE_GAP
        mu = jnp.mean(o, axis=-1, keepdims=True)
        oc = o - mu
        var = jnp.mean(oc * oc, axis=-1, keepdims=True)
        o = oc * lax.rsqrt(var + EPS) * gr[:, sl]
        g = proj_ref[rs, col(7, hh)]
        o = o * (g * jax.nn.sigmoid(g))
        ob_ref[rs, col(1, hh)] = o.astype(BF16)

    def chunk_body(i, carry):
        heads = []
        for j in range(CHUNK_UNROLL):
            idx = i * CHUNK_UNROLL + j
            b = idx // n_chunks
            rs = pl.ds(pl.multiple_of(idx * CHUNK, CHUNK), CHUNK)
            t0 = pl.multiple_of((idx - b * n_chunks) * CHUNK, CHUNK)
            for hh in range(N_HEADS):
                heads.append(hgrn_head(b, rs, hh))
                heads.append(ret_head(b, rs, t0, hh))
        live = []
        while heads or live:
            if heads:
                live.append(heads.pop(0))
            nxt = []
            for h in live:
                try:
                    next(h)
                    nxt.append(h)
                except StopIteration:
                    pass
            live = nxt
        return carry

    lax.fori_loop(0, n_iter, chunk_body, 0)
    pltpu.make_async_copy(hbuf_ref, xs_ref.at[pl.ds(0, ROW_TILE * TOK_SUB), :], scsem).wait()

    @pl.when(tt == pl.num_programs(1) - 1)
    def _():
        for b in range(bb_n):
            for hh in range(N_HEADS):
                sa_ref[b, hh] = sat_ref[b, hh].T

    out = _dot(ob_ref[...], wout_ref[...]).reshape(bb_n, tq, D_MODEL)
    x1 = x + g1_ref[...] * out
    x1_ref[...] = x1
    ms2 = jnp.mean(x1 * x1, axis=-1, keepdims=True)
    h2 = x1 * lax.rsqrt(ms2 + EPS) * nm_ref[...] * (1.0 + sc2_ref[...]) + sh2_ref[...]
    h2b = h2.reshape(rows, D_MODEL).astype(BF16)
    lt = _dot_nt(wrt_ref[...], h2b)
    route = _route(lt, br_ref[...])
    bits = lax.bitcast_convert_type(h2b.astype(F32), jnp.uint32)
    words = (bits[:, 0:D_MODEL // 2] >> 16) | (bits[:, D_MODEL // 2:D_MODEL] & jnp.uint32(0xFFFF0000))
    for s in range(PACK_SUB):
        hbuf_ref[pl.ds(s, ROW_TILE, stride=TOK_SUB), :] = words[:, s * 128:(s + 1) * 128]
    rec = jnp.concatenate([route, jnp.zeros((128 - ROUTE_ROWS, rows), F32)], axis=0).T
    hbuf_ref[pl.ds(PACK_SUB, ROW_TILE, stride=TOK_SUB), :] = lax.bitcast_convert_type(rec, jnp.uint32)

    combo = route[0:1, :]
    ids = lax.broadcasted_iota(jnp.int32, (COMBO_ROWS, ROW_TILE), 0).astype(F32)
    onehot = (ids == combo).astype(F32)
    cum = _dot(onehot.astype(BF16), triu_ref[...])
    total = dsp_ref[0][:, 0:1]
    cur = dsp_ref[1][:, 0:1]
    n_alloc = dsp_ref[2][:, 0:1]
    n_c = cum[:, ROW_TILE - 1:ROW_TILE]
    inv_blk = 1.0 / MOE_BLOCK
    before = jnp.floor((total + float(MOE_BLOCK - 1)) * inv_blk)
    after = jnp.floor((total + n_c + float(MOE_BLOCK - 1)) * inv_blk)
    new = after - before
    new_wide = jnp.broadcast_to(new, (COMBO_ROWS, 128))
    first_new = n_alloc + _dot(lstrict_ref[...], new_wide.astype(BF16))[:, 0:1]
    ordinal = total + cum - 1.0
    lblk = jnp.floor(ordinal * inv_blk)
    offs = ordinal - lblk * float(MOE_BLOCK)
    blk = jnp.where(lblk < before, cur, first_new + (lblk - before))
    pos = jnp.sum(onehot * (blk * float(MOE_BLOCK) + offs), axis=0, keepdims=True)
    pos_i = jnp.broadcast_to(pos, (ROUTE_ROWS, ROW_TILE)).astype(jnp.int32)
    pos3_ref[0] = pos_i
    posv_ref[...] = pos_i
    wide = (COMBO_ROWS, 128)
    dsp_ref[0] = jnp.broadcast_to(total + n_c, wide)
    dsp_ref[1] = jnp.broadcast_to(jnp.where(new > 0.0, first_new + new - 1.0, cur), wide)
    dsp_ref[2] = jnp.broadcast_to(n_alloc + jnp.sum(new, axis=0, keepdims=True), wide)
    blk_id = lax.broadcasted_iota(jnp.int32, (COMBO_ROWS, BLOCK_LANES), 1).astype(F32)
    pair_id = lax.broadcasted_iota(jnp.int32, (COMBO_ROWS, BLOCK_LANES), 0).astype(F32)
    fresh = (blk_id >= first_new) & (blk_id < first_new + new)
    owner = jnp.sum(jnp.where(fresh, pair_id, 0.0), axis=0, keepdims=True)
    bc_ref[...] += jnp.broadcast_to(owner, bc_ref.shape)
    pltpu.make_async_copy(posv_ref.at[0], poss_ref, psem).start()

    @pl.when(step == last_step)
    def _():
        pltpu.make_async_copy(posv_ref.at[0], poss_ref, psem).wait()

        def issue(g, carry):
            r0 = pl.multiple_of(g * ISSUE_UNROLL, ISSUE_UNROLL)
            for j in range(ISSUE_UNROLL):
                _tok_copy(hbuf_ref, r0 + j, xs_ref, poss_ref[r0 + j], scsem).start(priority=j % 2)
            return carry
        lax.fori_loop(0, ROW_TILE // ISSUE_UNROLL, issue, 0)
        pltpu.make_async_copy(hbuf_ref, xs_ref.at[pl.ds(0, ROW_TILE * TOK_SUB), :], scsem).wait()
        if gather:
            pltpu.make_async_copy(ys_ref.at[pl.ds(0, ROW_TILE * TOK_SUB), :], gbuf_ref.at[1 - slot],
                                  gsem.at[1 - slot]).wait()


def _mixer(x, mods, consts, lw, s_a0, s_r0, offset, dispatch, prev=None):
    batch, seq, _ = x.shape
    bb_n, tq = _tiling(batch, seq)
    nb, nt = batch // bb_n, seq // tq
    cos, sin = _rope_tables(seq, offset)
    sh1, sc1, g1, sh2, sc2 = mods
    dsp, bc, xs = dispatch
    gather = prev is not None
    if gather:
        pos, g2p, ys = prev
    else:
        pos = jnp.zeros((1,), jnp.int32)
        g2p = jnp.zeros((batch, 1, D_MODEL), F32)
        ys = jnp.zeros((ROUTE_ROWS * TOK_SUB, 128), F32)

    def per_b(shape):
        return pl.BlockSpec(shape, lambda b, t, pos: (b,) + (0,) * (len(shape) - 1))

    def const(shape):
        return pl.BlockSpec(shape, lambda b, t, pos: (0,) * len(shape))

    any_spec = pl.BlockSpec(memory_space=pl.ANY)
    mod_spec = per_b((bb_n, 1, D_MODEL))
    state_spec = per_b((bb_n, N_HEADS, D_HEAD, D_HEAD))
    rope_spec = pl.BlockSpec((tq, D_HEAD), lambda b, t, pos: (t, 0))
    operands = [
        (x, pl.BlockSpec((bb_n, tq, D_MODEL), lambda b, t, pos: (b, t, 0))),
        (g2p, mod_spec), (ys, any_spec),
        (sh1, mod_spec), (sc1, mod_spec), (g1, mod_spec), (sh2, mod_spec), (sc2, mod_spec),
        (lw["na"], const((1, D_MODEL))), (lw["nm"], const((1, D_MODEL))),
        (lw["w_in"], const((D_MODEL, IN_COLS))), (lw["w_out"], const((2 * G_WIDTH, D_MODEL))),
        (lw["lb"], const((1, G_WIDTH))), (lw["ga"], const((1, G_WIDTH))), (lw["gr"], const((1, G_WIDTH))),
        (consts["wrt"], const((N_EXPERTS, D_MODEL))), (consts["br"], const((N_EXPERTS, 1))),
        (cos, rope_spec), (sin, rope_spec),
        (consts["intra"], const((N_HEADS, CHUNK, CHUNK))), (consts["inter"], const((N_HEADS, CHUNK, D_HEAD))),
        (consts["kdec"], const((N_HEADS, CHUNK, D_HEAD))), (consts["sdec"], const((N_HEADS, 1, D_HEAD))),
        (consts["tri"], const((CHUNK, 3 * CHUNK))), (consts["triu"], const((ROW_TILE, ROW_TILE))),
        (consts["lstrict"], const((COMBO_ROWS, COMBO_ROWS))),
        (s_a0, any_spec), (s_r0, any_spec),
        (dsp, const(dsp.shape)), (bc, const(bc.shape)), (xs, any_spec),
    ]
    out_specs = [
        pl.BlockSpec((bb_n, tq, D_MODEL), lambda b, t, pos: (b, t, 0)),
        pl.BlockSpec((1, ROUTE_ROWS, ROW_TILE), lambda b, t, pos: (b * nt + t, 0, 0)),
        state_spec, state_spec,
        const(dsp.shape), const(bc.shape), any_spec,
    ]
    out_shape = [
        jax.ShapeDtypeStruct((batch, seq, D_MODEL), F32),
        jax.ShapeDtypeStruct((nb * nt, ROUTE_ROWS, ROW_TILE), jnp.int32),
        jax.ShapeDtypeStruct((batch, N_HEADS, D_HEAD, D_HEAD), F32),
        jax.ShapeDtypeStruct((batch, N_HEADS, D_HEAD, D_HEAD), F32),
        jax.ShapeDtypeStruct(dsp.shape, F32),
        jax.ShapeDtypeStruct(bc.shape, F32),
        jax.ShapeDtypeStruct(xs.shape, xs.dtype),
    ]
    gather_rows = ROW_TILE if gather else ROUTE_ROWS
    x1, pos3, s_a, s_r, dsp, bc, xs = pl.pallas_call(
        functools.partial(_mixer_kernel, bb_n=bb_n, tq=tq, gather=gather, dummy_row=xs.shape[0] // TOK_SUB - ROW_TILE),
        grid_spec=pltpu.PrefetchScalarGridSpec(
            num_scalar_prefetch=1,
            grid=(nb, nt),
            in_specs=[spec for _, spec in operands],
            out_specs=out_specs,
            scratch_shapes=[
                pltpu.VMEM((ROW_TILE, IN_COLS), F32),
                pltpu.VMEM((ROW_TILE, 2 * G_WIDTH), BF16),
                pltpu.VMEM((bb_n, N_HEADS, D_HEAD, D_HEAD), F32),
                pltpu.VMEM((2, gather_rows * TOK_SUB, 128), F32),
                pltpu.VMEM((ROW_TILE * TOK_SUB, 128), jnp.uint32),
                pltpu.VMEM((ROUTE_ROWS, ROW_TILE), jnp.int32),
                pltpu.SMEM((ROW_TILE,), jnp.int32),
                pltpu.SemaphoreType.DMA((2,)),
                pltpu.SemaphoreType.DMA((2,)),
                pltpu.SemaphoreType.DMA,
                pltpu.SemaphoreType.DMA,
            ],
        ),
        out_shape=out_shape,
        input_output_aliases={len(operands): 6},
        compiler_params=pltpu.CompilerParams(
            dimension_semantics=("arbitrary", "arbitrary"), vmem_limit_bytes=VMEM_LIMIT),
        name="mixer",
    )(pos, *[arr for arr, _ in operands])
    return x1, pos3[:, 0, :].reshape(batch * seq), s_a, s_r, (dsp, bc, xs)


def _rope_tables(seq, offset):
    half = D_HEAD // 2
    freq = ROPE_BASE ** (-jnp.arange(half, dtype=F32) / half)
    ang = (jnp.arange(seq, dtype=F32) + offset)[:, None] * freq[None, :]
    cos, sin = jnp.cos(ang), jnp.sin(ang)
    return jnp.concatenate([cos, cos], axis=-1), jnp.concatenate([-sin, sin], axis=-1)


def _retention_tables():
    log_g = jnp.log(1.0 - 2.0 ** (-5.0 - jnp.arange(N_HEADS, dtype=F32)))
    idx = jnp.arange(CHUNK, dtype=F32)
    rel = idx[:, None] - idx[None, :]
    intra = jnp.where(rel >= 0, jnp.exp(jnp.maximum(rel, 0.0)[None] * log_g[:, None, None]), 0.0)
    inter = jnp.exp((idx[None, :] + 1.0) * log_g[:, None])[..., None]
    kdec = jnp.exp((CHUNK - 1.0 - idx[None, :]) * log_g[:, None])[..., None]
    sdec = jnp.exp(CHUNK * log_g)[:, None, None]
    wide = (N_HEADS, CHUNK, D_HEAD)
    return (intra, jnp.broadcast_to(inter, wide), jnp.broadcast_to(kdec, wide),
            jnp.broadcast_to(sdec, (N_HEADS, 1, D_HEAD)))


def _tok_copy(src_ref, src_tok, dst_ref, dst_tok, sem):
    src = src_ref.at[pl.ds(pl.multiple_of(src_tok * TOK_SUB, TOK_SUB), TOK_SUB), :]
    dst = dst_ref.at[pl.ds(pl.multiple_of(dst_tok * TOK_SUB, TOK_SUB), TOK_SUB), :]
    return pltpu.make_async_copy(src, dst, sem)


def _tok_columns(ref, n_tok, n_sub, lead=()):
    return jnp.concatenate([ref[lead + (pl.ds(s, n_tok, stride=TOK_SUB), slice(None))] for s in range(n_sub)], axis=1)


def _moe_kernel(order_ref, ea_ref, eb_ref, nu_ref, xs_ref, w1a, w3a, w2a, w1b, w3b, w2b, ys_ref):
    del order_ref, ea_ref, eb_ref
    j = pl.program_id(0)

    @pl.when(j < nu_ref[0])
    def _():
        words = _tok_columns(xs_ref, MOE_BLOCK, PACK_SUB)
        lo = lax.bitcast_convert_type(words << 16, F32)
        hi = lax.bitcast_convert_type(words & jnp.uint32(0xFFFF0000), F32)
        x = jnp.concatenate([lo, hi], axis=1).astype(BF16)
        rec = lax.bitcast_convert_type(xs_ref[pl.ds(PACK_SUB, MOE_BLOCK, stride=TOK_SUB), :], F32)
        w_lo = rec[:, 1:2]
        w_hi = rec[:, 2:3]

        def ffn(w1, w3, w2):
            h1 = _dot(x, w1[0])
            hid = (h1 * jax.nn.sigmoid(h1)) * _dot(x, w3[0])
            return _dot(hid.astype(BF16), w2[0])

        y = ffn(w1a, w3a, w2a) * w_lo + ffn(w1b, w3b, w2b) * w_hi
        for s in range(TOK_SUB):
            ys_ref[pl.ds(s, MOE_BLOCK, stride=TOK_SUB), :] = y[:, s * 128:(s + 1) * 128]

    @pl.when(j >= nu_ref[0])
    def _():
        ys_ref[...] = jnp.zeros_like(ys_ref)


def _moe(order, ea, eb, n_used, xs, w1, w3, w2, layer):
    n_blocks = order.shape[0]
    blk_rows = MOE_BLOCK * TOK_SUB

    def w_spec(which):
        if which == 0:
            return pl.BlockSpec((None, 1, D_MODEL, D_MODEL), lambda j, order, ea, eb, nu: (layer, ea[j], 0, 0))
        return pl.BlockSpec((None, 1, D_MODEL, D_MODEL), lambda j, order, ea, eb, nu: (layer, eb[j], 0, 0))

    return pl.pallas_call(
        _moe_kernel,
        grid_spec=pltpu.PrefetchScalarGridSpec(
            num_scalar_prefetch=4,
            grid=(n_blocks,),
            in_specs=[
                pl.BlockSpec((blk_rows, 128),
                             lambda j, order, ea, eb, nu: (order[jnp.maximum(jnp.minimum(j, nu[0] - 1), 0)], 0)),
                w_spec(0), w_spec(0), w_spec(0), w_spec(1), w_spec(1), w_spec(1),
            ],
            out_specs=pl.BlockSpec((blk_rows, 128), lambda j, order, ea, eb, nu: (order[j], 0)),
        ),
        out_shape=jax.ShapeDtypeStruct((n_blocks * blk_rows, 128), F32),
        compiler_params=pltpu.CompilerParams(dimension_semantics=("arbitrary",), vmem_limit_bytes=VMEM_LIMIT),
        name="moe",
    )(order, ea, eb, n_used, xs, w1, w3, w2, w1, w3, w2)


def _combine_kernel(pos_ref, x1_ref, g2_ref, nf_ref, ys_ref, out_ref, buf_ref, sem):
    step = pl.program_id(0) * pl.num_programs(1) + pl.program_id(1)
    last_step = pl.num_programs(0) * pl.num_programs(1) - 1
    slot = step % 2

    def fetch_tile(tile, dst_slot):
        def issue(g, carry):
            r0 = pl.multiple_of(g * ISSUE_UNROLL, ISSUE_UNROLL)
            for j in range(ISSUE_UNROLL):
                _tok_copy(ys_ref, pos_ref[tile * ROW_TILE + r0 + j], buf_ref.at[dst_slot], r0 + j,
                          sem.at[dst_slot]).start(priority=j % 2)
            return carry
        lax.fori_loop(0, ROW_TILE // ISSUE_UNROLL, issue, 0)

    def wait_tile(dst_slot):
        pltpu.make_async_copy(ys_ref.at[pl.ds(0, ROW_TILE * TOK_SUB), :], buf_ref.at[dst_slot],
                              sem.at[dst_slot]).wait()

    @pl.when(step == 0)
    def _():
        fetch_tile(0, 0)

    fetch_tile(jnp.minimum(step + 1, last_step), 1 - slot)
    wait_tile(slot)
    y = _tok_columns(buf_ref, ROW_TILE, TOK_SUB, (slot,))
    x2 = x1_ref[...] + g2_ref[...] * y.reshape(x1_ref.shape)
    out_ref[...] = x2 * lax.rsqrt(jnp.mean(x2 * x2, axis=-1, keepdims=True) + EPS) * nf_ref[...]

    @pl.when(step == last_step)
    def _():
        wait_tile(1 - slot)


def _combine(pos, x1, g2, nf, ys):
    batch, seq, _ = x1.shape
    bb_n, tq = _tiling(batch, seq)
    tok_spec = pl.BlockSpec((bb_n, tq, D_MODEL), lambda b, t, pos: (b, t, 0))
    return pl.pallas_call(
        _combine_kernel,
        grid_spec=pltpu.PrefetchScalarGridSpec(
            num_scalar_prefetch=1,
            grid=(batch // bb_n, seq // tq),
            in_specs=[
                tok_spec,
                pl.BlockSpec((bb_n, 1, D_MODEL), lambda b, t, pos: (b, 0, 0)),
                pl.BlockSpec((1, D_MODEL), lambda b, t, pos: (0, 0)),
                pl.BlockSpec(memory_space=pl.ANY),
            ],
            out_specs=tok_spec,
            scratch_shapes=[pltpu.VMEM((2, ROW_TILE * TOK_SUB, 128), F32), pltpu.SemaphoreType.DMA((2,))],
        ),
        out_shape=jax.ShapeDtypeStruct(x1.shape, F32),
        compiler_params=pltpu.CompilerParams(dimension_semantics=("arbitrary", "arbitrary")),
        name="combine",
    )(pos, x1, g2, nf, ys)


def _block_order(dsp, bc, n_blocks):
    n_used = dsp[2, 0, 0].astype(jnp.int32)
    blk = jnp.arange(n_blocks, dtype=jnp.int32)
    owner = jnp.where(blk < n_used, bc[0, :n_blocks].astype(jnp.int32), N_COMBO)
    earlier = (owner[None, :] < owner[:, None]) | ((owner[None, :] == owner[:, None]) & (blk[None, :] < blk[:, None]))
    rank = jnp.sum(earlier.astype(jnp.int32), axis=1)
    order = jnp.sum(jnp.where(rank[None, :] == blk[:, None], blk[None, :], 0), axis=1).astype(jnp.int32)
    combo = jnp.minimum(owner[order], N_COMBO - 1)
    last = combo[jnp.maximum(n_used - 1, 0)]
    combo = jnp.where(blk < n_used, combo, last)
    lo_tab = jnp.array([0, 0, 0, 1, 1, 2], jnp.int32)
    hi_tab = jnp.array([1, 2, 3, 2, 3, 3], jnp.int32)
    grp = combo // PAIRS_PER_GROUP
    pair = combo % PAIRS_PER_GROUP
    return order, grp * EXP_PER_GROUP + lo_tab[pair], grp * EXP_PER_GROUP + hi_tab[pair], n_used.reshape(1)


def kernel(x_prompt, x_sample, state_hgrn, state_ret, c_prompt, c_sample, w_in, w_out, lb_logits, hgrn_norm,
           ret_norm, norm_attn, norm_moe, w_ada, b_ada, w_router, b_router, w1, w3, w2, norm_f):
    bp, tp, _ = x_prompt.shape
    bs, ts, _ = x_sample.shape
    n_p, n_s = bp * tp, bs * ts
    n_tok = n_p + n_s
    assert n_tok % MOE_BLOCK == 0
    n_blocks = n_tok // MOE_BLOCK + N_COMBO
    assert n_blocks <= BLOCK_LANES

    lb_cum = jnp.cumsum(jax.nn.softmax(lb_logits.astype(F32), axis=0), axis=0)
    lb_all = lb_cum - lb_cum[0:1]

    mod = _ada(jnp.concatenate([c_prompt, c_sample], axis=0), w_ada, b_ada)

    intra, inter, kdec, sdec = _retention_tables()
    consts = {
        "wrt": w_router.T.astype(BF16), "br": b_router.astype(F32).reshape(N_EXPERTS, 1),
        "intra": intra, "inter": inter, "kdec": kdec, "sdec": sdec,
        "tri": jnp.tile(jnp.tril(jnp.ones((CHUNK, CHUNK), BF16)), (1, 3)),
        "triu": jnp.triu(jnp.ones((ROW_TILE, ROW_TILE), BF16)),
        "lstrict": jnp.tril(jnp.ones((COMBO_ROWS, COMBO_ROWS), BF16), k=-1),
    }

    w1_b, w3_b, w2_b = w1.astype(BF16), w3.astype(BF16), w2.astype(BF16)

    xp, xs_ = x_prompt, x_sample
    sa_p = jnp.zeros((bp, N_HEADS, D_HEAD, D_HEAD), F32)
    sr_p = jnp.zeros((bp, N_HEADS, D_HEAD, D_HEAD), F32)
    new_states = []
    prev_p = prev_s = None
    for l in range(DEPTH):
        lw = {
            "na": norm_attn[l].reshape(1, D_MODEL), "nm": norm_moe[l].reshape(1, D_MODEL),
            "w_in": w_in[l].astype(BF16), "w_out": w_out[l].astype(BF16),
            "lb": lb_all[l].reshape(1, G_WIDTH), "ga": hgrn_norm[l].reshape(1, G_WIDTH),
            "gr": ret_norm[l].reshape(1, G_WIDTH),
        }
        parts = jnp.split(mod[l], 6, axis=-1)
        mods_p = [p[:bp, None, :] for p in parts]
        mods_s = [p[bp:, None, :] for p in parts]
        dsp0 = jnp.stack([jnp.zeros((COMBO_ROWS, 128), F32), jnp.full((COMBO_ROWS, 128), -1.0, F32),
                          jnp.zeros((COMBO_ROWS, 128), F32)])
        dispatch = (dsp0, jnp.zeros((ROUTE_ROWS, BLOCK_LANES), F32),
                    jnp.zeros(((n_blocks * MOE_BLOCK + ROW_TILE) * TOK_SUB, 128), jnp.uint32))
        x1p, pos_p, sa_np, sr_np, dispatch = _mixer(xp, mods_p[:5], consts, lw, sa_p, sr_p, 0, dispatch, prev_p)
        x1s, pos_s, sa_ns, sr_ns, dispatch = _mixer(xs_, mods_s[:5], consts, lw, state_hgrn[l].astype(F32),
                                                    state_ret[l].astype(F32), PAST_LEN, dispatch, prev_s)
        new_states.append((sa_np, sr_np, sa_ns, sr_ns))

        dsp, bc, xs = dispatch
        order, ea, eb, n_used = _block_order(dsp, bc, n_blocks)
        ys = _moe(order, ea, eb, n_used, xs, w1_b, w3_b, w2_b, l)

        xp, xs_ = x1p, x1s
        prev_p = (pos_p, mods_p[5], ys)
        prev_s = (pos_s, mods_s[5], ys)

    nf = norm_f.reshape(1, D_MODEL)
    xp = _combine(prev_p[0], xp, prev_p[1], nf, prev_p[2])
    xs_ = _combine(prev_s[0], xs_, prev_s[1], nf, prev_s[2])

    sa_prompt = jnp.stack([s[0] for s in new_states])
    sr_prompt = jnp.stack([s[1] for s in new_states])
    sa_sample = jnp.stack([s[2] for s in new_states])
    sr_sample = jnp.stack([s[3] for s in new_states])
    return (xp, xs_, sa_prompt, sr_prompt, sa_sample, sr_sample)
```

```python
import functools

import jax
import jax.numpy as jnp
from jax import lax
from jax.experimental import pallas as pl
from jax.experimental.pallas import tpu as pltpu

F32 = jnp.float32
BF16 = jnp.bfloat16

D_MODEL = 1024
DEPTH = 2
N_HEADS = 4
D_HEAD = 128
G_WIDTH = N_HEADS * D_HEAD
IN_COLS = 8 * G_WIDTH
CHUNK = 64
CHUNK_UNROLL = 2
STAGE_GAP = (None,) * 3
SUB = 8
LEVELS = (32, 16, 8)
N_EXPERTS = 16
EXP_PER_GROUP = 4
N_GROUPS = 4
PAIRS_PER_GROUP = 6
N_COMBO = N_GROUPS * PAIRS_PER_GROUP
COMBO_ROWS = 32
ROUTE_ROWS = 8
TOK_SUB = 8
PACK_SUB = 4
ROW_TILE = 512
ISSUE_UNROLL = 8
MOE_BLOCK = 256
BLOCK_LANES = 384
ADA_TILE = 1536
ROPE_BASE = 10000.0
PAST_LEN = 1024
EPS = 1e-6
LOG2E = 1.4426950408889634
VMEM_LIMIT = 56 * 1024 * 1024

_NT = (((1,), (1,)), ((), ()))
_TN = (((0,), (0,)), ((), ()))


def _dot(a, b):
    return jnp.dot(a, b, preferred_element_type=F32)


def _dot_nt(a, b):
    return lax.dot_general(a, b, _NT, preferred_element_type=F32)


def _dot_tn(a, b):
    return lax.dot_general(a, b, _TN, preferred_element_type=F32)


def _tiling(batch, seq):
    tq = min(seq, ROW_TILE)
    bb = ROW_TILE // tq
    assert tq % CHUNK == 0 and bb * tq == ROW_TILE and batch % bb == 0 and seq % tq == 0
    return bb, tq


def _ada_kernel(c_ref, w_ref, b_ref, o_ref):
    c = c_ref[...]
    cs = c * jax.nn.sigmoid(c)
    o_ref[0] = _dot(cs.astype(BF16), w_ref[0].astype(BF16)) + b_ref[0]


def _ada(c_all, w_ada, b_ada):
    m = c_all.shape[0]
    n = w_ada.shape[-1]
    return pl.pallas_call(
        _ada_kernel,
        grid=(DEPTH, n // ADA_TILE),
        in_specs=[
            pl.BlockSpec((m, D_MODEL), lambda l, j: (0, 0)),
            pl.BlockSpec((1, D_MODEL, ADA_TILE), lambda l, j: (l, 0, j)),
            pl.BlockSpec((1, 1, ADA_TILE), lambda l, j: (l, 0, j)),
        ],
        out_specs=pl.BlockSpec((1, m, ADA_TILE), lambda l, j: (l, 0, j)),
        out_shape=jax.ShapeDtypeStruct((DEPTH, m, n), F32),
        compiler_params=pltpu.CompilerParams(vmem_limit_bytes=VMEM_LIMIT),
        name="ada",
    )(c_all, w_ada, b_ada.reshape(DEPTH, 1, n))


def _route(lt, br):
    sc = jax.nn.sigmoid(lt)
    bi = sc + br
    s = [sc[e:e + 1] for e in range(N_EXPERTS)]
    v = [bi[e:e + 1] for e in range(N_EXPERTS)]
    gs = []
    for g in range(N_GROUPS):
        a = v[EXP_PER_GROUP * g:EXP_PER_GROUP * (g + 1)]
        m = a[0] + a[1]
        for i in range(EXP_PER_GROUP):
            for j in range(i + 1, EXP_PER_GROUP):
                if (i, j) != (0, 1):
                    m = jnp.maximum(m, a[i] + a[j])
        gs.append(m)
    best = gs[0]
    gsel = jnp.zeros_like(best)
    for g in range(1, N_GROUPS):
        upd = gs[g] > best
        best = jnp.where(upd, gs[g], best)
        gsel = jnp.where(upd, float(g), gsel)

    def pick(vals, key, n):
        out = vals[0]
        for g in range(1, n):
            out = jnp.where(key == float(g), vals[g], out)
        return out

    vv = [pick([v[EXP_PER_GROUP * g + j] for g in range(N_GROUPS)], gsel, N_GROUPS) for j in range(EXP_PER_GROUP)]
    ss = [pick([s[EXP_PER_GROUP * g + j] for g in range(N_GROUPS)], gsel, N_GROUPS) for j in range(EXP_PER_GROUP)]
    m1 = vv[0]
    i1 = jnp.zeros_like(m1)
    for j in range(1, EXP_PER_GROUP):
        upd = vv[j] > m1
        m1 = jnp.where(upd, vv[j], m1)
        i1 = jnp.where(upd, float(j), i1)
    m2 = jnp.full_like(m1, -jnp.inf)
    i2 = jnp.zeros_like(m1)
    for j in range(EXP_PER_GROUP):
        upd = (i1 != float(j)) & (vv[j] > m2)
        m2 = jnp.where(upd, vv[j], m2)
        i2 = jnp.where(upd, float(j), i2)
    s1 = pick(ss, i1, EXP_PER_GROUP)
    s2 = pick(ss, i2, EXP_PER_GROUP)
    den = s1 + s2
    w1 = s1 / den
    w2 = s2 / den
    first_lo = i1 < i2
    lo = jnp.where(first_lo, i1, i2)
    hi = jnp.where(first_lo, i2, i1)
    w_lo = jnp.where(first_lo, w1, w2)
    w_hi = jnp.where(first_lo, w2, w1)
    pair = jnp.where(lo == 0.0, hi - 1.0, jnp.where(lo == 1.0, hi + 1.0, 5.0))
    combo = gsel * float(PAIRS_PER_GROUP) + pair
    zero = jnp.zeros_like(combo)
    return jnp.concatenate([combo, w_lo, w_hi] + [zero] * (ROUTE_ROWS - 3), axis=0)


def _mixer_kernel(pos_ref, x_ref, g2p_ref, ys_ref, sh1_ref, sc1_ref, g1_ref, sh2_ref, sc2_ref, na_ref, nm_ref,
                  win_ref, wout_ref, lb_ref, ga_ref, gr_ref, wrt_ref, br_ref, cos_ref, sin_ref, intra_ref,
                  inter_ref, kdec_ref, sdec_ref, tri_ref, triu_ref, lstrict_ref, sa0_ref, sr0_ref,
                  dsp0_ref, bc0_ref, xs_in_ref,
                  x1_ref, pos3_ref, sa_ref, sr_ref, dsp_ref, bc_ref, xs_ref,
                  proj_ref, ob_ref, sat_ref, gbuf_ref, hbuf_ref, posv_ref, poss_ref, gsem, ssem, psem, scsem,
                  *, bb_n, tq, gather, dummy_row):
    del xs_in_ref
    tt = pl.program_id(1)
    rows = bb_n * tq
    n_chunks = tq // CHUNK
    n_iter = bb_n * n_chunks // CHUNK_UNROLL
    step = pl.program_id(0) * pl.num_programs(1) + tt
    last_step = pl.num_programs(0) * pl.num_programs(1) - 1

    x = x_ref[...]
    if gather:
        slot = step % 2

        def fetch(tile, dst_slot, r, priority):
            _tok_copy(ys_ref, pos_ref[tile * ROW_TILE + r], gbuf_ref.at[dst_slot], r,
                      gsem.at[dst_slot]).start(priority=priority)

        @pl.when(step == 0)
        def _():
            def issue(g, carry):
                r0 = pl.multiple_of(g * ISSUE_UNROLL, ISSUE_UNROLL)
                for j in range(ISSUE_UNROLL):
                    fetch(0, 0, r0 + j, j % 2)
                return carry
            lax.fori_loop(0, ROW_TILE // ISSUE_UNROLL, issue, 0)

        pltpu.make_async_copy(ys_ref.at[pl.ds(0, ROW_TILE * TOK_SUB), :], gbuf_ref.at[slot], gsem.at[slot]).wait()
        y_prev = _tok_columns(gbuf_ref, ROW_TILE, TOK_SUB, (slot,))
        x = x + g2p_ref[...] * y_prev.reshape(bb_n, tq, D_MODEL)

    def state_loads():
        b0 = pl.program_id(0) * bb_n
        return (pltpu.make_async_copy(sa0_ref.at[pl.ds(b0, bb_n)], sa_ref, ssem.at[0]),
                pltpu.make_async_copy(sr0_ref.at[pl.ds(b0, bb_n)], sr_ref, ssem.at[1]))

    @pl.when(tt == 0)
    def _():
        for load in state_loads():
            load.start()

    @pl.when(step == 0)
    def _():
        dsp_ref[...] = dsp0_ref[...]
        bc_ref[...] = bc0_ref[...]
        hbuf_ref[...] = jnp.zeros_like(hbuf_ref)
        posv_ref[...] = jnp.zeros_like(posv_ref)
        init = pltpu.make_async_copy(posv_ref.at[0], poss_ref, psem)
        init.start()
        init.wait()

    @pl.when(step > 0)
    def _():
        pltpu.make_async_copy(posv_ref.at[0], poss_ref, psem).wait()

    def put(r, priority):
        dst = jnp.where(step > 0, poss_ref[r], dummy_row + r)
        _tok_copy(hbuf_ref, r, xs_ref, dst, scsem).start(priority=priority)

    for r in range(ROW_TILE):
        put(r, r % 2)
    if gather:
        next_tile = jnp.minimum(step + 1, last_step)
        for r in range(ROW_TILE):
            fetch(next_tile, 1 - slot, r, (r + 1) % 2)
    ms = jnp.mean(x * x, axis=-1, keepdims=True)
    h = x * lax.rsqrt(ms + EPS) * na_ref[...] * (1.0 + sc1_ref[...]) + sh1_ref[...]
    proj_ref[...] = _dot(h.reshape(rows, D_MODEL).astype(BF16), win_ref[...])

    @pl.when(tt == 0)
    def _():
        for load in state_loads():
            load.wait()
        for b in range(bb_n):
            for hh in range(N_HEADS):
                sat_ref[b, hh] = sa_ref[b, hh].T

    lb = lb_ref[...]
    ln_1mlb = jnp.log1p(-lb)
    ga = ga_ref[...]
    gr = gr_ref[...]
    tri3 = tri_ref[...]
    t_i = lax.broadcasted_iota(jnp.int32, (CHUNK, CHUNK), 0)
    s_i = lax.broadcasted_iota(jnp.int32, (CHUNK, CHUNK), 1)
    level_masks = []
    for m in LEVELS:
        sh = m.bit_length() - 1
        level_masks.append(((t_i >> (sh + 1)) == (s_i >> (sh + 1)))
                           & (((t_i >> sh) & 1) == 1) & (((s_i >> sh) & 1) == 0))
    diag_mask = ((t_i >> 3) == (s_i >> 3)) & (s_i <= t_i)
    col_in_sub = s_i & (SUB - 1)

    def col(group, hh):
        return slice(group * G_WIDTH + hh * D_HEAD, group * G_WIDTH + (hh + 1) * D_HEAD)

    def hgrn_head(b, rs, hh):
        sl = slice(hh * D_HEAD, (hh + 1) * D_HEAD)
        z = proj_ref[rs, col(1, hh)]
        lbh = lb[:, sl]
        u = jnp.exp(-jnp.abs(z))
        ln_1pu = jnp.log(1.0 + u)
        num = jnp.where(z >= 0.0, 1.0 + lbh * u, lbh + u)
        l2f = (jnp.where(num > 0.0, jnp.log(num), z) - ln_1pu) * LOG2E
        l2k = (ln_1mlb[:, sl] - jnp.maximum(z, 0.0) - ln_1pu) * LOG2E
        hi = l2f.astype(BF16)
        rem = l2f - hi.astype(F32)
        mid = rem.astype(BF16)
        lo = (rem - mid.astype(F32)).astype(BF16)
        bcum = _dot(tri3, jnp.concatenate([hi, mid, lo], axis=0))
        yield from STAGE_GAP
        q = proj_ref[rs, col(0, hh)]
        vb = proj_ref[rs, col(2, hh)].astype(BF16)
        b_last = bcum[CHUNK - 1:CHUNK, :]
        q_in = (q * jnp.exp2(bcum)).astype(BF16)
        k_out = jnp.exp2(l2k + (b_last - bcum)).astype(BF16)
        upd = _dot_tn(vb, k_out)
        lvl = []
        for m in LEVELS:
            ref_row = bcum.reshape(CHUNK // (2 * m), 2 * m, D_HEAD)[:, m - 1:m, :]
            ref_row = jnp.broadcast_to(ref_row, (CHUNK // (2 * m), 2 * m, D_HEAD)).reshape(CHUNK, D_HEAD)
            dist = jnp.abs(bcum - ref_row)
            lvl.append(_dot_nt((q * jnp.exp2(-dist)).astype(BF16), jnp.exp2(l2k - dist).astype(BF16)))
        src3 = (l2k - bcum).reshape(CHUNK // SUB, SUB, D_HEAD)
        diag = jnp.zeros((CHUNK, CHUNK), F32)
        for sg in range(SUB):
            src = jnp.broadcast_to(src3[:, sg:sg + 1, :], src3.shape).reshape(CHUNK, D_HEAD)
            row_sum = jnp.sum(q * jnp.exp2(jnp.minimum(bcum + src, 0.0)), axis=-1, keepdims=True)
            diag = jnp.where(col_in_sub == sg, row_sum, diag)
        yield from STAGE_GAP
        a = jnp.where(diag_mask, diag, 0.0)
        for li in range(len(LEVELS)):
            a = jnp.where(level_masks[li], lvl[li], a)
        st = sat_ref[b, hh]
        o = _dot(a.astype(BF16), vb) + _dot_nt(q_in, st.astype(BF16))
        sat_ref[b, hh] = st * jnp.exp2(b_last) + upd
        yield from STAGE_GAP
        o = o * lax.rsqrt(jnp.mean(o * o, axis=-1, keepdims=True) + EPS)
        o = o * ga[:, sl] * jax.nn.sigmoid(proj_ref[rs, col(3, hh)])
        ob_ref[rs, sl] = o.astype(BF16)

    def ret_head(b, rs, t0, hh):
        sl = slice(hh * D_HEAD, (hh + 1) * D_HEAD)
        cs = cos_ref[pl.ds(t0, CHUNK), :]
        sn = sin_ref[pl.ds(t0, CHUNK), :]
        qh = proj_ref[rs, col(4, hh)]
        kh = proj_ref[rs, col(5, hh)]
        qr = qh * cs + pltpu.roll(qh, D_HEAD // 2, 1) * sn
        kr = (kh * cs + pltpu.roll(kh, D_HEAD // 2, 1) * sn) * (D_HEAD ** -0.5)
        qb = qr.astype(BF16)
        vh = proj_ref[rs, col(6, hh)].astype(BF16)
        sc = _dot_nt(qb, kr.astype(BF16))
        upd = _dot_tn((kr * kdec_ref[hh]).astype(BF16), vh)
        yield from STAGE_GAP
        s_r = sr_ref[b, hh]
        o = _dot((sc * intra_ref[hh]).astype(BF16), vh) + _dot(qb, s_r.astype(BF16)) * inter_ref[hh]
        sr_ref[b, hh] = s_r * sdec_ref[hh] + upd
        yield from STAGE_GAP
        mu = jnp.mean(o, axis=-1, keepdims=True)
        oc = o - mu
        var = jnp.mean(oc * oc, axis=-1, keepdims=True)
        o = oc * lax.rsqrt(var + EPS) * gr[:, sl]
        g = proj_ref[rs, col(7, hh)]
        o = o * (g * jax.nn.sigmoid(g))
        ob_ref[rs, col(1, hh)] = o.astype(BF16)

    def chunk_body(i, carry):
        heads = []
        for j in range(CHUNK_UNROLL):
            idx = i * CHUNK_UNROLL + j
            b = idx // n_chunks
            rs = pl.ds(pl.multiple_of(idx * CHUNK, CHUNK), CHUNK)
            t0 = pl.multiple_of((idx - b * n_chunks) * CHUNK, CHUNK)
            for hh in range(N_HEADS):
                heads.append(hgrn_head(b, rs, hh))
                heads.append(ret_head(b, rs, t0, hh))
        live = []
        while heads or live:
            if heads:
                live.append(heads.pop(0))
            nxt = []
            for h in live:
                try:
                    next(h)
                    nxt.append(h)
                except StopIteration:
                    pass
            live = nxt
        return carry

    lax.fori_loop(0, n_iter, chunk_body, 0)
    pltpu.make_async_copy(hbuf_ref, xs_ref.at[pl.ds(0, ROW_TILE * TOK_SUB), :], scsem).wait()

    @pl.when(tt == pl.num_programs(1) - 1)
    def _():
        for b in range(bb_n):
            for hh in range(N_HEADS):
                sa_ref[b, hh] = sat_ref[b, hh].T

    out = _dot(ob_ref[...], wout_ref[...]).reshape(bb_n, tq, D_MODEL)
    x1 = x + g1_ref[...] * out
    x1_ref[...] = x1
    ms2 = jnp.mean(x1 * x1, axis=-1, keepdims=True)
    h2 = x1 * lax.rsqrt(ms2 + EPS) * nm_ref[...] * (1.0 + sc2_ref[...]) + sh2_ref[...]
    h2b = h2.reshape(rows, D_MODEL).astype(BF16)
    lt = _dot_nt(wrt_ref[...], h2b)
    route = _route(lt, br_ref[...])
    bits = lax.bitcast_convert_type(h2b.astype(F32), jnp.uint32)
    words = (bits[:, 0:D_MODEL // 2] >> 16) | (bits[:, D_MODEL // 2:D_MODEL] & jnp.uint32(0xFFFF0000))
    for s in range(PACK_SUB):
        hbuf_ref[pl.ds(s, ROW_TILE, stride=TOK_SUB), :] = words[:, s * 128:(s + 1) * 128]
    rec = jnp.concatenate([route, jnp.zeros((128 - ROUTE_ROWS, rows), F32)], axis=0).T
    hbuf_ref[pl.ds(PACK_SUB, ROW_TILE, stride=TOK_SUB), :] = lax.bitcast_convert_type(rec, jnp.uint32)

    combo = route[0:1, :]
    ids = lax.broadcasted_iota(jnp.int32, (COMBO_ROWS, ROW_TILE), 0).astype(F32)
    onehot = (ids == combo).astype(F32)
    cum = _dot(onehot.astype(BF16), triu_ref[...])
    total = dsp_ref[0][:, 0:1]
    cur = dsp_ref[1][:, 0:1]
    n_alloc = dsp_ref[2][:, 0:1]
    n_c = cum[:, ROW_TILE - 1:ROW_TILE]
    inv_blk = 1.0 / MOE_BLOCK
    before = jnp.floor((total + float(MOE_BLOCK - 1)) * inv_blk)
    after = jnp.floor((total + n_c + float(MOE_BLOCK - 1)) * inv_blk)
    new = after - before
    new_wide = jnp.broadcast_to(new, (COMBO_ROWS, 128))
    first_new = n_alloc + _dot(lstrict_ref[...], new_wide.astype(BF16))[:, 0:1]
    ordinal = total + cum - 1.0
    lblk = jnp.floor(ordinal * inv_blk)
    offs = ordinal - lblk * float(MOE_BLOCK)
    blk = jnp.where(lblk < before, cur, first_new + (lblk - before))
    pos = jnp.sum(onehot * (blk * float(MOE_BLOCK) + offs), axis=0, keepdims=True)
    pos_i = jnp.broadcast_to(pos, (ROUTE_ROWS, ROW_TILE)).astype(jnp.int32)
    pos3_ref[0] = pos_i
    posv_ref[...] = pos_i
    wide = (COMBO_ROWS, 128)
    dsp_ref[0] = jnp.broadcast_to(total + n_c, wide)
    dsp_ref[1] = jnp.broadcast_to(jnp.where(new > 0.0, first_new + new - 1.0, cur), wide)
    dsp_ref[2] = jnp.broadcast_to(n_alloc + jnp.sum(new, axis=0, keepdims=True), wide)
    blk_id = lax.broadcasted_iota(jnp.int32, (COMBO_ROWS, BLOCK_LANES), 1).astype(F32)
    pair_id = lax.broadcasted_iota(jnp.int32, (COMBO_ROWS, BLOCK_LANES), 0).astype(F32)
    fresh = (blk_id >= first_new) & (blk_id < first_new + new)
    owner = jnp.sum(jnp.where(fresh, pair_id, 0.0), axis=0, keepdims=True)
    bc_ref[...] += jnp.broadcast_to(owner, bc_ref.shape)
    pltpu.make_async_copy(posv_ref.at[0], poss_ref, psem).start()

    @pl.when(step == last_step)
    def _():
        pltpu.make_async_copy(posv_ref.at[0], poss_ref, psem).wait()

        def issue(g, carry):
            r0 = pl.multiple_of(g * ISSUE_UNROLL, ISSUE_UNROLL)
            for j in range(ISSUE_UNROLL):
                _tok_copy(hbuf_ref, r0 + j, xs_ref, poss_ref[r0 + j], scsem).start(priority=j % 2)
            return carry
        lax.fori_loop(0, ROW_TILE // ISSUE_UNROLL, issue, 0)
        pltpu.make_async_copy(hbuf_ref, xs_ref.at[pl.ds(0, ROW_TILE * TOK_SUB), :], scsem).wait()
        if gather:
            pltpu.make_async_copy(ys_ref.at[pl.ds(0, ROW_TILE * TOK_SUB), :], gbuf_ref.at[1 - slot],
                                  gsem.at[1 - slot]).wait()


def _mixer(x, mods, consts, lw, s_a0, s_r0, offset, dispatch, prev=None):
    batch, seq, _ = x.shape
    bb_n, tq = _tiling(batch, seq)
    nb, nt = batch // bb_n, seq // tq
    cos, sin = _rope_tables(seq, offset)
    sh1, sc1, g1, sh2, sc2 = mods
    dsp, bc, xs = dispatch
    gather = prev is not None
    if gather:
        pos, g2p, ys = prev
    else:
        pos = jnp.zeros((1,), jnp.int32)
        g2p = jnp.zeros((batch, 1, D_MODEL), F32)
        ys = jnp.zeros((ROUTE_ROWS * TOK_SUB, 128), F32)

    def per_b(shape):
        return pl.BlockSpec(shape, lambda b, t, pos: (b,) + (0,) * (len(shape) - 1))

    def const(shape):
        return pl.BlockSpec(shape, lambda b, t, pos: (0,) * len(shape))

    any_spec = pl.BlockSpec(memory_space=pl.ANY)
    mod_spec = per_b((bb_n, 1, D_MODEL))
    state_spec = per_b((bb_n, N_HEADS, D_HEAD, D_HEAD))
    rope_spec = pl.BlockSpec((tq, D_HEAD), lambda b, t, pos: (t, 0))
    operands = [
        (x, pl.BlockSpec((bb_n, tq, D_MODEL), lambda b, t, pos: (b, t, 0))),
        (g2p, mod_spec), (ys, any_spec),
        (sh1, mod_spec), (sc1, mod_spec), (g1, mod_spec), (sh2, mod_spec), (sc2, mod_spec),
        (lw["na"], const((1, D_MODEL))), (lw["nm"], const((1, D_MODEL))),
        (lw["w_in"], const((D_MODEL, IN_COLS))), (lw["w_out"], const((2 * G_WIDTH, D_MODEL))),
        (lw["lb"], const((1, G_WIDTH))), (lw["ga"], const((1, G_WIDTH))), (lw["gr"], const((1, G_WIDTH))),
        (consts["wrt"], const((N_EXPERTS, D_MODEL))), (consts["br"], const((N_EXPERTS, 1))),
        (cos, rope_spec), (sin, rope_spec),
        (consts["intra"], const((N_HEADS, CHUNK, CHUNK))), (consts["inter"], const((N_HEADS, CHUNK, D_HEAD))),
        (consts["kdec"], const((N_HEADS, CHUNK, D_HEAD))), (consts["sdec"], const((N_HEADS, 1, D_HEAD))),
        (consts["tri"], const((CHUNK, 3 * CHUNK))), (consts["triu"], const((ROW_TILE, ROW_TILE))),
        (consts["lstrict"], const((COMBO_ROWS, COMBO_ROWS))),
        (s_a0, any_spec), (s_r0, any_spec),
        (dsp, const(dsp.shape)), (bc, const(bc.shape)), (xs, any_spec),
    ]
    out_specs = [
        pl.BlockSpec((bb_n, tq, D_MODEL), lambda b, t, pos: (b, t, 0)),
        pl.BlockSpec((1, ROUTE_ROWS, ROW_TILE), lambda b, t, pos: (b * nt + t, 0, 0)),
        state_spec, state_spec,
        const(dsp.shape), const(bc.shape), any_spec,
    ]
    out_shape = [
        jax.ShapeDtypeStruct((batch, seq, D_MODEL), F32),
        jax.ShapeDtypeStruct((nb * nt, ROUTE_ROWS, ROW_TILE), jnp.int32),
        jax.ShapeDtypeStruct((batch, N_HEADS, D_HEAD, D_HEAD), F32),
        jax.ShapeDtypeStruct((batch, N_HEADS, D_HEAD, D_HEAD), F32),
        jax.ShapeDtypeStruct(dsp.shape, F32),
        jax.ShapeDtypeStruct(bc.shape, F32),
        jax.ShapeDtypeStruct(xs.shape, xs.dtype),
    ]
    gather_rows = ROW_TILE if gather else ROUTE_ROWS
    x1, pos3, s_a, s_r, dsp, bc, xs = pl.pallas_call(
        functools.partial(_mixer_kernel, bb_n=bb_n, tq=tq, gather=gather, dummy_row=xs.shape[0] // TOK_SUB - ROW_TILE),
        grid_spec=pltpu.PrefetchScalarGridSpec(
            num_scalar_prefetch=1,
            grid=(nb, nt),
            in_specs=[spec for _, spec in operands],
            out_specs=out_specs,
            scratch_shapes=[
                pltpu.VMEM((ROW_TILE, IN_COLS), F32),
                pltpu.VMEM((ROW_TILE, 2 * G_WIDTH), BF16),
                pltpu.VMEM((bb_n, N_HEADS, D_HEAD, D_HEAD), F32),
                pltpu.VMEM((2, gather_rows * TOK_SUB, 128), F32),
                pltpu.VMEM((ROW_TILE * TOK_SUB, 128), jnp.uint32),
                pltpu.VMEM((ROUTE_ROWS, ROW_TILE), jnp.int32),
                pltpu.SMEM((ROW_TILE,), jnp.int32),
                pltpu.SemaphoreType.DMA((2,)),
                pltpu.SemaphoreType.DMA((2,)),
                pltpu.SemaphoreType.DMA,
                pltpu.SemaphoreType.DMA,
            ],
        ),
        out_shape=out_shape,
        input_output_aliases={len(operands): 6},
        compiler_params=pltpu.CompilerParams(
            dimension_semantics=("arbitrary", "arbitrary"), vmem_limit_bytes=VMEM_LIMIT),
        name="mixer",
    )(pos, *[arr for arr, _ in operands])
    return x1, pos3[:, 0, :].reshape(batch * seq), s_a, s_r, (dsp, bc, xs)


def _rope_tables(seq, offset):
    half = D_HEAD // 2
    freq = ROPE_BASE ** (-jnp.arange(half, dtype=F32) / half)
    ang = (jnp.arange(seq, dtype=F32) + offset)[:, None] * freq[None, :]
    cos, sin = jnp.cos(ang), jnp.sin(ang)
    return jnp.concatenate([cos, cos], axis=-1), jnp.concatenate([-sin, sin], axis=-1)


def _retention_tables():
    log_g = jnp.log(1.0 - 2.0 ** (-5.0 - jnp.arange(N_HEADS, dtype=F32)))
    idx = jnp.arange(CHUNK, dtype=F32)
    rel = idx[:, None] - idx[None, :]
    intra = jnp.where(rel >= 0, jnp.exp(jnp.maximum(rel, 0.0)[None] * log_g[:, None, None]), 0.0)
    inter = jnp.exp((idx[None, :] + 1.0) * log_g[:, None])[..., None]
    kdec = jnp.exp((CHUNK - 1.0 - idx[None, :]) * log_g[:, None])[..., None]
    sdec = jnp.exp(CHUNK * log_g)[:, None, None]
    wide = (N_HEADS, CHUNK, D_HEAD)
    return (intra, jnp.broadcast_to(inter, wide), jnp.broadcast_to(kdec, wide),
            jnp.broadcast_to(sdec, (N_HEADS, 1, D_HEAD)))


def _tok_copy(src_ref, src_tok, dst_ref, dst_tok, sem):
    src = src_ref.at[pl.ds(pl.multiple_of(src_tok * TOK_SUB, TOK_SUB), TOK_SUB), :]
    dst = dst_ref.at[pl.ds(pl.multiple_of(dst_tok * TOK_SUB, TOK_SUB), TOK_SUB), :]
    return pltpu.make_async_copy(src, dst, sem)


def _tok_columns(ref, n_tok, n_sub, lead=()):
    return jnp.concatenate([ref[lead + (pl.ds(s, n_tok, stride=TOK_SUB), slice(None))] for s in range(n_sub)], axis=1)


def _moe_kernel(order_ref, ea_ref, eb_ref, nu_ref, xs_ref, w1a, w3a, w2a, w1b, w3b, w2b, ys_ref):
    del order_ref, ea_ref, eb_ref
    j = pl.program_id(0)

    @pl.when(j < nu_ref[0])
    def _():
        words = _tok_columns(xs_ref, MOE_BLOCK, PACK_SUB)
        lo = lax.bitcast_convert_type(words << 16, F32)
        hi = lax.bitcast_convert_type(words & jnp.uint32(0xFFFF0000), F32)
        x = jnp.concatenate([lo, hi], axis=1).astype(BF16)
        rec = lax.bitcast_convert_type(xs_ref[pl.ds(PACK_SUB, MOE_BLOCK, stride=TOK_SUB), :], F32)
        w_lo = rec[:, 1:2]
        w_hi = rec[:, 2:3]

        def ffn(w1, w3, w2):
            h1 = _dot(x, w1[0])
            hid = (h1 * jax.nn.sigmoid(h1)) * _dot(x, w3[0])
            return _dot(hid.astype(BF16), w2[0])

        y = ffn(w1a, w3a, w2a) * w_lo + ffn(w1b, w3b, w2b) * w_hi
        for s in range(TOK_SUB):
            ys_ref[pl.ds(s, MOE_BLOCK, stride=TOK_SUB), :] = y[:, s * 128:(s + 1) * 128]

    @pl.when(j >= nu_ref[0])
    def _():
        ys_ref[...] = jnp.zeros_like(ys_ref)


def _moe(order, ea, eb, n_used, xs, w1, w3, w2, layer):
    n_blocks = order.shape[0]
    blk_rows = MOE_BLOCK * TOK_SUB

    def w_spec(which):
        if which == 0:
            return pl.BlockSpec((None, 1, D_MODEL, D_MODEL), lambda j, order, ea, eb, nu: (layer, ea[j], 0, 0))
        return pl.BlockSpec((None, 1, D_MODEL, D_MODEL), lambda j, order, ea, eb, nu: (layer, eb[j], 0, 0))

    return pl.pallas_call(
        _moe_kernel,
        grid_spec=pltpu.PrefetchScalarGridSpec(
            num_scalar_prefetch=4,
            grid=(n_blocks,),
            in_specs=[
                pl.BlockSpec((blk_rows, 128),
                             lambda j, order, ea, eb, nu: (order[jnp.maximum(jnp.minimum(j, nu[0] - 1), 0)], 0)),
                w_spec(0), w_spec(0), w_spec(0), w_spec(1), w_spec(1), w_spec(1),
            ],
            out_specs=pl.BlockSpec((blk_rows, 128), lambda j, order, ea, eb, nu: (order[j], 0)),
        ),
        out_shape=jax.ShapeDtypeStruct((n_blocks * blk_rows, 128), F32),
        compiler_params=pltpu.CompilerParams(dimension_semantics=("arbitrary",), vmem_limit_bytes=VMEM_LIMIT),
        name="moe",
    )(order, ea, eb, n_used, xs, w1, w3, w2, w1, w3, w2)


def _combine_kernel(pos_ref, x1_ref, g2_ref, nf_ref, ys_ref, out_ref, buf_ref, sem):
    step = pl.program_id(0) * pl.num_programs(1) + pl.program_id(1)
    last_step = pl.num_programs(0) * pl.num_programs(1) - 1
    slot = step % 2

    def fetch_tile(tile, dst_slot):
        def issue(g, carry):
            r0 = pl.multiple_of(g * ISSUE_UNROLL, ISSUE_UNROLL)
            for j in range(ISSUE_UNROLL):
                _tok_copy(ys_ref, pos_ref[tile * ROW_TILE + r0 + j], buf_ref.at[dst_slot], r0 + j,
                          sem.at[dst_slot]).start(priority=j % 2)
            return carry
        lax.fori_loop(0, ROW_TILE // ISSUE_UNROLL, issue, 0)

    def wait_tile(dst_slot):
        pltpu.make_async_copy(ys_ref.at[pl.ds(0, ROW_TILE * TOK_SUB), :], buf_ref.at[dst_slot],
                              sem.at[dst_slot]).wait()

    @pl.when(step == 0)
    def _():
        fetch_tile(0, 0)

    fetch_tile(jnp.minimum(step + 1, last_step), 1 - slot)
    wait_tile(slot)
    y = _tok_columns(buf_ref, ROW_TILE, TOK_SUB, (slot,))
    x2 = x1_ref[...] + g2_ref[...] * y.reshape(x1_ref.shape)
    out_ref[...] = x2 * lax.rsqrt(jnp.mean(x2 * x2, axis=-1, keepdims=True) + EPS) * nf_ref[...]

    @pl.when(step == last_step)
    def _():
        wait_tile(1 - slot)


def _combine(pos, x1, g2, nf, ys):
    batch, seq, _ = x1.shape
    bb_n, tq = _tiling(batch, seq)
    tok_spec = pl.BlockSpec((bb_n, tq, D_MODEL), lambda b, t, pos: (b, t, 0))
    return pl.pallas_call(
        _combine_kernel,
        grid_spec=pltpu.PrefetchScalarGridSpec(
            num_scalar_prefetch=1,
            grid=(batch // bb_n, seq // tq),
            in_specs=[
                tok_spec,
                pl.BlockSpec((bb_n, 1, D_MODEL), lambda b, t, pos: (b, 0, 0)),
                pl.BlockSpec((1, D_MODEL), lambda b, t, pos: (0, 0)),
                pl.BlockSpec(memory_space=pl.ANY),
            ],
            out_specs=tok_spec,
            scratch_shapes=[pltpu.VMEM((2, ROW_TILE * TOK_SUB, 128), F32), pltpu.SemaphoreType.DMA((2,))],
        ),
        out_shape=jax.ShapeDtypeStruct(x1.shape, F32),
        compiler_params=pltpu.CompilerParams(dimension_semantics=("arbitrary", "arbitrary")),
        name="combine",
    )(pos, x1, g2, nf, ys)


def _block_order(dsp, bc, n_blocks):
    n_used = dsp[2, 0, 0].astype(jnp.int32)
    blk = jnp.arange(n_blocks, dtype=jnp.int32)
    owner = jnp.where(blk < n_used, bc[0, :n_blocks].astype(jnp.int32), N_COMBO)
    earlier = (owner[None, :] < owner[:, None]) | ((owner[None, :] == owner[:, None]) & (blk[None, :] < blk[:, None]))
    rank = jnp.sum(earlier.astype(jnp.int32), axis=1)
    order = jnp.sum(jnp.where(rank[None, :] == blk[:, None], blk[None, :], 0), axis=1).astype(jnp.int32)
    combo = jnp.minimum(owner[order], N_COMBO - 1)
    last = combo[jnp.maximum(n_used - 1, 0)]
    combo = jnp.where(blk < n_used, combo, last)
    lo_tab = jnp.array([0, 0, 0, 1, 1, 2], jnp.int32)
    hi_tab = jnp.array([1, 2, 3, 2, 3, 3], jnp.int32)
    grp = combo // PAIRS_PER_GROUP
    pair = combo % PAIRS_PER_GROUP
    return order, grp * EXP_PER_GROUP + lo_tab[pair], grp * EXP_PER_GROUP + hi_tab[pair], n_used.reshape(1)


def kernel(x_prompt, x_sample, state_hgrn, state_ret, c_prompt, c_sample, w_in, w_out, lb_logits, hgrn_norm,
           ret_norm, norm_attn, norm_moe, w_ada, b_ada, w_router, b_router, w1, w3, w2, norm_f):
    bp, tp, _ = x_prompt.shape
    bs, ts, _ = x_sample.shape
    n_p, n_s = bp * tp, bs * ts
    n_tok = n_p + n_s
    assert n_tok % MOE_BLOCK == 0
    n_blocks = n_tok // MOE_BLOCK + N_COMBO
    assert n_blocks <= BLOCK_LANES

    lb_cum = jnp.cumsum(jax.nn.softmax(lb_logits.astype(F32), axis=0), axis=0)
    lb_all = lb_cum - lb_cum[0:1]

    mod = _ada(jnp.concatenate([c_prompt, c_sample], axis=0), w_ada, b_ada)

    intra, inter, kdec, sdec = _retention_tables()
    consts = {
        "wrt": w_router.T.astype(BF16), "br": b_router.astype(F32).reshape(N_EXPERTS, 1),
        "intra": intra, "inter": inter, "kdec": kdec, "sdec": sdec,
        "tri": jnp.tile(jnp.tril(jnp.ones((CHUNK, CHUNK), BF16)), (1, 3)),
        "triu": jnp.triu(jnp.ones((ROW_TILE, ROW_TILE), BF16)),
        "lstrict": jnp.tril(jnp.ones((COMBO_ROWS, COMBO_ROWS), BF16), k=-1),
    }

    w1_b, w3_b, w2_b = w1.astype(BF16), w3.astype(BF16), w2.astype(BF16)

    xp, xs_ = x_prompt, x_sample
    sa_p = jnp.zeros((bp, N_HEADS, D_HEAD, D_HEAD), F32)
    sr_p = jnp.zeros((bp, N_HEADS, D_HEAD, D_HEAD), F32)
    new_states = []
    prev_p = prev_s = None
    for l in range(DEPTH):
        lw = {
            "na": norm_attn[l].reshape(1, D_MODEL), "nm": norm_moe[l].reshape(1, D_MODEL),
            "w_in": w_in[l].astype(BF16), "w_out": w_out[l].astype(BF16),
            "lb": lb_all[l].reshape(1, G_WIDTH), "ga": hgrn_norm[l].reshape(1, G_WIDTH),
            "gr": ret_norm[l].reshape(1, G_WIDTH),
        }
        parts = jnp.split(mod[l], 6, axis=-1)
        mods_p = [p[:bp, None, :] for p in parts]
        mods_s = [p[bp:, None, :] for p in parts]
        dsp0 = jnp.stack([jnp.zeros((COMBO_ROWS, 128), F32), jnp.full((COMBO_ROWS, 128), -1.0, F32),
                          jnp.zeros((COMBO_ROWS, 128), F32)])
        dispatch = (dsp0, jnp.zeros((ROUTE_ROWS, BLOCK_LANES), F32),
                    jnp.zeros(((n_blocks * MOE_BLOCK + ROW_TILE) * TOK_SUB, 128), jnp.uint32))
        x1p, pos_p, sa_np, sr_np, dispatch = _mixer(xp, mods_p[:5], consts, lw, sa_p, sr_p, 0, dispatch, prev_p)
        x1s, pos_s, sa_ns, sr_ns, dispatch = _mixer(xs_, mods_s[:5], consts, lw, state_hgrn[l].astype(F32),
                                                    state_ret[l].astype(F32), PAST_LEN, dispatch, prev_s)
        new_states.append((sa_np, sr_np, sa_ns, sr_ns))

        dsp, bc, xs = dispatch
        order, ea, eb, n_used = _block_order(dsp, bc, n_blocks)
        ys = _moe(order, ea, eb, n_used, xs, w1_b, w3_b, w2_b, l)

        xp, xs_ = x1p, x1s
        prev_p = (pos_p, mods_p[5], ys)
        prev_s = (pos_s, mods_s[5], ys)

    nf = norm_f.reshape(1, D_MODEL)
    xp = _combine(prev_p[0], xp, prev_p[1], nf, prev_p[2])
    xs_ = _combine(prev_s[0], xs_, prev_s[1], nf, prev_s[2])

    sa_prompt = jnp.stack([s[0] for s in new_states])
    sr_prompt = jnp.stack([s[1] for s in new_states])
    sa_sample = jnp.stack([s[2] for s in new_states])
    sr_sample = jnp.stack([s[3] for s in new_states])
    return (xp, xs_, sa_prompt, sr_prompt, sa_sample, sr_sample)
```
